```python
import jax, jax.numpy as jnp
from jax import lax
import numpy as np

D_MODEL = 1024
BATCH = 2
SEQ = 8192
DEPTH = 2
DEC_BATCH = 32
DEC_SEQ = 64
PAST_LEN = 4096

CHUNK = 64
HG_HEADS = 4
HG_KDIM = 128
HG_VDIM = 128
HG_FDIM = HG_HEADS * HG_KDIM
HG_WIDTH = HG_HEADS * HG_VDIM
GLA_BLOCK = CHUNK // 4
FOX_HEADS = 8
FOX_HD = 64
FOX_WIDTH = FOX_HEADS * FOX_HD
Q_BLOCK = 128
MIX_WIDTH = HG_WIDTH + FOX_WIDTH
IN_WIDTH = 2 * HG_FDIM + 2 * HG_WIDTH + 3 * FOX_WIDTH + FOX_HEADS
N_EXPERTS = 32
TOP_K = 4
D_FF = D_MODEL
SWIGLU_ALPHA = 1.702
SWIGLU_LIMIT = 7.0
MOE_BLOCK = 128
EPS = 1e-6
NEG_INF = -1e30
F_FLOOR = 1e-30

kernel_name = 'hymba_hgrn2_fox_moe_stream_step'


def rmsnorm(x, g):
    xf = x.astype(jnp.float32)
    y = xf * lax.rsqrt(jnp.mean(xf * xf, axis=-1, keepdims=True) + EPS)
    return (y * g.astype(jnp.float32)).astype(x.dtype)


def hgrn2_lower_bounds(lb_logits):
    p = jax.nn.softmax(lb_logits.astype(jnp.float32), axis=0)
    return jnp.cumsum(p, axis=0) - p[0:1]


def gla_recurrence(q, k, v, log_f, S0):
    B, T, H, K = q.shape
    V = v.shape[-1]
    L = GLA_BLOCK
    pad = (-T) % L
    if pad:
        padw = ((0, 0), (0, pad), (0, 0), (0, 0))
        q, k, v, log_f = [jnp.pad(a, padw) for a in (q, k, v, log_f)]
    n = (T + pad) // L

    def blocks(a):
        return a.reshape(B, n, L, H, a.shape[-1]).transpose(1, 0, 3, 2, 4)

    causal = jnp.tril(jnp.ones((L, L), bool))[:, :, None]

    def step(S, blk):
        qc, kc, vc, gc = blk
        b = jnp.cumsum(gc, axis=2)
        o = jnp.einsum('bhtk,bhkv->bhtv', qc * jnp.exp(b), S)
        dec = jnp.exp(jnp.where(causal, b[:, :, :, None, :] - b[:, :, None, :, :], NEG_INF))
        att = jnp.sum(qc[:, :, :, None, :] * kc[:, :, None, :, :] * dec, axis=-1)
        o = o + jnp.einsum('bhts,bhsv->bhtv', att, vc)
        b_end = b[:, :, -1:, :]
        S = jnp.exp(b_end[:, :, 0, :, None]) * S + jnp.einsum('bhsk,bhsv->bhkv', kc * jnp.exp(b_end - b), vc)
        return S, o

    S, o = lax.scan(step, S0, (blocks(q), blocks(k), blocks(v), blocks(log_f)))
    o = o.transpose(1, 0, 3, 2, 4).reshape(B, n * L, H, V)[:, :T]
    return o, S


def fox_attend(q, k, v, c_q, c_k, q_pos, k_pos):
    s = jnp.einsum('bhqd,bhkd->bhqk', q, k).astype(jnp.float32) * (FOX_HD ** -0.5)
    s = s + c_q[..., :, None] - c_k[..., None, :]
    s = jnp.where(k_pos[None, :] <= q_pos[:, None], s, NEG_INF)
    p = jax.nn.softmax(s, axis=-1)
    return jnp.einsum('bhqk,bhkd->bhqd', p.astype(v.dtype), v)


def fox_prompt(q, k, v, lf):
    B, H, S, Dh = q.shape
    c = jnp.cumsum(lf, axis=-1)
    nb = S // Q_BLOCK
    qb = q.reshape(B, H, nb, Q_BLOCK, Dh).transpose(2, 0, 1, 3, 4)
    cb = c.reshape(B, H, nb, Q_BLOCK).transpose(2, 0, 1, 3)
    k_pos = jnp.arange(S)

    def block(args):
        qi, ci, i = args
        return fox_attend(qi, k, v, ci, c, i * Q_BLOCK + jnp.arange(Q_BLOCK), k_pos)

    o = lax.map(block, (qb, cb, jnp.arange(nb)))
    return o.transpose(1, 2, 0, 3, 4).reshape(B, H, S, Dh)


def token_mixer(h, w_in_l, lb_l, hg_norm_l, fox_fb_l, w_out_l, hg_state, fox_past):
    B, T, _ = h.shape
    f32 = jnp.float32
    z = h @ w_in_l
    o1 = HG_FDIM
    o2 = o1 + HG_FDIM
    o3 = o2 + HG_WIDTH
    o4 = o3 + HG_WIDTH
    o5 = o4 + FOX_WIDTH
    o6 = o5 + FOX_WIDTH
    o7 = o6 + FOX_WIDTH
    hq, hf, hi, hg = z[..., :o1], z[..., o1:o2], z[..., o2:o3], z[..., o3:o4]
    fq, fk, fv, ff = z[..., o4:o5], z[..., o5:o6], z[..., o6:o7], z[..., o7:]

    q = jax.nn.silu(hq.astype(f32)).reshape(B, T, HG_HEADS, HG_KDIM) * (HG_KDIM ** -0.5)
    zf = hf.astype(f32).reshape(B, T, HG_HEADS, HG_KDIM)
    lb = lb_l.reshape(HG_HEADS, HG_KDIM)
    f = lb + (1.0 - lb) * jax.nn.sigmoid(zf)
    log_f = jnp.log(jnp.maximum(f, F_FLOOR))
    kk = (1.0 - lb) * jax.nn.sigmoid(-zf)
    vv = hi.astype(f32).reshape(B, T, HG_HEADS, HG_VDIM)
    if hg_state is None:
        S0 = jnp.zeros((B, HG_HEADS, HG_KDIM, HG_VDIM), f32)
    else:
        S0 = hg_state.astype(f32)
    o_hg, S_new = gla_recurrence(q, kk, vv, log_f, S0)
    o_hg = rmsnorm(o_hg, hg_norm_l) * jax.nn.silu(hg.astype(f32)).reshape(B, T, HG_HEADS, HG_VDIM)
    o_hg = o_hg.reshape(B, T, HG_WIDTH).astype(h.dtype)

    def heads(a):
        return a.reshape(B, T, FOX_HEADS, FOX_HD).transpose(0, 2, 1, 3)
    fqh, fkh, fvh = heads(fq), heads(fk), heads(fv)
    lf = jax.nn.log_sigmoid(ff.astype(f32) + fox_fb_l.astype(f32)).transpose(0, 2, 1)
    if fox_past is None:
        o_fx = fox_prompt(fqh, fkh, fvh, lf)
    else:
        kc, vc, lfc = fox_past
        P = kc.shape[2]
        k_all = jnp.concatenate([kc.astype(fkh.dtype), fkh], axis=2)
        v_all = jnp.concatenate([vc.astype(fvh.dtype), fvh], axis=2)
        c_all = jnp.cumsum(jnp.concatenate([lfc.astype(f32), lf], axis=-1), axis=-1)
        o_fx = fox_attend(fqh, k_all, v_all, c_all[..., P:], c_all, P + jnp.arange(T), jnp.arange(P + T))
    o_fx = o_fx.transpose(0, 2, 1, 3).reshape(B, T, FOX_WIDTH).astype(h.dtype)

    out = jnp.concatenate([o_hg, o_fx], axis=-1) @ w_out_l
    return out, S_new, fkh, fvh, lf


def moe(h, w_router, b_router, w_up, b_up, w_down, b_down):
    B, T, D = h.shape
    N = B * T
    A = N * TOP_K
    xt = h.reshape(N, D)
    logits = (xt @ w_router + b_router).astype(jnp.float32)
    top_v, top_i = lax.top_k(logits, TOP_K)
    gates = jax.nn.softmax(top_v, axis=-1).reshape(A)
    expert = top_i.reshape(A).astype(jnp.int32)
    token = jnp.arange(A, dtype=jnp.int32) // TOP_K
    order = jnp.argsort(expert, stable=True)
    e_s, t_s, g_s = expert[order], token[order], gates[order]
    counts = jax.ops.segment_sum(jnp.ones((A,), jnp.int32), expert, num_segments=N_EXPERTS)
    padded = (counts + MOE_BLOCK - 1) // MOE_BLOCK * MOE_BLOCK
    starts = jnp.cumsum(counts) - counts
    pends = jnp.cumsum(padded)
    pstarts = pends - padded
    dest = pstarts[e_s] + jnp.arange(A, dtype=jnp.int32) - starts[e_s]
    n_blocks = -(-A // MOE_BLOCK) + N_EXPERTS
    P = n_blocks * MOE_BLOCK
    tok_buf = jnp.full((P,), N, jnp.int32).at[dest].set(t_s)
    gate_buf = jnp.zeros((P,), jnp.float32).at[dest].set(g_s)
    blk_e = jnp.minimum(jnp.searchsorted(pends, jnp.arange(n_blocks, dtype=jnp.int32) * MOE_BLOCK, side='right'),
                        N_EXPERTS - 1)
    x_pad = jnp.concatenate([xt, jnp.zeros((1, D), xt.dtype)], axis=0)
    xb = x_pad[tok_buf].reshape(n_blocks, MOE_BLOCK, D)

    def expert_block(args):
        xblk, e = args
        u = xblk @ w_up[e] + b_up[e]
        glu = jnp.minimum(u[:, 0::2], SWIGLU_LIMIT)
        lin = jnp.clip(u[:, 1::2], -SWIGLU_LIMIT, SWIGLU_LIMIT)
        a = glu * jax.nn.sigmoid(SWIGLU_ALPHA * glu) * (lin + 1.0)
        return a @ w_down[e] + b_down[e]

    yb = lax.map(expert_block, (xb, blk_e)).reshape(P, D)
    y = jnp.zeros((N + 1, D), yb.dtype).at[tok_buf].add(yb * gate_buf[:, None].astype(yb.dtype))
    return y[:N].reshape(B, T, D).astype(h.dtype)


def trunk(x, c, hg_states, fox_k, fox_v, fox_lf, w_ada, b_ada, norm_mix, norm_ffn, w_in, lb_logits, hg_norm,
          fox_fb, w_out, w_router, b_router, w_up, b_up, w_down, b_down, norm_final):
    lbs = hgrn2_lower_bounds(lb_logits)
    new_S, new_k, new_v, new_lf = [], [], [], []
    for l in range(DEPTH):
        mod = (jax.nn.silu(c) @ w_ada[l] + b_ada[l])[:, None, :]
        sh1, sc1, g1, sh2, sc2, g2 = jnp.split(mod, 6, axis=-1)
        h = rmsnorm(x, norm_mix[l]) * (1.0 + sc1) + sh1
        past = None if fox_k is None else (fox_k[l], fox_v[l], fox_lf[l])
        S0 = None if hg_states is None else hg_states[l]
        y, S, k, v, lf = token_mixer(h, w_in[l], lbs[l], hg_norm[l], fox_fb[l], w_out[l], S0, past)
        x = x + g1 * y
        h = rmsnorm(x, norm_ffn[l]) * (1.0 + sc2) + sh2
        x = x + g2 * moe(h, w_router[l], b_router[l], w_up[l], b_up[l], w_down[l], b_down[l])
        new_S.append(S)
        new_k.append(k)
        new_v.append(v)
        new_lf.append(lf)
    return rmsnorm(x, norm_final), jnp.stack(new_S), jnp.stack(new_k), jnp.stack(new_v), jnp.stack(new_lf)


def setup_inputs(seed: int = 0) -> dict:
    key = jax.random.key(seed)
    ks = jax.random.split(key, 24)
    f32 = jnp.float32
    D = D_MODEL

    def nrm(k, shape, s=1.0):
        return jax.random.normal(k, shape, f32) * s

    return {
        'x_prompt': nrm(ks[0], (BATCH, SEQ, D)),
        'x_sample': nrm(ks[1], (DEC_BATCH, DEC_SEQ, D)),
        'c_prompt': nrm(ks[2], (BATCH, D)),
        'c_sample': nrm(ks[3], (DEC_BATCH, D)),
        'state_hgrn': nrm(ks[4], (DEPTH, DEC_BATCH, HG_HEADS, HG_KDIM, HG_VDIM), 0.5),
        'cache_fox_k': nrm(ks[5], (DEPTH, DEC_BATCH, FOX_HEADS, PAST_LEN, FOX_HD)),
        'cache_fox_v': nrm(ks[6], (DEPTH, DEC_BATCH, FOX_HEADS, PAST_LEN, FOX_HD)),
        'cache_fox_lf': jax.nn.log_sigmoid(1.0 + nrm(ks[7], (DEPTH, DEC_BATCH, FOX_HEADS, PAST_LEN), 0.5)),
        'w_ada': nrm(ks[8], (DEPTH, D, 6 * D), 0.5 * D ** -0.5),
        'b_ada': nrm(ks[9], (DEPTH, 6 * D), 0.01),
        'norm_mix': 1.0 + nrm(ks[10], (DEPTH, D), 0.05),
        'norm_ffn': 1.0 + nrm(ks[11], (DEPTH, D), 0.05),
        'w_in': nrm(ks[12], (DEPTH, D, IN_WIDTH), D ** -0.5),
        'lb_logits': nrm(ks[13], (DEPTH, HG_FDIM), 0.5),
        'hg_norm': 1.0 + nrm(ks[14], (DEPTH, HG_VDIM), 0.05),
        'fox_fb': 1.0 + nrm(ks[15], (DEPTH, FOX_HEADS), 0.1),
        'w_out': nrm(ks[16], (DEPTH, MIX_WIDTH, D), MIX_WIDTH ** -0.5),
        'w_router': nrm(ks[17], (DEPTH, D, N_EXPERTS), D ** -0.5),
        'b_router': nrm(ks[18], (DEPTH, N_EXPERTS), 0.01),
        'w_up': nrm(ks[19], (DEPTH, N_EXPERTS, D, 2 * D_FF), D ** -0.5),
        'b_up': nrm(ks[20], (DEPTH, N_EXPERTS, 2 * D_FF), 0.01),
        'w_down': nrm(ks[21], (DEPTH, N_EXPERTS, D_FF, D), D_FF ** -0.5),
        'b_down': nrm(ks[22], (DEPTH, N_EXPERTS, D), 0.01),
        'norm_final': 1.0 + nrm(ks[23], (D,), 0.05),
    }


def reference(x_prompt, x_sample, c_prompt, c_sample, state_hgrn, cache_fox_k, cache_fox_v, cache_fox_lf,
              w_ada, b_ada, norm_mix, norm_ffn, w_in, lb_logits, hg_norm, fox_fb, w_out,
              w_router, b_router, w_up, b_up, w_down, b_down, norm_final):
    y_prompt, s_p, k_p, v_p, lf_p = trunk(
        x_prompt, c_prompt, None, None, None, None,
        w_ada, b_ada, norm_mix, norm_ffn, w_in, lb_logits, hg_norm, fox_fb, w_out,
        w_router, b_router, w_up, b_up, w_down, b_down, norm_final)
    y_sample, s_s, k_s, v_s, lf_s = trunk(
        x_sample, c_sample, state_hgrn, cache_fox_k, cache_fox_v, cache_fox_lf,
        w_ada, b_ada, norm_mix, norm_ffn, w_in, lb_logits, hg_norm, fox_fb, w_out,
        w_router, b_router, w_up, b_up, w_down, b_down, norm_final)
    return (y_prompt, y_sample, s_p, k_p, v_p, lf_p, s_s, k_s, v_s, lf_s)
```

```python
import functools

import jax
import jax.numpy as jnp
from jax import lax
from jax.experimental import pallas as pl
from jax.experimental.pallas import tpu as pltpu

F32 = jnp.float32
BF16 = jnp.bfloat16

HG_HEADS = 4
HG_DIM = 128
HG_WIDTH = HG_HEADS * HG_DIM
FOX_HEADS = 8
FOX_HD = 64
FOX_WIDTH = FOX_HEADS * FOX_HD
FOX_PAIRS = FOX_HEADS // 2
N_EXPERTS = 32
TOP_K = 4
GLA_BLOCK = 16
MOD_GROUP = 64
SWIGLU_ALPHA = 1.702
SWIGLU_LIMIT = 7.0
EPS = 1e-6
NEG_INF = -1e30
F_FLOOR = 1e-30

ROW_TILE = 512
HGRN_CHUNK = 256
ATTN_BLOCK = 256
MOE_TILE = 256
VMEM_LIMIT = 56 * 1024 * 1024


def _params(*sem):
    return pltpu.CompilerParams(dimension_semantics=sem, vmem_limit_bytes=VMEM_LIMIT)


def _split_bf16(a):
    hi = a.astype(BF16)
    lo = (a - hi.astype(F32)).astype(BF16)
    return hi, lo


def _dot(a, b):
    return jnp.dot(a, b, preferred_element_type=F32)


def _dot_nt(a, b):
    return lax.dot_general(a, b, (((1,), (1,)), ((), ())), preferred_element_type=F32)


def _dot_tn(a, b):
    return lax.dot_general(a, b, (((0,), (0,)), ((), ())), preferred_element_type=F32)


def _sigmoid_pair(z):
    e = jnp.exp(-jnp.abs(z))
    r = 1.0 / (1.0 + e)
    er = e * r
    pos = z >= 0.0
    return jnp.where(pos, r, er), jnp.where(pos, er, r)


def _silu(z):
    return z * _sigmoid_pair(z)[0]


def _log_sigmoid(z):
    return jnp.minimum(z, 0.0) - jnp.log(1.0 + jnp.exp(-jnp.abs(z)))


def _rms(x):
    return x * lax.rsqrt(jnp.mean(x * x, axis=-1, keepdims=True) + EPS)


def _modulate(y, scale, shift):
    rows, d = y.shape
    g = rows // MOD_GROUP
    y3 = y.reshape(g, MOD_GROUP, d)
    return (y3 * (1.0 + scale[:, None, :]) + shift[:, None, :]).reshape(rows, d)


def _ada_kernel(c_ref, w_ref, b_ref, o_ref):
    c = c_ref[...]
    s_hi, s_lo = _split_bf16(_silu(c))
    w_hi, w_lo = _split_bf16(w_ref[0])
    o_ref[0] = _dot(s_hi, w_hi) + _dot(s_hi, w_lo) + _dot(s_lo, w_hi) + b_ref[0]


def _ada_call(c, w_ada, b_ada):
    depth, d, d6 = w_ada.shape
    bc = c.shape[0]
    tn = d6 // 4
    return pl.pallas_call(
        _ada_kernel,
        grid=(depth, d6 // tn),
        in_specs=[
            pl.BlockSpec((bc, d), lambda l, j: (0, 0)),
            pl.BlockSpec((1, d, tn), lambda l, j: (l, 0, j)),
            pl.BlockSpec((1, 1, tn), lambda l, j: (l, 0, j)),
        ],
        out_specs=pl.BlockSpec((1, bc, tn), lambda l, j: (l, 0, j)),
        out_shape=jax.ShapeDtypeStruct((depth, bc, d6), F32),
        compiler_params=_params("parallel", "parallel"),
        name="ada_mod",
    )(c, w_ada, b_ada.reshape(depth, 1, d6))


def _pre_kernel(x_ref, mod_ref, nw_ref, whg_ref, wfx_ref, wff_ref, fb_ref,
                zhg_ref, q_ref, kb_ref, vb_ref, k_ref, v_ref, lf_ref, *, nb, tt):
    x = x_ref[...]
    h = _modulate(_rms(x) * nw_ref[...], mod_ref[1], mod_ref[0])
    hb = h.astype(BF16)
    zhg_ref[...] = _dot(hb, whg_ref[...])
    zfx = _dot(hb, wfx_ref[...])
    q = zfx[:, :FOX_WIDTH] * (FOX_HD ** -0.5)
    k = zfx[:, FOX_WIDTH:2 * FOX_WIDTH]
    v = zfx[:, 2 * FOX_WIDTH:]
    q_ref[...] = q.astype(BF16)
    kb_ref[...] = k.astype(BF16)
    vb_ref[...] = v.astype(BF16)
    for b in range(nb):
        for hd in range(FOX_HEADS):
            rows = slice(b * tt, (b + 1) * tt)
            cols = slice(hd * FOX_HD, (hd + 1) * FOX_HD)
            k_ref[b, hd] = k[rows, cols]
            v_ref[b, hd] = v[rows, cols]
    ff = _dot_nt(wff_ref[...], hb)
    lf_ref[...] = _log_sigmoid(ff + fb_ref[...])


def _pre_call(x, modg, nw, whg, wfx, wffT, fb, *, row0, batch, seq):
    d = x.shape[1]
    n = batch * seq
    tm = ROW_TILE
    tt = min(seq, tm)
    nb = tm // tt
    assert n % tm == 0 and row0 % tm == 0 and tm % tt == 0 and seq % tt == 0
    i0 = row0 // tm
    g = tm // MOD_GROUP
    tpb = seq // tt

    def head_major_idx(i):
        return (i // tpb, 0, i % tpb, 0) if nb == 1 else (i, 0, 0, 0)

    tok = lambda w, dt: jax.ShapeDtypeStruct((n, w), dt)
    hm = jax.ShapeDtypeStruct((batch, FOX_HEADS, seq, FOX_HD), F32)
    const = lambda shape: pl.BlockSpec(shape, lambda i: (0,) * len(shape))
    return pl.pallas_call(
        functools.partial(_pre_kernel, nb=nb, tt=tt),
        grid=(n // tm,),
        in_specs=[
            pl.BlockSpec((tm, d), lambda i: (i + i0, 0)),
            pl.BlockSpec((2, g, d), lambda i: (0, i + i0, 0)),
            const((1, d)),
            const(whg.shape), const(wfx.shape), const(wffT.shape), const(fb.shape),
        ],
        out_specs=[
            pl.BlockSpec((tm, 4 * HG_WIDTH), lambda i: (i, 0)),
            pl.BlockSpec((tm, FOX_WIDTH), lambda i: (i, 0)),
            pl.BlockSpec((tm, FOX_WIDTH), lambda i: (i, 0)),
            pl.BlockSpec((tm, FOX_WIDTH), lambda i: (i, 0)),
            pl.BlockSpec((nb, FOX_HEADS, tt, FOX_HD), head_major_idx),
            pl.BlockSpec((nb, FOX_HEADS, tt, FOX_HD), head_major_idx),
            pl.BlockSpec((FOX_HEADS, tm), lambda i: (0, i)),
        ],
        out_shape=[tok(4 * HG_WIDTH, F32), tok(FOX_WIDTH, BF16), tok(FOX_WIDTH, BF16),
                   tok(FOX_WIDTH, BF16), hm, hm, jax.ShapeDtypeStruct((FOX_HEADS, n), F32)],
        compiler_params=_params("parallel"),
        name="pre_mixer",
    )(x, modg, nw, whg, wfx, wffT, fb)


def _hgrn_kernel(z_ref, s0_ref, lb_ref, gn_ref, tri_ref, o_ref, s_ref,
                 st_ref, q_s, k_s, b_s, *, chunk):
    t = pl.program_id(1)
    L = GLA_BLOCK
    W = HG_WIDTH

    @pl.when(t == 0)
    def _():
        for h in range(HG_HEADS):
            st_ref[h] = s0_ref[0, h].T

    lb = lb_ref[...]
    zf = z_ref[:, W:2 * W]
    sig, sig_neg = _sigmoid_pair(zf)
    f = lb + (1.0 - lb) * sig
    g = jnp.log(jnp.maximum(f, F_FLOOR))
    k_s[...] = (1.0 - lb) * sig_neg
    q_s[...] = _silu(z_ref[:, :W]) * (HG_DIM ** -0.5)
    g0 = g.astype(BF16)
    r1 = g - g0.astype(F32)
    g1 = r1.astype(BF16)
    g2 = (r1 - g1.astype(F32)).astype(BF16)
    tri = tri_ref[...]
    cs = tri.shape[0]
    for r in range(chunk // cs):
        rows = slice(r * cs, (r + 1) * cs)
        b_s[rows, :] = _dot(tri, g0[rows]) + _dot(tri, g1[rows]) + _dot(tri, g2[rows])

    row_id = lax.broadcasted_iota(jnp.int32, (L, L), 0)
    col_id = lax.broadcasted_iota(jnp.int32, (L, L), 1)
    gn = gn_ref[...]

    def block(i, carry):
        rows = pl.ds(pl.multiple_of(i * L, L), L)
        for h in range(HG_HEADS):
            cols = slice(h * HG_DIM, (h + 1) * HG_DIM)
            q = q_s[rows, cols]
            kk = k_s[rows, cols]
            b = b_s[rows, cols]
            vv = z_ref[rows, 2 * W + h * HG_DIM:2 * W + (h + 1) * HG_DIM]
            hg = z_ref[rows, 3 * W + h * HG_DIM:3 * W + (h + 1) * HG_DIM]
            st = st_ref[h]
            o = _dot_nt((q * jnp.exp(b)).astype(BF16), st.astype(BF16))
            att = jnp.zeros((L, L), F32)
            for s in range(L):
                dec = jnp.exp(jnp.minimum(b - b[s:s + 1, :], 0.0))
                col = jnp.sum(q * kk[s:s + 1, :] * dec, axis=-1, keepdims=True)
                att = jnp.where(col_id == s, col, att)
            att = jnp.where(row_id >= col_id, att, 0.0)
            vb = vv.astype(BF16)
            o = o + _dot(att.astype(BF16), vb)
            b_end = b[L - 1:L, :]
            kd = kk * jnp.exp(b_end - b)
            st_ref[h] = st * jnp.exp(b_end) + _dot_tn(vb, kd.astype(BF16))
            o_ref[rows, cols] = (_rms(o) * gn * _silu(hg)).astype(BF16)
        return carry

    lax.fori_loop(0, chunk // L, block, 0)

    @pl.when(t == pl.num_programs(1) - 1)
    def _():
        for h in range(HG_HEADS):
            s_ref[0, h] = st_ref[h].T


def _hgrn_call(zhg, s0, lb, gn, *, batch, seq):
    chunk = min(HGRN_CHUNK, seq)
    cs = min(128, chunk)
    assert seq % chunk == 0 and chunk % cs == 0 and cs % GLA_BLOCK == 0
    r = jnp.arange(cs, dtype=jnp.int32)
    tri = ((r[:, None] // GLA_BLOCK == r[None, :] // GLA_BLOCK) & (r[None, :] <= r[:, None])).astype(BF16)
    nt = seq // chunk
    scr = lambda: pltpu.VMEM((chunk, HG_WIDTH), F32)
    return pl.pallas_call(
        functools.partial(_hgrn_kernel, chunk=chunk),
        grid=(batch, nt),
        in_specs=[
            pl.BlockSpec((chunk, 4 * HG_WIDTH), lambda b, t: (b * nt + t, 0)),
            pl.BlockSpec((1, HG_HEADS, HG_DIM, HG_DIM), lambda b, t: (b, 0, 0, 0)),
            pl.BlockSpec((1, HG_WIDTH), lambda b, t: (0, 0)),
            pl.BlockSpec((1, HG_DIM), lambda b, t: (0, 0)),
            pl.BlockSpec((cs, cs), lambda b, t: (0, 0)),
        ],
        out_specs=[
            pl.BlockSpec((chunk, HG_WIDTH), lambda b, t: (b * nt + t, 0)),
            pl.BlockSpec((1, HG_HEADS, HG_DIM, HG_DIM), lambda b, t: (b, 0, 0, 0)),
        ],
        out_shape=[jax.ShapeDtypeStruct((batch * seq, HG_WIDTH), BF16),
                   jax.ShapeDtypeStruct((batch, HG_HEADS, HG_DIM, HG_DIM), F32)],
        scratch_shapes=[pltpu.VMEM((HG_HEADS, HG_DIM, HG_DIM), F32), scr(), scr(), scr()],
        compiler_params=_params("parallel", "arbitrary"),
        name="hgrn",
    )(zhg, s0, lb, gn, tri)


def _flash_update(s, vblk, m, l, acc):
    m_new = jnp.maximum(m, jnp.max(s, axis=-1, keepdims=True))
    alpha = jnp.exp(m - m_new)
    p = jnp.exp(s - m_new)
    l = alpha * l + jnp.sum(p, axis=-1, keepdims=True)
    acc = alpha * acc + _dot(p.astype(BF16), vblk)
    return m_new, l, acc


def _attn_prompt_kernel(jstart_ref, q_ref, k_ref, v_ref, c_ref, o_ref, *, blk, nq):
    b, p, i = pl.program_id(0), pl.program_id(1), pl.program_id(2)
    j0 = jstart_ref[(b * FOX_PAIRS + p) * nq + i]
    q2 = q_ref[...]
    lane = lax.broadcasted_iota(jnp.int32, (blk, 2 * FOX_HD), 1)
    row_id = lax.broadcasted_iota(jnp.int32, (blk, blk), 0)
    col_id = lax.broadcasted_iota(jnp.int32, (blk, blk), 1)
    outs = []
    for hh in range(2):
        in_head = (lane >= hh * FOX_HD) & (lane < (hh + 1) * FOX_HD)
        qh = jnp.where(in_head, q2, jnp.zeros_like(q2))

        def scores(j):
            ks = pl.ds(pl.multiple_of(j * blk, blk), blk)
            s = _dot_nt(qh, k_ref[ks, :]) - c_ref[0, 0, hh:hh + 1, ks]
            return s, v_ref[ks, :]

        def body(j, carry):
            s, vblk = scores(j)
            return _flash_update(s, vblk, *carry)

        init = (jnp.full((blk, 1), NEG_INF, F32), jnp.zeros((blk, 1), F32),
                jnp.zeros((blk, 2 * FOX_HD), F32))
        carry = lax.fori_loop(j0, i, body, init)
        s, vblk = scores(i)
        s = jnp.where(col_id <= row_id, s, NEG_INF)
        m, l, acc = _flash_update(s, vblk, *carry)
        outs.append(acc / l)
    o_ref[...] = jnp.where(lane < FOX_HD, outs[0], outs[1]).astype(BF16)


def _attn_prompt_call(jstart, qb, kb, vb, c4, *, batch, seq):
    blk = min(ATTN_BLOCK, seq)
    assert seq % blk == 0
    nq = seq // blk
    kv_spec = pl.BlockSpec((seq, 2 * FOX_HD), lambda b, p, i, js: (b, p))
    return pl.pallas_call(
        functools.partial(_attn_prompt_kernel, blk=blk, nq=nq),
        grid_spec=pltpu.PrefetchScalarGridSpec(
            num_scalar_prefetch=1,
            grid=(batch, FOX_PAIRS, nq),
            in_specs=[
                pl.BlockSpec((blk, 2 * FOX_HD), lambda b, p, i, js: (b * nq + i, p)),
                kv_spec, kv_spec,
                pl.BlockSpec((1, 1, 2, seq), lambda b, p, i, js: (b, p, 0, 0)),
            ],
            out_specs=pl.BlockSpec((blk, 2 * FOX_HD), lambda b, p, i, js: (b * nq + i, p)),
        ),
        out_shape=jax.ShapeDtypeStruct((batch * seq, FOX_WIDTH), BF16),
        compiler_params=_params("parallel", "parallel", "arbitrary"),
        name="attn_prompt",
    )(jstart, qb, kb, vb, c4)


def _attn_decode_kernel(q_ref, kn_ref, vn_ref, kc_ref, vc_ref, c_ref, o_ref, *, past, tt):
    q2 = q_ref[...]
    kn = kn_ref[...]
    vn = vn_ref[...]
    lane = lax.broadcasted_iota(jnp.int32, (tt, 2 * FOX_HD), 1)
    row_id = lax.broadcasted_iota(jnp.int32, (tt, tt), 0)
    col_id = lax.broadcasted_iota(jnp.int32, (tt, tt), 1)
    outs = []
    for hh in range(2):
        in_head = (lane >= hh * FOX_HD) & (lane < (hh + 1) * FOX_HD)
        qh = jnp.where(in_head, q2, jnp.zeros_like(q2))
        qc = q2[:, hh * FOX_HD:(hh + 1) * FOX_HD]
        c = c_ref[0, 0, hh:hh + 1, :]
        s_c = _dot_nt(qc, kc_ref[0, hh].astype(BF16)) - c[:, :past]
        s_n = _dot_nt(qh, kn) - c[:, past:]
        s_n = jnp.where(col_id <= row_id, s_n, NEG_INF)
        m = jnp.maximum(jnp.max(s_c, axis=-1, keepdims=True), jnp.max(s_n, axis=-1, keepdims=True))
        p_c = jnp.exp(s_c - m)
        p_n = jnp.exp(s_n - m)
        l = jnp.sum(p_c, axis=-1, keepdims=True) + jnp.sum(p_n, axis=-1, keepdims=True)
        o_c = _dot(p_c.astype(BF16), vc_ref[0, hh].astype(BF16))
        o_n = _dot(p_n.astype(BF16), vn)
        o_n = o_n[:, hh * FOX_HD:(hh + 1) * FOX_HD]
        outs.append((o_c + o_n) / l)
    o_ref[...] = jnp.concatenate(outs, axis=-1).astype(BF16)


def _attn_decode_call(qb, kb, vb, kc, vc, c4, *, batch, tt):
    past = kc.shape[2]
    tok_spec = pl.BlockSpec((tt, 2 * FOX_HD), lambda b, p: (b, p))
    cache_spec = pl.BlockSpec((1, 2, past, FOX_HD), lambda b, p: (b, p, 0, 0))
    return pl.pallas_call(
        functools.partial(_attn_decode_kernel, past=past, tt=tt),
        grid=(batch, FOX_PAIRS),
        in_specs=[tok_spec, tok_spec, tok_spec, cache_spec, cache_spec,
                  pl.BlockSpec((1, 1, 2, past + tt), lambda b, p: (b, p, 0, 0))],
        out_specs=tok_spec,
        out_shape=jax.ShapeDtypeStruct((batch * tt, FOX_WIDTH), BF16),
        compiler_params=_params("parallel", "parallel"),
        name="attn_decode",
    )(qb, kb, vb, kc, vc, c4)


def _post_kernel(x_ref, ohg_ref, ofx_ref, mod_ref, nw_ref, w1_ref, w2_ref,
                 wr_hi_ref, wr_lo_ref, br_ref, sut_ref,
                 x1_ref, h2_ref, idx_ref, gate_ref, cnt_ref, cnt_s):
    i = pl.program_id(0)

    @pl.when(i == 0)
    def _():
        cnt_s[...] = jnp.zeros_like(cnt_s)

    y = _dot(ohg_ref[...], w1_ref[...]) + _dot(ofx_ref[...], w2_ref[...])
    rows, d = y.shape
    g = rows // MOD_GROUP
    gate1 = mod_ref[0]
    x1 = x_ref[...] + (y.reshape(g, MOD_GROUP, d) * gate1[:, None, :]).reshape(rows, d)
    x1_ref[...] = x1
    h2 = _modulate(_rms(x1) * nw_ref[...], mod_ref[2], mod_ref[1])
    h_hi, h_lo = _split_bf16(h2)
    h2_ref[...] = h_hi
    wr_hi = wr_hi_ref[...]
    logits = (_dot_nt(wr_hi, h_hi) + _dot_nt(wr_lo_ref[...], h_hi) + _dot_nt(wr_hi, h_lo)
              + br_ref[...])
    e_f = lax.broadcasted_iota(jnp.int32, logits.shape, 0).astype(F32)
    work = logits
    vals, hots = [], []
    for _ in range(TOP_K):
        mx = jnp.max(work, axis=0, keepdims=True)
        ix = jnp.min(jnp.where(work == mx, e_f, float(N_EXPERTS)), axis=0, keepdims=True)
        hot = e_f == ix
        vals.append(mx)
        hots.append(hot)
        work = jnp.where(hot, -jnp.inf, work)
    exps = [jnp.exp(v - vals[0]) for v in vals]
    den = exps[0] + exps[1] + exps[2] + exps[3]
    hot_f = [h.astype(F32) for h in hots]
    hot_all = hot_f[0] + hot_f[1] + hot_f[2] + hot_f[3]
    before = cnt_s[:, 0:1] + _dot(hot_all.astype(BF16), sut_ref[...])
    for k in range(TOP_K):
        idx_ref[k:k + 1, :] = jnp.sum(hot_f[k] * e_f, axis=0, keepdims=True).astype(jnp.int32)
        idx_ref[TOP_K + k:TOP_K + k + 1, :] = jnp.sum(
            hot_f[k] * before, axis=0, keepdims=True).astype(jnp.int32)
        gate_ref[k:k + 1, :] = exps[k] / den
        gate_ref[TOP_K + k:TOP_K + k + 1, :] = jnp.zeros_like(den)
    total = cnt_s[...] + jnp.sum(hot_all, axis=1, keepdims=True)
    cnt_s[...] = total
    cnt_ref[...] = total


def _post_call(x, ohg, ofx, modg, nw, w1, w2, wr_hi, wr_lo, br, sut):
    n, d = x.shape
    tm = ROW_TILE
    g = tm // MOD_GROUP
    const = lambda shape: pl.BlockSpec(shape, lambda i: (0,) * len(shape))
    row = lambda w: pl.BlockSpec((tm, w), lambda i: (i, 0))
    return pl.pallas_call(
        _post_kernel,
        grid=(n // tm,),
        in_specs=[
            row(d), row(HG_WIDTH), row(FOX_WIDTH),
            pl.BlockSpec((3, g, d), lambda i: (0, i, 0)),
            const((1, d)), const(w1.shape), const(w2.shape),
            const(wr_hi.shape), const(wr_lo.shape), const(br.shape), const(sut.shape),
        ],
        out_specs=[
            row(d), row(d),
            pl.BlockSpec((2 * TOP_K, tm), lambda i: (0, i)),
            pl.BlockSpec((2 * TOP_K, tm), lambda i: (0, i)),
            const((N_EXPERTS, 128)),
        ],
        out_shape=[
            jax.ShapeDtypeStruct((n, d), F32), jax.ShapeDtypeStruct((n, d), BF16),
            jax.ShapeDtypeStruct((2 * TOP_K, n), jnp.int32),
            jax.ShapeDtypeStruct((2 * TOP_K, n), F32),
            jax.ShapeDtypeStruct((N_EXPERTS, 128), F32),
        ],
        scratch_shapes=[pltpu.VMEM((N_EXPERTS, 128), F32)],
        compiler_params=_params("arbitrary"),
        name="post_mixer",
    )(x, ohg, ofx, modg, nw, w1, w2, wr_hi, wr_lo, br, sut)


def _expert_kernel(blk_e_ref, n_used_ref, x_ref, wu_ref, bu_ref, wd_ref, bd_ref, o_ref):
    i = pl.program_id(0)
    used = i < n_used_ref[0]

    @pl.when(used)
    def _():
        u = _dot(x_ref[...], wu_ref[0]) + bu_ref[0]
        f = u.shape[1] // 2
        glu = jnp.minimum(u[:, :f], SWIGLU_LIMIT)
        lin = jnp.clip(u[:, f:], -SWIGLU_LIMIT, SWIGLU_LIMIT)
        a = glu * _sigmoid_pair(SWIGLU_ALPHA * glu)[0] * (lin + 1.0)
        o_ref[...] = _dot(a.astype(BF16), wd_ref[0]) + bd_ref[0]

    @pl.when(jnp.logical_not(used))
    def _():
        o_ref[...] = jnp.zeros_like(o_ref)


def _expert_call(blk_e, n_used, xs, wu, bu, wd, bd):
    p, d = xs.shape
    f2 = wu.shape[2]
    tm = MOE_TILE
    return pl.pallas_call(
        _expert_kernel,
        grid_spec=pltpu.PrefetchScalarGridSpec(
            num_scalar_prefetch=2,
            grid=(p // tm,),
            in_specs=[
                pl.BlockSpec((tm, d), lambda i, be, nu: (jnp.minimum(i, nu[0] - 1), 0)),
                pl.BlockSpec((1, d, f2), lambda i, be, nu: (be[i], 0, 0)),
                pl.BlockSpec((1, 1, f2), lambda i, be, nu: (be[i], 0, 0)),
                pl.BlockSpec((1, f2 // 2, d), lambda i, be, nu: (be[i], 0, 0)),
                pl.BlockSpec((1, 1, d), lambda i, be, nu: (be[i], 0, 0)),
            ],
            out_specs=pl.BlockSpec((tm, d), lambda i, be, nu: (i, 0)),
        ),
        out_shape=jax.ShapeDtypeStruct((p, d), F32),
        compiler_params=_params("arbitrary"),
        name="experts",
    )(blk_e, n_used, xs, wu, bu, wd, bd)


def _combine_kernel(x_ref, y_ref, gate_ref, mod_ref, nw_ref, o_ref, *, final):
    gates = gate_ref[...]
    y = gates[:, 0:1] * y_ref[0]
    for k in range(1, TOP_K):
        y = y + gates[:, k:k + 1] * y_ref[k]
    rows, d = y.shape
    g = rows // MOD_GROUP
    gate2 = mod_ref[0]
    x2 = x_ref[...] + (y.reshape(g, MOD_GROUP, d) * gate2[:, None, :]).reshape(rows, d)
    if final:
        x2 = _rms(x2) * nw_ref[...]
    o_ref[...] = x2


def _combine_call(x1, yg, gates, modg, nw, *, final):
    n, d = x1.shape
    tm = ROW_TILE
    g = tm // MOD_GROUP
    return pl.pallas_call(
        functools.partial(_combine_kernel, final=final),
        grid=(n // tm,),
        in_specs=[
            pl.BlockSpec((tm, d), lambda i: (i, 0)),
            pl.BlockSpec((TOP_K, tm, d), lambda i: (0, i, 0)),
            pl.BlockSpec((tm, TOP_K), lambda i: (i, 0)),
            pl.BlockSpec((1, g, d), lambda i: (5, i, 0)),
            pl.BlockSpec((1, d), lambda i: (0, 0)),
        ],
        out_specs=pl.BlockSpec((tm, d), lambda i: (i, 0)),
        out_shape=jax.ShapeDtypeStruct((n, d), F32),
        compiler_params=_params("parallel"),
        name="combine",
    )(x1, yg, gates, modg, nw)


def _hgrn2_lower_bounds(lb_logits):
    p = jax.nn.softmax(lb_logits.astype(F32), axis=0)
    return jnp.cumsum(p, axis=0) - p[0:1]


def _route(idx8, cnt, n_tok):
    a = n_tok * TOP_K
    n_blocks = -(-a // MOE_TILE) + N_EXPERTS
    counts = cnt[:, 0].astype(jnp.int32)
    padded = (counts + MOE_TILE - 1) // MOE_TILE * MOE_TILE
    pends = jnp.cumsum(padded)
    pstarts = pends - padded
    dest = pstarts[idx8[:TOP_K]] + idx8[TOP_K:]
    token = jnp.broadcast_to(jnp.arange(n_tok, dtype=jnp.int32)[None, :], (TOP_K, n_tok))
    tok_buf = jnp.zeros((n_blocks * MOE_TILE,), jnp.int32).at[dest.reshape(-1)].set(
        token.reshape(-1), unique_indices=True)
    blk_start = jnp.arange(n_blocks, dtype=jnp.int32) * MOE_TILE
    blk_e = jnp.minimum(jnp.searchsorted(pends, blk_start, side="right"), N_EXPERTS - 1).astype(jnp.int32)
    n_used = (pends[-1:] // MOE_TILE).astype(jnp.int32)
    return dest, tok_buf, blk_e, n_used


def kernel(x_prompt, x_sample, c_prompt, c_sample, state_hgrn, cache_fox_k, cache_fox_v, cache_fox_lf,
           w_ada, b_ada, norm_mix, norm_ffn, w_in, lb_logits, hg_norm, fox_fb, w_out,
           w_router, b_router, w_up, b_up, w_down, b_down, norm_final):
    bp, tp, d = x_prompt.shape
    bs, ts, _ = x_sample.shape
    depth = w_in.shape[0]
    n_p, n_s = bp * tp, bs * ts
    n = n_p + n_s
    assert tp % MOD_GROUP == 0 and ts == MOD_GROUP
    assert n_p % ROW_TILE == 0 and n_s % ROW_TILE == 0

    x = jnp.concatenate([x_prompt.reshape(n_p, d), x_sample.reshape(n_s, d)], axis=0)
    c = jnp.concatenate([c_prompt, c_sample], axis=0)
    bc = c.shape[0]
    c = jnp.pad(c, ((0, (-bc) % 8), (0, 0)))
    mod = _ada_call(c, w_ada, b_ada)
    mod_rows = jnp.concatenate([jnp.repeat(mod[:, :bp], tp // MOD_GROUP, axis=1), mod[:, bp:bp + bs]], axis=1)
    modg = mod_rows.reshape(depth, n // MOD_GROUP, 6, d).transpose(0, 2, 1, 3)

    lbs = _hgrn2_lower_bounds(lb_logits)
    t = jnp.arange(ROW_TILE, dtype=jnp.int32)
    sut = (t[:, None] < t[None, :]).astype(BF16)
    s0_prompt = jnp.zeros((bp, HG_HEADS, HG_DIM, HG_DIM), F32)
    f2 = w_up.shape[-1]

    new_s_p, new_k_p, new_v_p, new_lf_p = [], [], [], []
    new_s_s, new_k_s, new_v_s, new_lf_s = [], [], [], []
    for l in range(depth):
        o4 = 4 * HG_WIDTH
        w_in_b = w_in[l].astype(BF16)
        whg = w_in_b[:, :o4]
        wfx = w_in_b[:, o4:o4 + 3 * FOX_WIDTH]
        wffT = w_in_b[:, o4 + 3 * FOX_WIDTH:].T
        fb = fox_fb[l].astype(F32).reshape(FOX_HEADS, 1)
        nw_mix = norm_mix[l].reshape(1, d)
        pre = functools.partial(_pre_call, x, modg[l], nw_mix, whg, wfx, wffT, fb)
        zhg_p, qb_p, kb_p, vb_p, k_p, v_p, lfT_p = pre(row0=0, batch=bp, seq=tp)
        zhg_s, qb_s, kb_s, vb_s, k_s, v_s, lfT_s = pre(row0=n_p, batch=bs, seq=ts)
        lf_p = lfT_p.reshape(FOX_HEADS, bp, tp).transpose(1, 0, 2)
        lf_s = lfT_s.reshape(FOX_HEADS, bs, ts).transpose(1, 0, 2)

        lb = lbs[l].reshape(1, HG_WIDTH)
        gn = hg_norm[l].astype(F32).reshape(1, HG_DIM)
        ohg_p, s_p = _hgrn_call(zhg_p, s0_prompt, lb, gn, batch=bp, seq=tp)
        ohg_s, s_s = _hgrn_call(zhg_s, state_hgrn[l].astype(F32), lb, gn, batch=bs, seq=ts)

        c_p = jnp.cumsum(lf_p, axis=-1).reshape(bp, FOX_PAIRS, 2, tp)
        nq = tp // min(ATTN_BLOCK, tp)
        jstart = jnp.zeros((bp * FOX_PAIRS * nq,), jnp.int32)
        ofx_p = _attn_prompt_call(jstart, qb_p, kb_p, vb_p, c_p, batch=bp, seq=tp)
        past = cache_fox_lf.shape[-1]
        c_s = jnp.cumsum(jnp.concatenate([cache_fox_lf[l].astype(F32), lf_s], axis=-1), axis=-1)
        ofx_s = _attn_decode_call(qb_s, kb_s, vb_s, cache_fox_k[l], cache_fox_v[l],
                                  c_s.reshape(bs, FOX_PAIRS, 2, past + ts), batch=bs, tt=ts)

        ohg = jnp.concatenate([ohg_p, ohg_s], axis=0)
        ofx = jnp.concatenate([ofx_p, ofx_s], axis=0)
        w_out_b = w_out[l].astype(BF16)
        wr_hi, wr_lo = _split_bf16(w_router[l].astype(F32).T)
        x1, h2, idx8, gate8, cnt = _post_call(
            x, ohg, ofx, modg[l, 2:5], norm_ffn[l].reshape(1, d), w_out_b[:HG_WIDTH], w_out_b[HG_WIDTH:],
            wr_hi, wr_lo, b_router[l].astype(F32).reshape(N_EXPERTS, 1), sut)

        dest, tok_buf, blk_e, n_used = _route(idx8, cnt, n)
        xs = jnp.take(h2, tok_buf, axis=0)
        wu = jnp.concatenate([w_up[l][:, :, 0::2], w_up[l][:, :, 1::2]], axis=-1).astype(BF16)
        bu = jnp.concatenate([b_up[l][:, 0::2], b_up[l][:, 1::2]], axis=-1).reshape(N_EXPERTS, 1, f2)
        yb = _expert_call(blk_e, n_used, xs, wu, bu, w_down[l].astype(BF16),
                          b_down[l].reshape(N_EXPERTS, 1, d))
        yg = jnp.take(yb, dest, axis=0)
        x = _combine_call(x1, yg, gate8[:TOP_K].T, modg[l], norm_final.reshape(1, d),
                          final=(l == depth - 1))

        new_s_p.append(s_p); new_k_p.append(k_p); new_v_p.append(v_p); new_lf_p.append(lf_p)
        new_s_s.append(s_s); new_k_s.append(k_s); new_v_s.append(v_s); new_lf_s.append(lf_s)

    y_prompt = x[:n_p].reshape(bp, tp, d)
    y_sample = x[n_p:].reshape(bs, ts, d)
    st = jnp.stack
    return (y_prompt, y_sample, st(new_s_p), st(new_k_p), st(new_v_p), st(new_lf_p),
            st(new_s_s), st(new_k_s), st(new_v_s), st(new_lf_s))
```

```python
import functools

import jax
import jax.numpy as jnp
from jax import lax
from jax.experimental import pallas as pl
from jax.experimental.pallas import tpu as pltpu

F32 = jnp.float32
BF16 = jnp.bfloat16
U32 = jnp.uint32

HG_HEADS = 4
HG_DIM = 128
HG_WIDTH = HG_HEADS * HG_DIM
FOX_HEADS = 8
FOX_HD = 64
FOX_WIDTH = FOX_HEADS * FOX_HD
FOX_PAIRS = FOX_HEADS // 2
N_EXPERTS = 32
TOP_K = 4
GLA_BLOCK = 16
MOD_GROUP = 64
SWIGLU_ALPHA = 1.702
SWIGLU_LIMIT = 7.0
EPS = 1e-6
NEG_INF = -1e30
F_FLOOR = 1e-30
EXP_UNDERFLOW = 88.0
NORM_SLACK = 1.01

LANES = 128
MXU_DIM = 256
ROW_TILE = 512
HGRN_CHUNK = 256
ATTN_BLOCK = 256
MOE_TILE = 256
VMEM_LIMIT = 56 * 1024 * 1024


def _params(*sem):
    return pltpu.CompilerParams(dimension_semantics=sem, vmem_limit_bytes=VMEM_LIMIT)


def _split_bf16(a):
    hi = a.astype(BF16)
    lo = (a - hi.astype(F32)).astype(BF16)
    return hi, lo


def _dot(a, b):
    return jnp.dot(a, b, preferred_element_type=F32)


def _dot_nt(a, b):
    return lax.dot_general(a, b, (((1,), (1,)), ((), ())), preferred_element_type=F32)


def _dot_tn(a, b):
    return lax.dot_general(a, b, (((0,), (0,)), ((), ())), preferred_element_type=F32)


def _sigmoid_pair(z):
    e = jnp.exp(-jnp.abs(z))
    r = 1.0 / (1.0 + e)
    er = e * r
    pos = z >= 0.0
    return jnp.where(pos, r, er), jnp.where(pos, er, r)


def _silu(z):
    return z * _sigmoid_pair(z)[0]


def _log_sigmoid(z):
    return jnp.minimum(z, 0.0) - jnp.log(1.0 + jnp.exp(-jnp.abs(z)))


def _rms(x):
    return x * lax.rsqrt(jnp.mean(x * x, axis=-1, keepdims=True) + EPS)


def _modulate(y, scale, shift):
    rows, d = y.shape
    g = rows // MOD_GROUP
    y3 = y.reshape(g, MOD_GROUP, d)
    return (y3 * (1.0 + scale[:, None, :]) + shift[:, None, :]).reshape(rows, d)


def _gated_residual(x, y, gate):
    rows, d = y.shape
    g = rows // MOD_GROUP
    return x + (y.reshape(g, MOD_GROUP, d) * gate[:, None, :]).reshape(rows, d)


def _pack_halves(y):
    w = y.shape[1] // 2
    hi = lax.bitcast_convert_type(y[:, :w].astype(BF16).astype(F32), U32)
    lo = lax.bitcast_convert_type(y[:, w:].astype(BF16).astype(F32), U32)
    return hi | (lo >> 16)


def _unpack_halves(u):
    hi = lax.bitcast_convert_type(u & jnp.uint32(0xFFFF0000), F32)
    lo = lax.bitcast_convert_type(u << 16, F32)
    return jnp.concatenate([hi, lo], axis=-1)


def _ada_kernel(c_ref, w_ref, b_ref, o_ref):
    c = c_ref[...]
    s_hi, s_lo = _split_bf16(_silu(c))
    w_hi, w_lo = _split_bf16(w_ref[0])
    o_ref[0] = _dot(s_hi, w_hi) + _dot(s_hi, w_lo) + _dot(s_lo, w_hi) + b_ref[0]


def _ada_call(c, w_ada, b_ada):
    depth, d, d6 = w_ada.shape
    bc = c.shape[0]
    tn = d6 // 4
    return pl.pallas_call(
        _ada_kernel,
        grid=(depth, d6 // tn),
        in_specs=[
            pl.BlockSpec((bc, d), lambda l, j: (0, 0)),
            pl.BlockSpec((1, d, tn), lambda l, j: (l, 0, j)),
            pl.BlockSpec((1, 1, tn), lambda l, j: (l, 0, j)),
        ],
        out_specs=pl.BlockSpec((1, bc, tn), lambda l, j: (l, 0, j)),
        out_shape=jax.ShapeDtypeStruct((depth, bc, d6), F32),
        compiler_params=_params("parallel", "parallel"),
        name="ada_mod",
    )(c, w_ada, b_ada.reshape(depth, 1, d6))


def _pre_kernel(x_ref, mod_ref, nw_ref, whg_ref, wfx_ref, wff_ref, fb_ref, k_acc, v_acc,
                zhg_ref, q_ref, kb_ref, vb_ref, k_ref, v_ref, lf_ref, *, nb, tt):
    del k_acc, v_acc
    x = x_ref[...]
    h = _modulate(_rms(x) * nw_ref[...], mod_ref[1], mod_ref[0])
    hb = h.astype(BF16)
    zhg_ref[...] = _dot(hb, whg_ref[...])
    zfx = _dot(hb, wfx_ref[...])
    q = zfx[:, :FOX_WIDTH] * (FOX_HD ** -0.5)
    k = zfx[:, FOX_WIDTH:2 * FOX_WIDTH]
    v = zfx[:, 2 * FOX_WIDTH:]
    q_ref[...] = q.astype(BF16)
    kb_ref[...] = k.astype(BF16)
    vb_ref[...] = v.astype(BF16)
    for b in range(nb):
        for hd in range(FOX_HEADS):
            rows = slice(b * tt, (b + 1) * tt)
            cols = slice(hd * FOX_HD, (hd + 1) * FOX_HD)
            k_ref[0, b, hd] = k[rows, cols]
            v_ref[0, b, hd] = v[rows, cols]
    ff = _dot_nt(wff_ref[...], hb)
    lf_ref[...] = _log_sigmoid(ff + fb_ref[...])


def _pre_call(x, modg, nw, whg, wfx, wffT, fb, k_acc, v_acc, *, layer, row0, batch, seq):
    d = x.shape[1]
    n = batch * seq
    tm = ROW_TILE
    tt = min(seq, tm)
    nb = tm // tt
    assert n % tm == 0 and row0 % tm == 0 and tm % tt == 0 and seq % tt == 0
    i0 = row0 // tm
    g = tm // MOD_GROUP
    tpb = seq // tt

    def head_major_idx(i):
        return (layer, i // tpb, 0, i % tpb, 0) if nb == 1 else (layer, i, 0, 0, 0)

    tok = lambda w, dt: jax.ShapeDtypeStruct((n, w), dt)
    hm = jax.ShapeDtypeStruct(k_acc.shape, F32)
    const = lambda shape: pl.BlockSpec(shape, lambda i: (0,) * len(shape))
    any_spec = pl.BlockSpec(memory_space=pl.ANY)
    return pl.pallas_call(
        functools.partial(_pre_kernel, nb=nb, tt=tt),
        grid=(n // tm,),
        in_specs=[
            pl.BlockSpec((tm, d), lambda i: (i + i0, 0)),
            pl.BlockSpec((2, g, d), lambda i: (0, i + i0, 0)),
            const((1, d)),
            const(whg.shape), const(wfx.shape), const(wffT.shape), const(fb.shape),
            any_spec, any_spec,
        ],
        out_specs=[
            pl.BlockSpec((tm, 4 * HG_WIDTH), lambda i: (i, 0)),
            pl.BlockSpec((tm, FOX_WIDTH), lambda i: (i, 0)),
            pl.BlockSpec((tm, FOX_WIDTH), lambda i: (i, 0)),
            pl.BlockSpec((tm, FOX_WIDTH), lambda i: (i, 0)),
            pl.BlockSpec((1, nb, FOX_HEADS, tt, FOX_HD), head_major_idx),
            pl.BlockSpec((1, nb, FOX_HEADS, tt, FOX_HD), head_major_idx),
            pl.BlockSpec((FOX_HEADS, tm), lambda i: (0, i)),
        ],
        out_shape=[tok(4 * HG_WIDTH, F32), tok(FOX_WIDTH, BF16), tok(FOX_WIDTH, BF16),
                   tok(FOX_WIDTH, BF16), hm, hm, jax.ShapeDtypeStruct((FOX_HEADS, n), F32)],
        input_output_aliases={7: 4, 8: 5},
        compiler_params=_params("parallel"),
        name="pre_mixer",
    )(x, modg, nw, whg, wfx, wffT, fb, k_acc, v_acc)


def _hgrn_kernel(z_ref, s0_ref, lb_ref, gn_ref, tri_ref, o_ref, s_ref,
                 st_ref, q_s, k_s, b_s, *, chunk):
    t = pl.program_id(1)
    L = GLA_BLOCK
    W = HG_WIDTH

    @pl.when(t == 0)
    def _():
        for h in range(HG_HEADS):
            st_ref[h] = s0_ref[0, h].T

    lb = lb_ref[...]
    zf = z_ref[:, W:2 * W]
    sig, sig_neg = _sigmoid_pair(zf)
    f = lb + (1.0 - lb) * sig
    g = jnp.log(jnp.maximum(f, F_FLOOR))
    k_s[...] = (1.0 - lb) * sig_neg
    q_s[...] = _silu(z_ref[:, :W]) * (HG_DIM ** -0.5)
    g0 = g.astype(BF16)
    r1 = g - g0.astype(F32)
    g1 = r1.astype(BF16)
    g2 = (r1 - g1.astype(F32)).astype(BF16)
    tri = tri_ref[...]
    cs = tri.shape[0]
    for r in range(chunk // cs):
        rows = slice(r * cs, (r + 1) * cs)
        b_s[rows, :] = _dot(tri, g0[rows]) + _dot(tri, g1[rows]) + _dot(tri, g2[rows])

    row_id = lax.broadcasted_iota(jnp.int32, (L, L), 0)
    col_id = lax.broadcasted_iota(jnp.int32, (L, L), 1)
    gn = gn_ref[...]

    def block(i, carry):
        rows = pl.ds(pl.multiple_of(i * L, L), L)
        for h in range(HG_HEADS):
            cols = slice(h * HG_DIM, (h + 1) * HG_DIM)
            q = q_s[rows, cols]
            kk = k_s[rows, cols]
            b = b_s[rows, cols]
            vv = z_ref[rows, 2 * W + h * HG_DIM:2 * W + (h + 1) * HG_DIM]
            hg = z_ref[rows, 3 * W + h * HG_DIM:3 * W + (h + 1) * HG_DIM]
            st = st_ref[h]
            o = _dot_nt((q * jnp.exp(b)).astype(BF16), st.astype(BF16))
            att = jnp.zeros((L, L), F32)
            for s in range(L):
                dec = jnp.exp(jnp.minimum(b - b[s:s + 1, :], 0.0))
                col = jnp.sum(q * kk[s:s + 1, :] * dec, axis=-1, keepdims=True)
                att = jnp.where(col_id == s, col, att)
            att = jnp.where(row_id >= col_id, att, 0.0)
            vb = vv.astype(BF16)
            o = o + _dot(att.astype(BF16), vb)
            b_end = b[L - 1:L, :]
            kd = kk * jnp.exp(b_end - b)
            st_ref[h] = st * jnp.exp(b_end) + _dot_tn(vb, kd.astype(BF16))
            o_ref[rows, cols] = (_rms(o) * gn * _silu(hg)).astype(BF16)
        return carry

    lax.fori_loop(0, chunk // L, block, 0)

    @pl.when(t == pl.num_programs(1) - 1)
    def _():
        for h in range(HG_HEADS):
            s_ref[0, h] = st_ref[h].T


def _hgrn_call(zhg, s0, lb, gn, *, batch, seq):
    chunk = min(HGRN_CHUNK, seq)
    cs = min(LANES, chunk)
    assert seq % chunk == 0 and chunk % cs == 0 and cs % GLA_BLOCK == 0
    r = jnp.arange(cs, dtype=jnp.int32)
    tri = ((r[:, None] // GLA_BLOCK == r[None, :] // GLA_BLOCK) & (r[None, :] <= r[:, None])).astype(BF16)
    nt = seq // chunk
    scr = lambda: pltpu.VMEM((chunk, HG_WIDTH), F32)
    return pl.pallas_call(
        functools.partial(_hgrn_kernel, chunk=chunk),
        grid=(batch, nt),
        in_specs=[
            pl.BlockSpec((chunk, 4 * HG_WIDTH), lambda b, t: (b * nt + t, 0)),
            pl.BlockSpec((1, HG_HEADS, HG_DIM, HG_DIM), lambda b, t: (b, 0, 0, 0)),
            pl.BlockSpec((1, HG_WIDTH), lambda b, t: (0, 0)),
            pl.BlockSpec((1, HG_DIM), lambda b, t: (0, 0)),
            pl.BlockSpec((cs, cs), lambda b, t: (0, 0)),
        ],
        out_specs=[
            pl.BlockSpec((chunk, HG_WIDTH), lambda b, t: (b * nt + t, 0)),
            pl.BlockSpec((1, HG_HEADS, HG_DIM, HG_DIM), lambda b, t: (b, 0, 0, 0)),
        ],
        out_shape=[jax.ShapeDtypeStruct((batch * seq, HG_WIDTH), BF16),
                   jax.ShapeDtypeStruct((batch, HG_HEADS, HG_DIM, HG_DIM), F32)],
        scratch_shapes=[pltpu.VMEM((HG_HEADS, HG_DIM, HG_DIM), F32), scr(), scr(), scr()],
        compiler_params=_params("parallel", "arbitrary"),
        name="hgrn",
    )(zhg, s0, lb, gn, tri)


def _flash_update(s, vblk, m, l, acc):
    m_new = jnp.maximum(m, jnp.max(s, axis=-1, keepdims=True))
    alpha = jnp.exp(m - m_new)
    p = jnp.exp(s - m_new)
    l = alpha * l + jnp.sum(p, axis=-1, keepdims=True)
    acc = alpha * acc + _dot(p.astype(BF16), vblk)
    return m_new, l, acc


def _attn_prompt_kernel(jstart_ref, q_ref, k_ref, v_ref, c_ref, o_ref, *, blk, nq):
    b, p, i = pl.program_id(0), pl.program_id(1), pl.program_id(2)
    q2 = q_ref[...]
    lane = lax.broadcasted_iota(jnp.int32, (blk, 2 * FOX_HD), 1)
    row_id = lax.broadcasted_iota(jnp.int32, (blk, blk), 0)
    col_id = lax.broadcasted_iota(jnp.int32, (blk, blk), 1)
    outs = []
    for hh in range(2):
        j0 = jstart_ref[((b * FOX_PAIRS + p) * 2 + hh) * nq + i]
        in_head = (lane >= hh * FOX_HD) & (lane < (hh + 1) * FOX_HD)
        qh = jnp.where(in_head, q2, jnp.zeros_like(q2))

        def scores(j):
            ks = pl.ds(pl.multiple_of(j * blk, blk), blk)
            s = _dot_nt(qh, k_ref[ks, :]) - c_ref[0, 0, hh:hh + 1, ks]
            return s, v_ref[ks, :]

        def body(j, carry):
            s, vblk = scores(j)
            return _flash_update(s, vblk, *carry)

        init = (jnp.full((blk, 1), NEG_INF, F32), jnp.zeros((blk, 1), F32),
                jnp.zeros((blk, 2 * FOX_HD), F32))
        carry = lax.fori_loop(j0, i, body, init)
        s, vblk = scores(i)
        s = jnp.where(col_id <= row_id, s, NEG_INF)
        m, l, acc = _flash_update(s, vblk, *carry)
        outs.append(acc / l)
    o_ref[...] = jnp.where(lane < FOX_HD, outs[0], outs[1]).astype(BF16)


def _attn_prompt_call(jstart, qb, kb, vb, c4, *, batch, seq, blk):
    nq = seq // blk
    kv_spec = pl.BlockSpec((seq, 2 * FOX_HD), lambda b, p, i, js: (b, p))
    return pl.pallas_call(
        functools.partial(_attn_prompt_kernel, blk=blk, nq=nq),
        grid_spec=pltpu.PrefetchScalarGridSpec(
            num_scalar_prefetch=1,
            grid=(batch, FOX_PAIRS, nq),
            in_specs=[
                pl.BlockSpec((blk, 2 * FOX_HD), lambda b, p, i, js: (b * nq + i, p)),
                kv_spec, kv_spec,
                pl.BlockSpec((1, 1, 2, seq), lambda b, p, i, js: (b, p, 0, 0)),
            ],
            out_specs=pl.BlockSpec((blk, 2 * FOX_HD), lambda b, p, i, js: (b * nq + i, p)),
        ),
        out_shape=jax.ShapeDtypeStruct((batch * seq, FOX_WIDTH), BF16),
        compiler_params=_params("parallel", "parallel", "arbitrary"),
        name="attn_prompt",
    )(jstart, qb, kb, vb, c4)


def _attn_block_starts(qb, kb, c, *, batch, seq, blk):
    def norms(a):
        a = a.astype(F32).reshape(batch, seq, FOX_HEADS, FOX_HD)
        return jnp.sqrt(jnp.sum(a * a, axis=-1)).transpose(0, 2, 1) * NORM_SLACK

    qn, kn = norms(qb), norms(kb)
    nq = seq // blk
    blocks = lambda a: a.reshape(batch, FOX_HEADS, nq, blk)
    q_max = jnp.max(blocks(qn), axis=-1)
    k_max = jnp.max(kn, axis=-1)[..., None, None]
    row_floor = jnp.min(blocks(-(qn * kn) - c), axis=-1)
    c_min = jnp.min(blocks(c), axis=-1)
    upper = q_max[..., :, None] * k_max - c_min[..., None, :]
    skip = upper < row_floor[..., :, None] - EXP_UNDERFLOW
    lead = jnp.sum(jnp.cumprod(skip.astype(jnp.int32), axis=-1), axis=-1)
    start = jnp.minimum(lead, jnp.arange(nq, dtype=jnp.int32))
    return start.reshape(-1).astype(jnp.int32)


def _attn_decode_kernel(q_ref, kn_ref, vn_ref, kc_ref, vc_ref, c_ref, o_ref, *, past, tt):
    q2 = q_ref[...]
    kn = kn_ref[...]
    vn = vn_ref[...]
    lane = lax.broadcasted_iota(jnp.int32, (tt, 2 * FOX_HD), 1)
    row_id = lax.broadcasted_iota(jnp.int32, (tt, tt), 0)
    col_id = lax.broadcasted_iota(jnp.int32, (tt, tt), 1)
    outs = []
    for hh in range(2):
        in_head = (lane >= hh * FOX_HD) & (lane < (hh + 1) * FOX_HD)
        qh = jnp.where(in_head, q2, jnp.zeros_like(q2))
        qc = q2[:, hh * FOX_HD:(hh + 1) * FOX_HD]
        c = c_ref[0, 0, hh:hh + 1, :]
        s_c = _dot_nt(qc, kc_ref[0, 0, hh].astype(BF16)) - c[:, :past]
        s_n = _dot_nt(qh, kn) - c[:, past:]
        s_n = jnp.where(col_id <= row_id, s_n, NEG_INF)
        m = jnp.maximum(jnp.max(s_c, axis=-1, keepdims=True), jnp.max(s_n, axis=-1, keepdims=True))
        p_c = jnp.exp(s_c - m)
        p_n = jnp.exp(s_n - m)
        l = jnp.sum(p_c, axis=-1, keepdims=True) + jnp.sum(p_n, axis=-1, keepdims=True)
        o_c = _dot(p_c.astype(BF16), vc_ref[0, 0, hh].astype(BF16))
        o_n = _dot(p_n.astype(BF16), vn)
        o_n = o_n[:, hh * FOX_HD:(hh + 1) * FOX_HD]
        outs.append((o_c + o_n) / l)
    o_ref[...] = jnp.concatenate(outs, axis=-1).astype(BF16)


def _attn_decode_call(qb, kb, vb, kc, vc, c4, *, layer, batch, tt):
    past = kc.shape[3]
    tok_spec = pl.BlockSpec((tt, 2 * FOX_HD), lambda b, p: (b, p))
    cache_spec = pl.BlockSpec((1, 1, 2, past, FOX_HD), lambda b, p: (layer, b, p, 0, 0))
    return pl.pallas_call(
        functools.partial(_attn_decode_kernel, past=past, tt=tt),
        grid=(batch, FOX_PAIRS),
        in_specs=[tok_spec, tok_spec, tok_spec, cache_spec, cache_spec,
                  pl.BlockSpec((1, 1, 2, past + tt), lambda b, p: (b, p, 0, 0))],
        out_specs=tok_spec,
        out_shape=jax.ShapeDtypeStruct((batch * tt, FOX_WIDTH), BF16),
        compiler_params=_params("parallel", "parallel"),
        name="attn_decode",
    )(qb, kb, vb, kc, vc, c4)


def _post_kernel(x_ref, ohg_ref, ofx_ref, mod_ref, nw_ref, w1_ref, w2_ref,
                 wr_hi_ref, wr_lo_ref, br_ref, sut_ref,
                 x1_ref, h2_ref, idx_ref, gate_ref, cnt_ref, cnt_s):
    i = pl.program_id(0)

    @pl.when(i == 0)
    def _():
        cnt_s[...] = jnp.zeros_like(cnt_s)

    y = _dot(ohg_ref[...], w1_ref[...]) + _dot(ofx_ref[...], w2_ref[...])
    x1 = _gated_residual(x_ref[...], y, mod_ref[0])
    x1_ref[...] = x1
    h2 = _modulate(_rms(x1) * nw_ref[...], mod_ref[2], mod_ref[1])
    h_hi, h_lo = _split_bf16(h2)
    h2_ref[...] = _pack_halves(h2)
    wr_hi = wr_hi_ref[...]
    logits = (_dot_nt(wr_hi, h_hi) + _dot_nt(wr_lo_ref[...], h_hi) + _dot_nt(wr_hi, h_lo)
              + br_ref[...])
    e_f = lax.broadcasted_iota(jnp.int32, logits.shape, 0).astype(F32)
    work = logits
    vals, hots = [], []
    for _ in range(TOP_K):
        mx = jnp.max(work, axis=0, keepdims=True)
        ix = jnp.min(jnp.where(work == mx, e_f, float(N_EXPERTS)), axis=0, keepdims=True)
        hot = e_f == ix
        vals.append(mx)
        hots.append(hot)
        work = jnp.where(hot, -jnp.inf, work)
    exps = [jnp.exp(v - vals[0]) for v in vals]
    den = exps[0] + exps[1] + exps[2] + exps[3]
    hot_f = [h.astype(F32) for h in hots]
    hot_all = hot_f[0] + hot_f[1] + hot_f[2] + hot_f[3]
    before = cnt_s[:, 0:1] + _dot(hot_all.astype(BF16), sut_ref[...])
    for k in range(TOP_K):
        idx_ref[k:k + 1, :] = jnp.sum(hot_f[k] * e_f, axis=0, keepdims=True).astype(jnp.int32)
        idx_ref[TOP_K + k:TOP_K + k + 1, :] = jnp.sum(
            hot_f[k] * before, axis=0, keepdims=True).astype(jnp.int32)
        gate_ref[k:k + 1, :] = exps[k] / den
        gate_ref[TOP_K + k:TOP_K + k + 1, :] = jnp.zeros_like(den)
    total = cnt_s[...] + jnp.sum(hot_all, axis=1, keepdims=True)
    cnt_s[...] = total
    cnt_ref[...] = total


def _post_call(x, ohg, ofx, modg, nw, w1, w2, wr_hi, wr_lo, br, sut):
    n, d = x.shape
    tm = ROW_TILE
    g = tm // MOD_GROUP
    const = lambda shape: pl.BlockSpec(shape, lambda i: (0,) * len(shape))
    row = lambda w: pl.BlockSpec((tm, w), lambda i: (i, 0))
    return pl.pallas_call(
        _post_kernel,
        grid=(n // tm,),
        in_specs=[
            row(d), row(HG_WIDTH), row(FOX_WIDTH),
            pl.BlockSpec((3, g, d), lambda i: (0, i, 0)),
            const((1, d)), const(w1.shape), const(w2.shape),
            const(wr_hi.shape), const(wr_lo.shape), const(br.shape), const(sut.shape),
        ],
        out_specs=[
            row(d), row(d // 2),
            pl.BlockSpec((2 * TOP_K, tm), lambda i: (0, i)),
            pl.BlockSpec((2 * TOP_K, tm), lambda i: (0, i)),
            const((N_EXPERTS, LANES)),
        ],
        out_shape=[
            jax.ShapeDtypeStruct((n, d), F32), jax.ShapeDtypeStruct((n, d // 2), U32),
            jax.ShapeDtypeStruct((2 * TOP_K, n), jnp.int32),
            jax.ShapeDtypeStruct((2 * TOP_K, n), F32),
            jax.ShapeDtypeStruct((N_EXPERTS, LANES), F32),
        ],
        scratch_shapes=[pltpu.VMEM((N_EXPERTS, LANES), F32)],
        compiler_params=_params("arbitrary"),
        name="post_mixer",
    )(x, ohg, ofx, modg, nw, w1, w2, wr_hi, wr_lo, br, sut)


def _deint_kernel(w_ref, perm_ref, o_ref):
    w = w_ref[0, 0].astype(BF16)
    perm = perm_ref[...]
    f = w.shape[1] // 2
    half = MXU_DIM // 2
    for t in range(w.shape[1] // MXU_DIM):
        r = _dot(w[:, t * MXU_DIM:(t + 1) * MXU_DIM], perm).astype(BF16)
        o_ref[0, :, t * half:(t + 1) * half] = r[:, :half]
        o_ref[0, :, f + t * half:f + (t + 1) * half] = r[:, half:]


def _deint_call(w_up, *, layer):
    _, n_e, d, f2 = w_up.shape
    tr = ROW_TILE
    j = jnp.arange(MXU_DIM, dtype=jnp.int32)
    half = MXU_DIM // 2
    src = jnp.where(j < half, 2 * j, 2 * (j - half) + 1)
    perm = (j[:, None] == src[None, :]).astype(BF16)
    return pl.pallas_call(
        _deint_kernel,
        grid=(n_e, d // tr),
        in_specs=[
            pl.BlockSpec((1, 1, tr, f2), lambda e, r: (layer, e, r, 0)),
            pl.BlockSpec((MXU_DIM, MXU_DIM), lambda e, r: (0, 0)),
        ],
        out_specs=pl.BlockSpec((1, tr, f2), lambda e, r: (e, r, 0)),
        out_shape=jax.ShapeDtypeStruct((n_e, d, f2), BF16),
        compiler_params=_params("parallel", "parallel"),
        name="deint_w_up",
    )(w_up, perm)


def _expert_kernel(blk_e_ref, n_used_ref, x_ref, wu_ref, bu_ref, wd_ref, bd_ref, o_ref, wd_s):
    i = pl.program_id(0)
    used = i < n_used_ref[0]
    new_expert = jnp.logical_or(i == 0, blk_e_ref[i] != blk_e_ref[jnp.maximum(i - 1, 0)])

    @pl.when(jnp.logical_and(used, new_expert))
    def _():
        wd_s[...] = wd_ref[0, 0].astype(BF16)

    @pl.when(used)
    def _():
        x = _unpack_halves(x_ref[...]).astype(BF16)
        u = _dot(x, wu_ref[0]) + bu_ref[0]
        f = u.shape[1] // 2
        glu = jnp.minimum(u[:, :f], SWIGLU_LIMIT)
        lin = jnp.clip(u[:, f:], -SWIGLU_LIMIT, SWIGLU_LIMIT)
        a = glu * _sigmoid_pair(SWIGLU_ALPHA * glu)[0] * (lin + 1.0)
        o_ref[...] = _pack_halves(_dot(a.astype(BF16), wd_s[...]) + bd_ref[0, 0])

    @pl.when(jnp.logical_not(used))
    def _():
        o_ref[...] = jnp.zeros_like(o_ref)


def _expert_call(blk_e, n_used, xs, wu, bu, w_down, b_down, *, layer):
    p, dh = xs.shape
    d = 2 * dh
    f2 = wu.shape[2]
    tm = MOE_TILE
    return pl.pallas_call(
        _expert_kernel,
        grid_spec=pltpu.PrefetchScalarGridSpec(
            num_scalar_prefetch=2,
            grid=(p // tm,),
            in_specs=[
                pl.BlockSpec((tm, dh), lambda i, be, nu: (jnp.minimum(i, nu[0] - 1), 0)),
                pl.BlockSpec((1, d, f2), lambda i, be, nu: (be[i], 0, 0)),
                pl.BlockSpec((1, 1, f2), lambda i, be, nu: (be[i], 0, 0)),
                pl.BlockSpec((1, 1, f2 // 2, d), lambda i, be, nu: (layer, be[i], 0, 0)),
                pl.BlockSpec((1, 1, 1, d), lambda i, be, nu: (layer, be[i], 0, 0)),
            ],
            out_specs=pl.BlockSpec((tm, dh), lambda i, be, nu: (i, 0)),
            scratch_shapes=[pltpu.VMEM((f2 // 2, d), BF16)],
        ),
        out_shape=jax.ShapeDtypeStruct((p, dh), U32),
        compiler_params=_params("arbitrary"),
        name="experts",
    )(blk_e, n_used, xs, wu, bu, w_down, b_down.reshape(b_down.shape[0], N_EXPERTS, 1, d))


def _combine_kernel(x_ref, y_ref, gate_ref, mod_ref, nw_ref, o_ref, *, final):
    gates = gate_ref[...]
    y = gates[:, 0:1] * _unpack_halves(y_ref[0])
    for k in range(1, TOP_K):
        y = y + gates[:, k:k + 1] * _unpack_halves(y_ref[k])
    x2 = _gated_residual(x_ref[...], y, mod_ref[0])
    if final:
        x2 = _rms(x2) * nw_ref[...]
    o_ref[...] = x2


def _combine_call(x1, yg, gates, modg, nw, *, final):
    n, d = x1.shape
    tm = ROW_TILE
    g = tm // MOD_GROUP
    return pl.pallas_call(
        functools.partial(_combine_kernel, final=final),
        grid=(n // tm,),
        in_specs=[
            pl.BlockSpec((tm, d), lambda i: (i, 0)),
            pl.BlockSpec((TOP_K, tm, d // 2), lambda i: (0, i, 0)),
            pl.BlockSpec((tm, TOP_K), lambda i: (i, 0)),
            pl.BlockSpec((1, g, d), lambda i: (5, i, 0)),
            pl.BlockSpec((1, d), lambda i: (0, 0)),
        ],
        out_specs=pl.BlockSpec((tm, d), lambda i: (i, 0)),
        out_shape=jax.ShapeDtypeStruct((n, d), F32),
        compiler_params=_params("parallel"),
        name="combine",
    )(x1, yg, gates, modg, nw)


def _hgrn2_lower_bounds(lb_logits):
    p = jax.nn.softmax(lb_logits.astype(F32), axis=0)
    return jnp.cumsum(p, axis=0) - p[0:1]


def _cumsum_last(x):
    t = x.shape[-1]
    nb = t // LANES
    xb = x.reshape(-1, nb, LANES)
    i = jnp.arange(LANES)
    upper = (i[:, None] <= i[None, :]).astype(F32)
    within = jnp.einsum("rnk,kj->rnj", xb, upper, precision=lax.Precision.HIGHEST)
    j = jnp.arange(nb)
    strict = (j[:, None] < j[None, :]).astype(F32)
    offset = jnp.dot(within[..., -1], strict, precision=lax.Precision.HIGHEST)
    return (within + offset[..., None]).reshape(x.shape)


def _route(idx8, cnt, n_tok):
    a = n_tok * TOP_K
    n_blocks = -(-a // MOE_TILE) + N_EXPERTS
    counts = cnt[:, 0].astype(jnp.int32)
    padded = (counts + MOE_TILE - 1) // MOE_TILE * MOE_TILE
    pends = jnp.cumsum(padded)
    pstarts = pends - padded
    e_ids = jnp.arange(N_EXPERTS, dtype=jnp.int32)
    base = jnp.sum(jnp.where(idx8[:TOP_K, :, None] == e_ids, pstarts, 0), axis=-1)
    dest = base + idx8[TOP_K:]
    token = jnp.broadcast_to(jnp.arange(n_tok, dtype=jnp.int32)[None, :], (TOP_K, n_tok))
    tok_buf = jnp.zeros((n_blocks * MOE_TILE,), jnp.int32).at[dest.reshape(-1)].set(
        token.reshape(-1), unique_indices=True, mode="promise_in_bounds")
    blk_start = jnp.arange(n_blocks, dtype=jnp.int32) * MOE_TILE
    blk_e = jnp.minimum(jnp.sum(pends[None, :] <= blk_start[:, None], axis=1), N_EXPERTS - 1).astype(jnp.int32)
    n_used = (pends[-1:] // MOE_TILE).astype(jnp.int32)
    return dest, tok_buf, blk_e, n_used


def kernel(x_prompt, x_sample, c_prompt, c_sample, state_hgrn, cache_fox_k, cache_fox_v, cache_fox_lf,
           w_ada, b_ada, norm_mix, norm_ffn, w_in, lb_logits, hg_norm, fox_fb, w_out,
           w_router, b_router, w_up, b_up, w_down, b_down, norm_final):
    bp, tp, d = x_prompt.shape
    bs, ts, _ = x_sample.shape
    depth = w_in.shape[0]
    n_p, n_s = bp * tp, bs * ts
    n = n_p + n_s
    past = cache_fox_lf.shape[-1]
    assert tp % MOD_GROUP == 0 and ts == MOD_GROUP
    assert n_p % ROW_TILE == 0 and n_s % ROW_TILE == 0

    x = jnp.concatenate([x_prompt.reshape(n_p, d), x_sample.reshape(n_s, d)], axis=0)
    c = jnp.concatenate([c_prompt, c_sample], axis=0)
    bc = c.shape[0]
    c = jnp.pad(c, ((0, (-bc) % 8), (0, 0)))
    mod = _ada_call(c, w_ada, b_ada)
    mod_rows = jnp.concatenate([jnp.repeat(mod[:, :bp], tp // MOD_GROUP, axis=1), mod[:, bp:bp + bs]], axis=1)
    modg = mod_rows.reshape(depth, n // MOD_GROUP, 6, d).transpose(0, 2, 1, 3)

    lbs = _hgrn2_lower_bounds(lb_logits)
    t = jnp.arange(ROW_TILE, dtype=jnp.int32)
    sut = (t[:, None] < t[None, :]).astype(BF16)
    s0_prompt = jnp.zeros((bp, HG_HEADS, HG_DIM, HG_DIM), F32)
    f2 = w_up.shape[-1]
    blk = min(ATTN_BLOCK, tp)
    assert tp % blk == 0
    c_cache = _cumsum_last(cache_fox_lf.astype(F32))

    k_p = jnp.zeros((depth, bp, FOX_HEADS, tp, FOX_HD), F32)
    v_p = jnp.zeros_like(k_p)
    k_s = jnp.zeros((depth, bs, FOX_HEADS, ts, FOX_HD), F32)
    v_s = jnp.zeros_like(k_s)
    new_s_p, new_lf_p, new_s_s, new_lf_s = [], [], [], []
    for l in range(depth):
        o4 = 4 * HG_WIDTH
        w_in_b = w_in[l].astype(BF16)
        whg = w_in_b[:, :o4]
        wfx = w_in_b[:, o4:o4 + 3 * FOX_WIDTH]
        wffT = w_in_b[:, o4 + 3 * FOX_WIDTH:].T
        fb = fox_fb[l].astype(F32).reshape(FOX_HEADS, 1)
        nw_mix = norm_mix[l].reshape(1, d)
        pre = functools.partial(_pre_call, x, modg[l], nw_mix, whg, wfx, wffT, fb, layer=l)
        zhg_p, qb_p, kb_p, vb_p, k_p, v_p, lfT_p = pre(k_p, v_p, row0=0, batch=bp, seq=tp)
        zhg_s, qb_s, kb_s, vb_s, k_s, v_s, lfT_s = pre(k_s, v_s, row0=n_p, batch=bs, seq=ts)
        lf_p = lfT_p.reshape(FOX_HEADS, bp, tp).transpose(1, 0, 2)
        lf_s = lfT_s.reshape(FOX_HEADS, bs, ts).transpose(1, 0, 2)

        lb = lbs[l].reshape(1, HG_WIDTH)
        gn = hg_norm[l].astype(F32).reshape(1, HG_DIM)
        ohg_p, s_p = _hgrn_call(zhg_p, s0_prompt, lb, gn, batch=bp, seq=tp)
        ohg_s, s_s = _hgrn_call(zhg_s, state_hgrn[l].astype(F32), lb, gn, batch=bs, seq=ts)

        c_p = _cumsum_last(lf_p)
        jstart = _attn_block_starts(qb_p, kb_p, c_p, batch=bp, seq=tp, blk=blk)
        ofx_p = _attn_prompt_call(jstart, qb_p, kb_p, vb_p, c_p.reshape(bp, FOX_PAIRS, 2, tp),
                                  batch=bp, seq=tp, blk=blk)
        c_new = c_cache[l][..., -1:] + jnp.cumsum(lf_s, axis=-1)
        c_s = jnp.concatenate([c_cache[l], c_new], axis=-1)
        ofx_s = _attn_decode_call(qb_s, kb_s, vb_s, cache_fox_k, cache_fox_v,
                                  c_s.reshape(bs, FOX_PAIRS, 2, past + ts), layer=l, batch=bs, tt=ts)

        ohg = jnp.concatenate([ohg_p, ohg_s], axis=0)
        ofx = jnp.concatenate([ofx_p, ofx_s], axis=0)
        w_out_b = w_out[l].astype(BF16)
        wr_hi, wr_lo = _split_bf16(w_router[l].astype(F32).T)
        x1, h2, idx8, gate8, cnt = _post_call(
            x, ohg, ofx, modg[l, 2:5], norm_ffn[l].reshape(1, d), w_out_b[:HG_WIDTH], w_out_b[HG_WIDTH:],
            wr_hi, wr_lo, b_router[l].astype(F32).reshape(N_EXPERTS, 1), sut)

        dest, tok_buf, blk_e, n_used = _route(idx8, cnt, n)
        xs = h2.at[tok_buf].get(mode="promise_in_bounds")
        wu = _deint_call(w_up, layer=l)
        bu = jnp.concatenate([b_up[l][:, 0::2], b_up[l][:, 1::2]], axis=-1).reshape(N_EXPERTS, 1, f2)
        yb = _expert_call(blk_e, n_used, xs, wu, bu, w_down, b_down, layer=l)
        yg = yb.at[dest].get(mode="promise_in_bounds")
        x = _combine_call(x1, yg, gate8[:TOP_K].T, modg[l], norm_final.reshape(1, d),
                          final=(l == depth - 1))

        new_s_p.append(s_p); new_lf_p.append(lf_p)
        new_s_s.append(s_s); new_lf_s.append(lf_s)

    y_prompt = x[:n_p].reshape(bp, tp, d)
    y_sample = x[n_p:].reshape(bs, ts, d)
    st = jnp.stack
    return (y_prompt, y_sample, st(new_s_p), k_p, v_p, st(new_lf_p),
            st(new_s_s), k_s, v_s, st(new_lf_s))
```

```python
import functools

import jax
import jax.numpy as jnp
from jax import lax
from jax.experimental import pallas as pl
from jax.experimental.pallas import tpu as pltpu

F32 = jnp.float32
BF16 = jnp.bfloat16
U32 = jnp.uint32

HG_HEADS = 4
HG_DIM = 128
HG_WIDTH = HG_HEADS * HG_DIM
FOX_HEADS = 8
FOX_HD = 64
FOX_WIDTH = FOX_HEADS * FOX_HD
FOX_PAIRS = FOX_HEADS // 2
N_EXPERTS = 32
TOP_K = 4
GLA_BLOCK = 16
MOD_GROUP = 64
SWIGLU_ALPHA = 1.702
SWIGLU_LIMIT = 7.0
EPS = 1e-6
NEG_INF = -1e30
F_FLOOR = 1e-30
EXP_UNDERFLOW = 88.0
NORM_SLACK = 1.01

LANES = 128
MXU_DIM = 256
ROW_TILE = 512
HGRN_CHUNK = 256
ATTN_BLOCK = 256
MOE_TILE = 256
DECODE_HEADS = 4
VMEM_LIMIT = 56 * 1024 * 1024


def _params(*sem):
    return pltpu.CompilerParams(dimension_semantics=sem, vmem_limit_bytes=VMEM_LIMIT)


def _dot(a, b):
    return jnp.dot(a, b, preferred_element_type=F32)


def _dot_nt(a, b):
    return lax.dot_general(a, b, (((1,), (1,)), ((), ())), preferred_element_type=F32)


def _dot_tn(a, b):
    return lax.dot_general(a, b, (((0,), (0,)), ((), ())), preferred_element_type=F32)


def _sigmoid_pair(z):
    e = jnp.exp(-jnp.abs(z))
    r = 1.0 / (1.0 + e)
    er = e * r
    pos = z >= 0.0
    return jnp.where(pos, r, er), jnp.where(pos, er, r)


def _silu(z):
    return z * _sigmoid_pair(z)[0]


def _log_sigmoid(z):
    return jnp.minimum(z, 0.0) - jnp.log(1.0 + jnp.exp(-jnp.abs(z)))


def _rms(x):
    return x * lax.rsqrt(jnp.mean(x * x, axis=-1, keepdims=True) + EPS)


def _modulate(y, scale, shift):
    rows, d = y.shape
    g = rows // MOD_GROUP
    y3 = y.reshape(g, MOD_GROUP, d)
    return (y3 * (1.0 + scale[:, None, :]) + shift[:, None, :]).reshape(rows, d)


def _gated_residual(x, y, gate):
    rows, d = y.shape
    g = rows // MOD_GROUP
    return x + (y.reshape(g, MOD_GROUP, d) * gate[:, None, :]).reshape(rows, d)


def _pack_halves(y):
    w = y.shape[1] // 2
    hi = lax.bitcast_convert_type(y[:, :w].astype(BF16).astype(F32), U32)
    lo = lax.bitcast_convert_type(y[:, w:].astype(BF16).astype(F32), U32)
    return hi | (lo >> 16)


def _unpack_halves(u):
    hi = lax.bitcast_convert_type(u & jnp.uint32(0xFFFF0000), F32)
    lo = lax.bitcast_convert_type(u << 16, F32)
    return jnp.concatenate([hi, lo], axis=-1)


def _ada_kernel(c_ref, w_ref, b_ref, o_ref):
    s = _silu(c_ref[...])
    o_ref[0] = _dot(s.astype(BF16), w_ref[0].astype(BF16)) + b_ref[0]


def _ada_call(c, w_ada, b_ada):
    depth, d, d6 = w_ada.shape
    bc = c.shape[0]
    tn = d6 // 4
    return pl.pallas_call(
        _ada_kernel,
        grid=(depth, d6 // tn),
        in_specs=[
            pl.BlockSpec((bc, d), lambda l, j: (0, 0)),
            pl.BlockSpec((1, d, tn), lambda l, j: (l, 0, j)),
            pl.BlockSpec((1, 1, tn), lambda l, j: (l, 0, j)),
        ],
        out_specs=pl.BlockSpec((1, bc, tn), lambda l, j: (l, 0, j)),
        out_shape=jax.ShapeDtypeStruct((depth, bc, d6), F32),
        compiler_params=_params("parallel", "parallel"),
        name="ada_mod",
    )(c, w_ada, b_ada.reshape(depth, 1, d6))


def _pre_kernel(x_ref, mod_ref, nw_ref, whg_ref, wfx_ref, wff_ref, fb_ref, k_acc, v_acc,
                zhg_ref, q_ref, kb_ref, vb_ref, k_ref, v_ref, lf_ref, *, nb, tt, time_minor):
    del k_acc, v_acc
    x = x_ref[...]
    h = _modulate(_rms(x) * nw_ref[...], mod_ref[1], mod_ref[0])
    hb = h.astype(BF16)
    zhg_ref[...] = _dot(hb, whg_ref[...])
    zfx = _dot(hb, wfx_ref[...])
    q = zfx[:, :FOX_WIDTH] * (FOX_HD ** -0.5)
    k = zfx[:, FOX_WIDTH:2 * FOX_WIDTH]
    v = zfx[:, 2 * FOX_WIDTH:]
    q_ref[...] = q.astype(BF16)
    kb_ref[...] = k.astype(BF16)
    vb_ref[...] = v.astype(BF16)
    if time_minor:
        k, v = k.T, v.T
    for b in range(nb):
        for hd in range(FOX_HEADS):
            rows = slice(b * tt, (b + 1) * tt)
            cols = slice(hd * FOX_HD, (hd + 1) * FOX_HD)
            k_ref[0, b, hd] = k[cols, rows] if time_minor else k[rows, cols]
            v_ref[0, b, hd] = v[cols, rows] if time_minor else v[rows, cols]
    ff = _dot_nt(wff_ref[...], hb)
    lf_ref[...] = _log_sigmoid(ff + fb_ref[...])


def _pre_call(x, modg, nw, whg, wfx, wffT, fb, k_acc, v_acc, *, layer, row0, batch, seq):
    d = x.shape[1]
    time_minor = k_acc.shape[-2:] == (FOX_HD, seq) and seq != FOX_HD
    n = batch * seq
    tm = ROW_TILE
    tt = min(seq, tm)
    nb = tm // tt
    assert n % tm == 0 and row0 % tm == 0 and tm % tt == 0 and seq % tt == 0
    i0 = row0 // tm
    g = tm // MOD_GROUP
    tpb = seq // tt

    hm_block = (1, nb, FOX_HEADS, FOX_HD, tt) if time_minor else (1, nb, FOX_HEADS, tt, FOX_HD)

    def head_major_idx(i):
        if nb > 1:
            return (layer, i, 0, 0, 0)
        return (layer, i // tpb, 0, 0, i % tpb) if time_minor else (layer, i // tpb, 0, i % tpb, 0)

    tok = lambda w, dt: jax.ShapeDtypeStruct((n, w), dt)
    hm = jax.ShapeDtypeStruct(k_acc.shape, F32)
    const = lambda shape: pl.BlockSpec(shape, lambda i: (0,) * len(shape))
    any_spec = pl.BlockSpec(memory_space=pl.ANY)
    return pl.pallas_call(
        functools.partial(_pre_kernel, nb=nb, tt=tt, time_minor=time_minor),
        grid=(n // tm,),
        in_specs=[
            pl.BlockSpec((tm, d), lambda i: (i + i0, 0)),
            pl.BlockSpec((2, g, d), lambda i: (0, i + i0, 0)),
            const((1, d)),
            const(whg.shape), const(wfx.shape), const(wffT.shape), const(fb.shape),
            any_spec, any_spec,
        ],
        out_specs=[
            pl.BlockSpec((tm, 4 * HG_WIDTH), lambda i: (i, 0)),
            pl.BlockSpec((tm, FOX_WIDTH), lambda i: (i, 0)),
            pl.BlockSpec((tm, FOX_WIDTH), lambda i: (i, 0)),
            pl.BlockSpec((tm, FOX_WIDTH), lambda i: (i, 0)),
            pl.BlockSpec(hm_block, head_major_idx),
            pl.BlockSpec(hm_block, head_major_idx),
            pl.BlockSpec((FOX_HEADS, tm), lambda i: (0, i)),
        ],
        out_shape=[tok(4 * HG_WIDTH, F32), tok(FOX_WIDTH, BF16), tok(FOX_WIDTH, BF16),
                   tok(FOX_WIDTH, BF16), hm, hm, jax.ShapeDtypeStruct((FOX_HEADS, n), F32)],
        input_output_aliases={7: 4, 8: 5},
        compiler_params=_params("parallel"),
        name="pre_mixer",
    )(x, modg, nw, whg, wfx, wffT, fb, k_acc, v_acc)


def _hgrn_kernel(z_ref, s0_ref, lb_ref, gn_ref, tri_ref, o_ref, s_ref,
                 st_ref, q_s, k_s, b_s, oi_s, u_s, sb_s, *, chunk):
    t = pl.program_id(1)
    L = GLA_BLOCK
    W = HG_WIDTH

    @pl.when(t == 0)
    def _():
        for h in range(HG_HEADS):
            st_ref[h] = s0_ref[0, h].T

    lb = lb_ref[...]
    zf = z_ref[:, W:2 * W]
    sig, sig_neg = _sigmoid_pair(zf)
    f = lb + (1.0 - lb) * sig
    g = jnp.log(jnp.maximum(f, F_FLOOR))
    k_s[...] = (1.0 - lb) * sig_neg
    q_s[...] = _silu(z_ref[:, :W]) * (HG_DIM ** -0.5)
    g0 = g.astype(BF16)
    r1 = g - g0.astype(F32)
    g1 = r1.astype(BF16)
    g2 = (r1 - g1.astype(F32)).astype(BF16)
    tri = tri_ref[...]
    cs = tri.shape[0]
    for r in range(chunk // cs):
        rows = slice(r * cs, (r + 1) * cs)
        b_s[rows, :] = _dot(tri, g0[rows]) + _dot(tri, g1[rows]) + _dot(tri, g2[rows])

    H = L // 2
    row_id = lax.broadcasted_iota(jnp.int32, (L, L), 0)
    col_id = lax.broadcasted_iota(jnp.int32, (L, L), 1)
    col_hi = lax.broadcasted_iota(jnp.int32, (H, L), 1)
    gn = gn_ref[...]
    nblk = chunk // L
    block_rows = lambda i: pl.ds(pl.multiple_of(i * L, L), L)

    def intra(i, carry):
        rows = block_rows(i)
        for h in range(HG_HEADS):
            cols = slice(h * HG_DIM, (h + 1) * HG_DIM)
            q = q_s[rows, cols]
            kk = k_s[rows, cols]
            b = b_s[rows, cols]
            vb = z_ref[rows, 2 * W + h * HG_DIM:2 * W + (h + 1) * HG_DIM].astype(BF16)
            att = jnp.zeros((L, L), F32)
            for s in range(H):
                dec = jnp.exp(jnp.minimum(b - b[s:s + 1, :], 0.0))
                col = jnp.sum(q * kk[s:s + 1, :] * dec, axis=-1, keepdims=True)
                att = jnp.where(col_id == s, col, att)
            att_hi = att[H:]
            q_hi, b_hi = q[H:], b[H:]
            for s in range(H, L):
                dec = jnp.exp(jnp.minimum(b_hi - b[s:s + 1, :], 0.0))
                col = jnp.sum(q_hi * kk[s:s + 1, :] * dec, axis=-1, keepdims=True)
                att_hi = jnp.where(col_hi == s, col, att_hi)
            att = jnp.concatenate([att[:H], att_hi], axis=0)
            att = jnp.where(row_id >= col_id, att, 0.0)
            oi_s[rows, cols] = _dot(att.astype(BF16), vb)
            kd = kk * jnp.exp(b[L - 1:L, :] - b)
            u_s[i * HG_HEADS + h] = _dot_tn(vb, kd.astype(BF16))
        return carry

    def scan(i, carry):
        tail = pl.ds(pl.multiple_of(i * L + H, H), H)
        for h in range(HG_HEADS):
            cols = slice(h * HG_DIM, (h + 1) * HG_DIM)
            st = st_ref[h]
            sb_s[i * HG_HEADS + h] = st.astype(BF16)
            st_ref[h] = st * jnp.exp(b_s[tail, cols][H - 1:H, :]) + u_s[i * HG_HEADS + h]
        return carry

    def inter(i, carry):
        rows = block_rows(i)
        for h in range(HG_HEADS):
            cols = slice(h * HG_DIM, (h + 1) * HG_DIM)
            qe = (q_s[rows, cols] * jnp.exp(b_s[rows, cols])).astype(BF16)
            o = oi_s[rows, cols] + _dot_nt(qe, sb_s[i * HG_HEADS + h])
            hg = z_ref[rows, 3 * W + h * HG_DIM:3 * W + (h + 1) * HG_DIM]
            o_ref[rows, cols] = (_rms(o) * gn * _silu(hg)).astype(BF16)
        return carry

    unroll = 2 if nblk % 2 == 0 else 1
    lax.fori_loop(0, nblk, intra, 0, unroll=unroll)
    lax.fori_loop(0, nblk, scan, 0, unroll=unroll)
    lax.fori_loop(0, nblk, inter, 0, unroll=unroll)

    @pl.when(t == pl.num_programs(1) - 1)
    def _():
        for h in range(HG_HEADS):
            s_ref[0, h] = st_ref[h].T


def _hgrn_call(zhg, s0, lb, gn, *, batch, seq):
    chunk = min(HGRN_CHUNK, seq)
    cs = min(LANES, chunk)
    assert seq % chunk == 0 and chunk % cs == 0 and cs % GLA_BLOCK == 0
    r = jnp.arange(cs, dtype=jnp.int32)
    tri = ((r[:, None] // GLA_BLOCK == r[None, :] // GLA_BLOCK) & (r[None, :] <= r[:, None])).astype(BF16)
    nt = seq // chunk
    scr = lambda: pltpu.VMEM((chunk, HG_WIDTH), F32)
    per_chunk = chunk // GLA_BLOCK * HG_HEADS
    return pl.pallas_call(
        functools.partial(_hgrn_kernel, chunk=chunk),
        grid=(batch, nt),
        in_specs=[
            pl.BlockSpec((chunk, 4 * HG_WIDTH), lambda b, t: (b * nt + t, 0)),
            pl.BlockSpec((1, HG_HEADS, HG_DIM, HG_DIM), lambda b, t: (b, 0, 0, 0)),
            pl.BlockSpec((1, HG_WIDTH), lambda b, t: (0, 0)),
            pl.BlockSpec((1, HG_DIM), lambda b, t: (0, 0)),
            pl.BlockSpec((cs, cs), lambda b, t: (0, 0)),
        ],
        out_specs=[
            pl.BlockSpec((chunk, HG_WIDTH), lambda b, t: (b * nt + t, 0)),
            pl.BlockSpec((1, HG_HEADS, HG_DIM, HG_DIM), lambda b, t: (b, 0, 0, 0)),
        ],
        out_shape=[jax.ShapeDtypeStruct((batch * seq, HG_WIDTH), BF16),
                   jax.ShapeDtypeStruct((batch, HG_HEADS, HG_DIM, HG_DIM), F32)],
        scratch_shapes=[pltpu.VMEM((HG_HEADS, HG_DIM, HG_DIM), F32), scr(), scr(), scr(), scr(),
                        pltpu.VMEM((per_chunk, HG_DIM, HG_DIM), F32),
                        pltpu.VMEM((per_chunk, HG_DIM, HG_DIM), BF16)],
        compiler_params=_params("parallel", "arbitrary"),
        name="hgrn",
    )(zhg, s0, lb, gn, tri)


def _attn_prompt_kernel(jstart_ref, q_ref, k_ref, v_ref, c_ref, cq_ref, o_ref, s_scr, *, blk, nq):
    b, p, i = pl.program_id(0), pl.program_id(1), pl.program_id(2)
    q2 = q_ref[...]
    lane = lax.broadcasted_iota(jnp.int32, (blk, 2 * FOX_HD), 1)
    row_id = lax.broadcasted_iota(jnp.int32, (blk, blk), 0)
    col_id = lax.broadcasted_iota(jnp.int32, (blk, blk), 1)
    key_rows = lambda j: pl.ds(pl.multiple_of(j * blk, blk), blk)
    outs = []
    for hh in range(2):
        head = p * 2 + hh
        j0 = jstart_ref[(b * FOX_HEADS + head) * nq + i]
        in_head = (lane >= hh * FOX_HD) & (lane < (hh + 1) * FOX_HD)
        qh = jnp.where(in_head, q2, jnp.zeros_like(q2))
        c_q = cq_ref[0, 0, :, hh:hh + 1]

        def scores(j):
            return (_dot_nt(qh, k_ref[key_rows(j), :]) + c_q) - c_ref[0, 0, hh:hh + 1, key_rows(j)]

        def fill(j, m):
            s = scores(j)
            s_scr[j] = s
            return jnp.maximum(m, jnp.max(s, axis=-1, keepdims=True))

        m = lax.fori_loop(j0, i, fill, jnp.full((blk, 1), NEG_INF, F32))
        s = jnp.where(col_id <= row_id, scores(i), NEG_INF)
        s_scr[i] = s
        m = jnp.maximum(m, jnp.max(s, axis=-1, keepdims=True))

        def exponentiate(j, l):
            e = jnp.exp(s_scr[j] - m)
            s_scr[j] = e
            return l + jnp.sum(e, axis=-1, keepdims=True)

        l = lax.fori_loop(j0, i + 1, exponentiate, jnp.zeros((blk, 1), F32))
        inv_l = 1.0 / l

        def weigh(j, acc):
            return acc + _dot((s_scr[j] * inv_l).astype(BF16), v_ref[key_rows(j), :])

        outs.append(lax.fori_loop(j0, i + 1, weigh, jnp.zeros((blk, 2 * FOX_HD), F32)))
    o_ref[...] = jnp.where(lane < FOX_HD, outs[0], outs[1]).astype(BF16)


def _attn_prompt_call(jstart, qb, kb, vb, c, *, batch, seq, blk):
    nq = seq // blk
    kv_spec = pl.BlockSpec((seq, 2 * FOX_HD), lambda b, p, i, js: (b, p))
    c4 = c.reshape(batch, FOX_PAIRS, 2, seq)
    c_rows = c4.transpose(0, 1, 3, 2)
    return pl.pallas_call(
        functools.partial(_attn_prompt_kernel, blk=blk, nq=nq),
        grid_spec=pltpu.PrefetchScalarGridSpec(
            num_scalar_prefetch=1,
            grid=(batch, FOX_PAIRS, nq),
            in_specs=[
                pl.BlockSpec((blk, 2 * FOX_HD), lambda b, p, i, js: (b * nq + i, p)),
                kv_spec, kv_spec,
                pl.BlockSpec((1, 1, 2, seq), lambda b, p, i, js: (b, p, 0, 0)),
                pl.BlockSpec((1, 1, blk, 2), lambda b, p, i, js: (b, p, i, 0)),
            ],
            out_specs=pl.BlockSpec((blk, 2 * FOX_HD), lambda b, p, i, js: (b * nq + i, p)),
            scratch_shapes=[pltpu.VMEM((nq, blk, blk), F32)],
        ),
        out_shape=jax.ShapeDtypeStruct((batch * seq, FOX_WIDTH), BF16),
        compiler_params=_params("parallel", "parallel", "arbitrary"),
        name="attn_prompt",
    )(jstart, qb, kb, vb, c4, c_rows)


def _attn_block_starts(qb, kb, c, *, batch, seq, blk):
    def norms(a):
        a = a.astype(F32).reshape(batch, seq, FOX_HEADS, FOX_HD)
        return jnp.sqrt(jnp.sum(a * a, axis=-1)).transpose(0, 2, 1) * NORM_SLACK

    qn, kn = norms(qb), norms(kb)
    nq = seq // blk
    blocks = lambda a: a.reshape(batch, FOX_HEADS, nq, blk)
    k_max = jnp.max(kn, axis=-1, keepdims=True)
    need = jnp.max(blocks(qn * (k_max + kn) + c), axis=-1) + EXP_UNDERFLOW
    c_min = jnp.min(blocks(c), axis=-1)
    skip = c_min[..., None, :] > need[..., :, None]
    lead = jnp.sum(jnp.cumprod(skip.astype(jnp.int32), axis=-1), axis=-1)
    start = jnp.minimum(lead, jnp.arange(nq, dtype=jnp.int32))
    return start.reshape(-1).astype(jnp.int32)


def _attn_decode_kernel(q_ref, kn_ref, vn_ref, kc_ref, vc_ref, c_ref, cq_ref, o_ref, *, past, tt, heads):
    row_id = lax.broadcasted_iota(jnp.int32, (tt, tt), 0)
    col_id = lax.broadcasted_iota(jnp.int32, (tt, tt), 1)
    outs = []
    for hh in range(heads):
        cols = slice(hh * FOX_HD, (hh + 1) * FOX_HD)
        q = q_ref[:, cols]
        c = c_ref[0, 0, hh:hh + 1, :]
        c_q = cq_ref[0, 0, :, hh:hh + 1]
        s_c = (_dot(q, kc_ref[0, 0, hh].astype(BF16)) + c_q) - c[:, :past]
        s_n = (_dot_nt(q, kn_ref[:, cols]) + c_q) - c[:, past:]
        s_n = jnp.where(col_id <= row_id, s_n, NEG_INF)
        m = jnp.maximum(jnp.max(s_c, axis=-1, keepdims=True), jnp.max(s_n, axis=-1, keepdims=True))
        e_c = jnp.exp(s_c - m)
        e_n = jnp.exp(s_n - m)
        inv_l = 1.0 / (jnp.sum(e_c, axis=-1, keepdims=True) + jnp.sum(e_n, axis=-1, keepdims=True))
        outs.append(_dot_nt((e_c * inv_l).astype(BF16), vc_ref[0, 0, hh].astype(BF16))
                    + _dot((e_n * inv_l).astype(BF16), vn_ref[:, cols]))
    o_ref[...] = jnp.concatenate(outs, axis=-1).astype(BF16)


def _attn_decode_call(qb, kb, vb, kc_t, vc_t, c, *, layer, batch, tt):
    past = kc_t.shape[4]
    heads = DECODE_HEADS
    groups = FOX_HEADS // heads
    tok_spec = pl.BlockSpec((tt, heads * FOX_HD), lambda b, p: (b, p))
    cache_spec = pl.BlockSpec((1, 1, heads, FOX_HD, past), lambda b, p: (layer, b, p, 0, 0))
    c4 = c.reshape(batch, groups, heads, past + tt)
    c_rows = c4[..., past:].transpose(0, 1, 3, 2)
    return pl.pallas_call(
        functools.partial(_attn_decode_kernel, past=past, tt=tt, heads=heads),
        grid=(batch, groups),
        in_specs=[tok_spec, tok_spec, tok_spec, cache_spec, cache_spec,
                  pl.BlockSpec((1, 1, heads, past + tt), lambda b, p: (b, p, 0, 0)),
                  pl.BlockSpec((1, 1, tt, heads), lambda b, p: (b, p, 0, 0))],
        out_specs=tok_spec,
        out_shape=jax.ShapeDtypeStruct((batch * tt, FOX_WIDTH), BF16),
        compiler_params=_params("parallel", "parallel"),
        name="attn_decode",
    )(qb, kb, vb, kc_t, vc_t, c4, c_rows)


def _post_kernel(x_ref, ohg_ref, ofx_ref, mod_ref, nw_ref, w1_ref, w2_ref,
                 wr_ref, br_ref, sut_ref,
                 x1_ref, h2_ref, idx_ref, gate_ref, cnt_ref, cnt_s):
    i = pl.program_id(0)

    @pl.when(i == 0)
    def _():
        cnt_s[...] = jnp.zeros_like(cnt_s)

    y = _dot(ohg_ref[...], w1_ref[...]) + _dot(ofx_ref[...], w2_ref[...])
    x1 = _gated_residual(x_ref[...], y, mod_ref[0])
    x1_ref[...] = x1
    h2 = _modulate(_rms(x1) * nw_ref[...], mod_ref[2], mod_ref[1])
    h2_ref[...] = _pack_halves(h2)
    logits = _dot_nt(wr_ref[...], h2.astype(BF16)) + br_ref[...]
    e_f = lax.broadcasted_iota(jnp.int32, logits.shape, 0).astype(F32)
    work = logits
    vals, hots = [], []
    for _ in range(TOP_K):
        mx = jnp.max(work, axis=0, keepdims=True)
        ix = jnp.min(jnp.where(work == mx, e_f, float(N_EXPERTS)), axis=0, keepdims=True)
        hot = e_f == ix
        vals.append(mx)
        hots.append(hot)
        work = jnp.where(hot, -jnp.inf, work)
    exps = [jnp.exp(v - vals[0]) for v in vals]
    den = exps[0] + exps[1] + exps[2] + exps[3]
    hot_f = [h.astype(F32) for h in hots]
    hot_all = hot_f[0] + hot_f[1] + hot_f[2] + hot_f[3]
    before = cnt_s[:, 0:1] + _dot(hot_all.astype(BF16), sut_ref[...])
    for k in range(TOP_K):
        idx_ref[k:k + 1, :] = jnp.sum(hot_f[k] * e_f, axis=0, keepdims=True).astype(jnp.int32)
        idx_ref[TOP_K + k:TOP_K + k + 1, :] = jnp.sum(
            hot_f[k] * before, axis=0, keepdims=True).astype(jnp.int32)
        gate_ref[k:k + 1, :] = exps[k] / den
        gate_ref[TOP_K + k:TOP_K + k + 1, :] = jnp.zeros_like(den)
    total = cnt_s[...] + jnp.sum(hot_all, axis=1, keepdims=True)
    cnt_s[...] = total
    cnt_ref[...] = total


def _post_call(x, ohg, ofx, modg, nw, w1, w2, wr, br, sut):
    n, d = x.shape
    tm = ROW_TILE
    g = tm // MOD_GROUP
    const = lambda shape: pl.BlockSpec(shape, lambda i: (0,) * len(shape))
    row = lambda w: pl.BlockSpec((tm, w), lambda i: (i, 0))
    return pl.pallas_call(
        _post_kernel,
        grid=(n // tm,),
        in_specs=[
            row(d), row(HG_WIDTH), row(FOX_WIDTH),
            pl.BlockSpec((3, g, d), lambda i: (0, i, 0)),
            const((1, d)), const(w1.shape), const(w2.shape),
            const(wr.shape), const(br.shape), const(sut.shape),
        ],
        out_specs=[
            row(d), row(d // 2),
            pl.BlockSpec((2 * TOP_K, tm), lambda i: (0, i)),
            pl.BlockSpec((2 * TOP_K, tm), lambda i: (0, i)),
            const((N_EXPERTS, LANES)),
        ],
        out_shape=[
            jax.ShapeDtypeStruct((n, d), F32), jax.ShapeDtypeStruct((n, d // 2), U32),
            jax.ShapeDtypeStruct((2 * TOP_K, n), jnp.int32),
            jax.ShapeDtypeStruct((2 * TOP_K, n), F32),
            jax.ShapeDtypeStruct((N_EXPERTS, LANES), F32),
        ],
        scratch_shapes=[pltpu.VMEM((N_EXPERTS, LANES), F32)],
        compiler_params=_params("arbitrary"),
        name="post_mixer",
    )(x, ohg, ofx, modg, nw, w1, w2, wr, br, sut)


def _deint_kernel(w_ref, perm_ref, o_ref):
    w = w_ref[0, 0].astype(BF16)
    perm = perm_ref[...]
    f = w.shape[1] // 2
    half = MXU_DIM // 2
    for t in range(w.shape[1] // MXU_DIM):
        r = _dot(w[:, t * MXU_DIM:(t + 1) * MXU_DIM], perm).astype(BF16)
        o_ref[0, :, t * half:(t + 1) * half] = r[:, :half]
        o_ref[0, :, f + t * half:f + (t + 1) * half] = r[:, half:]


def _deint_call(w_up, *, layer):
    _, n_e, d, f2 = w_up.shape
    tr = ROW_TILE
    j = jnp.arange(MXU_DIM, dtype=jnp.int32)
    half = MXU_DIM // 2
    src = jnp.where(j < half, 2 * j, 2 * (j - half) + 1)
    perm = (j[:, None] == src[None, :]).astype(BF16)
    return pl.pallas_call(
        _deint_kernel,
        grid=(n_e, d // tr),
        in_specs=[
            pl.BlockSpec((1, 1, tr, f2), lambda e, r: (layer, e, r, 0)),
            pl.BlockSpec((MXU_DIM, MXU_DIM), lambda e, r: (0, 0)),
        ],
        out_specs=pl.BlockSpec((1, tr, f2), lambda e, r: (e, r, 0)),
        out_shape=jax.ShapeDtypeStruct((n_e, d, f2), BF16),
        compiler_params=_params("parallel", "parallel"),
        name="deint_w_up",
    )(w_up, perm)


def _expert_kernel(blk_e_ref, n_used_ref, x_ref, wu_ref, bu_ref, wd_ref, bd_ref, o_ref, wd_s):
    i = pl.program_id(0)
    used = i < n_used_ref[0]
    new_expert = jnp.logical_or(i == 0, blk_e_ref[i] != blk_e_ref[jnp.maximum(i - 1, 0)])

    @pl.when(jnp.logical_and(used, new_expert))
    def _():
        wd_s[...] = wd_ref[0, 0].astype(BF16)

    @pl.when(used)
    def _():
        x = _unpack_halves(x_ref[...]).astype(BF16)
        u = _dot(x, wu_ref[0]) + bu_ref[0]
        f = u.shape[1] // 2
        glu = jnp.minimum(u[:, :f], SWIGLU_LIMIT)
        lin = jnp.clip(u[:, f:], -SWIGLU_LIMIT, SWIGLU_LIMIT)
        a = glu * _sigmoid_pair(SWIGLU_ALPHA * glu)[0] * (lin + 1.0)
        o_ref[...] = _pack_halves(_dot(a.astype(BF16), wd_s[...]) + bd_ref[0, 0])

    @pl.when(jnp.logical_not(used))
    def _():
        o_ref[...] = jnp.zeros_like(o_ref)


def _expert_call(blk_e, n_used, xs, wu, bu, w_down, b_down, *, layer):
    p, dh = xs.shape
    d = 2 * dh
    f2 = wu.shape[2]
    tm = MOE_TILE
    return pl.pallas_call(
        _expert_kernel,
        grid_spec=pltpu.PrefetchScalarGridSpec(
            num_scalar_prefetch=2,
            grid=(p // tm,),
            in_specs=[
                pl.BlockSpec((tm, dh), lambda i, be, nu: (jnp.minimum(i, nu[0] - 1), 0)),
                pl.BlockSpec((1, d, f2), lambda i, be, nu: (be[i], 0, 0)),
                pl.BlockSpec((1, 1, f2), lambda i, be, nu: (be[i], 0, 0)),
                pl.BlockSpec((1, 1, f2 // 2, d), lambda i, be, nu: (layer, be[i], 0, 0)),
                pl.BlockSpec((1, 1, 1, d), lambda i, be, nu: (layer, be[i], 0, 0)),
            ],
            out_specs=pl.BlockSpec((tm, dh), lambda i, be, nu: (i, 0)),
            scratch_shapes=[pltpu.VMEM((f2 // 2, d), BF16)],
        ),
        out_shape=jax.ShapeDtypeStruct((p, dh), U32),
        compiler_params=_params("arbitrary"),
        name="experts",
    )(blk_e, n_used, xs, wu, bu, w_down, b_down.reshape(b_down.shape[0], N_EXPERTS, 1, d))


def _combine_kernel(x_ref, y_ref, gate_ref, mod_ref, nw_ref, o_ref, *, final):
    gates = gate_ref[...]
    y = gates[:, 0:1] * _unpack_halves(y_ref[0])
    for k in range(1, TOP_K):
        y = y + gates[:, k:k + 1] * _unpack_halves(y_ref[k])
    x2 = _gated_residual(x_ref[...], y, mod_ref[0])
    if final:
        x2 = _rms(x2) * nw_ref[...]
    o_ref[...] = x2


def _combine_call(x1, yg, gates, modg, nw, *, final):
    n, d = x1.shape
    tm = ROW_TILE
    g = tm // MOD_GROUP
    return pl.pallas_call(
        functools.partial(_combine_kernel, final=final),
        grid=(n // tm,),
        in_specs=[
            pl.BlockSpec((tm, d), lambda i: (i, 0)),
            pl.BlockSpec((TOP_K, tm, d // 2), lambda i: (0, i, 0)),
            pl.BlockSpec((tm, TOP_K), lambda i: (i, 0)),
            pl.BlockSpec((1, g, d), lambda i: (5, i, 0)),
            pl.BlockSpec((1, d), lambda i: (0, 0)),
        ],
        out_specs=pl.BlockSpec((tm, d), lambda i: (i, 0)),
        out_shape=jax.ShapeDtypeStruct((n, d), F32),
        compiler_params=_params("parallel"),
        name="combine",
    )(x1, yg, gates, modg, nw)


def _hgrn2_lower_bounds(lb_logits):
    p = jax.nn.softmax(lb_logits.astype(F32), axis=0)
    return jnp.cumsum(p, axis=0) - p[0:1]


def _cumsum_last(x):
    t = x.shape[-1]
    nb = t // LANES
    xb = x.reshape(-1, nb, LANES)
    i = jnp.arange(LANES)
    upper = (i[:, None] <= i[None, :]).astype(F32)
    within = jnp.einsum("rnk,kj->rnj", xb, upper, precision=lax.Precision.HIGHEST)
    j = jnp.arange(nb)
    strict = (j[:, None] < j[None, :]).astype(F32)
    offset = jnp.dot(within[..., -1], strict, precision=lax.Precision.HIGHEST)
    return (within + offset[..., None]).reshape(x.shape)


def _route(idx8, cnt, n_tok):
    a = n_tok * TOP_K
    n_blocks = -(-a // MOE_TILE) + N_EXPERTS
    counts = cnt[:, 0].astype(jnp.int32)
    padded = (counts + MOE_TILE - 1) // MOE_TILE * MOE_TILE
    pends = jnp.cumsum(padded)
    pstarts = pends - padded
    e_ids = jnp.arange(N_EXPERTS, dtype=jnp.int32)
    base = jnp.sum(jnp.where(idx8[:TOP_K, :, None] == e_ids, pstarts, 0), axis=-1)
    dest = base + idx8[TOP_K:]
    token = jnp.broadcast_to(jnp.arange(n_tok, dtype=jnp.int32)[None, :], (TOP_K, n_tok))
    tok_buf = jnp.zeros((n_blocks * MOE_TILE,), jnp.int32).at[dest.reshape(-1)].set(
        token.reshape(-1), unique_indices=True, mode="promise_in_bounds")
    blk_start = jnp.arange(n_blocks, dtype=jnp.int32) * MOE_TILE
    blk_e = jnp.minimum(jnp.sum(pends[None, :] <= blk_start[:, None], axis=1), N_EXPERTS - 1).astype(jnp.int32)
    n_used = (pends[-1:] // MOE_TILE).astype(jnp.int32)
    return dest, tok_buf, blk_e, n_used


def kernel(x_prompt, x_sample, c_prompt, c_sample, state_hgrn, cache_fox_k, cache_fox_v, cache_fox_lf,
           w_ada, b_ada, norm_mix, norm_ffn, w_in, lb_logits, hg_norm, fox_fb, w_out,
           w_router, b_router, w_up, b_up, w_down, b_down, norm_final):
    bp, tp, d = x_prompt.shape
    bs, ts, _ = x_sample.shape
    depth = w_in.shape[0]
    n_p, n_s = bp * tp, bs * ts
    n = n_p + n_s
    past = cache_fox_lf.shape[-1]
    assert tp % MOD_GROUP == 0 and ts == MOD_GROUP
    assert n_p % ROW_TILE == 0 and n_s % ROW_TILE == 0

    x = jnp.concatenate([x_prompt.reshape(n_p, d), x_sample.reshape(n_s, d)], axis=0)
    c = jnp.concatenate([c_prompt, c_sample], axis=0)
    bc = c.shape[0]
    c = jnp.pad(c, ((0, (-bc) % 8), (0, 0)))
    mod = _ada_call(c, w_ada, b_ada)
    mod_rows = jnp.concatenate([jnp.repeat(mod[:, :bp], tp // MOD_GROUP, axis=1), mod[:, bp:bp + bs]], axis=1)
    modg = mod_rows.reshape(depth, n // MOD_GROUP, 6, d).transpose(0, 2, 1, 3)

    lbs = _hgrn2_lower_bounds(lb_logits)
    t = jnp.arange(ROW_TILE, dtype=jnp.int32)
    sut = (t[:, None] < t[None, :]).astype(BF16)
    s0_prompt = jnp.zeros((bp, HG_HEADS, HG_DIM, HG_DIM), F32)
    f2 = w_up.shape[-1]
    blk = min(ATTN_BLOCK, tp)
    assert tp % blk == 0
    c_cache = _cumsum_last(cache_fox_lf.astype(F32))

    kc_t = jnp.swapaxes(cache_fox_k, -1, -2)
    vc_t = jnp.swapaxes(cache_fox_v, -1, -2)
    k_p = jnp.zeros((depth, bp, FOX_HEADS, FOX_HD, tp), F32)
    v_p = jnp.zeros_like(k_p)
    k_s = jnp.zeros((depth, bs, FOX_HEADS, ts, FOX_HD), F32)
    v_s = jnp.zeros_like(k_s)
    new_s_p, new_lf_p, new_s_s, new_lf_s = [], [], [], []
    for l in range(depth):
        o4 = 4 * HG_WIDTH
        w_in_b = w_in[l].astype(BF16)
        whg = w_in_b[:, :o4]
        wfx = w_in_b[:, o4:o4 + 3 * FOX_WIDTH]
        wffT = w_in_b[:, o4 + 3 * FOX_WIDTH:].T
        fb = fox_fb[l].astype(F32).reshape(FOX_HEADS, 1)
        nw_mix = norm_mix[l].reshape(1, d)
        pre = functools.partial(_pre_call, x, modg[l], nw_mix, whg, wfx, wffT, fb, layer=l)
        zhg_p, qb_p, kb_p, vb_p, k_p, v_p, lfT_p = pre(k_p, v_p, row0=0, batch=bp, seq=tp)
        zhg_s, qb_s, kb_s, vb_s, k_s, v_s, lfT_s = pre(k_s, v_s, row0=n_p, batch=bs, seq=ts)
        lf_p = lfT_p.reshape(FOX_HEADS, bp, tp).transpose(1, 0, 2)
        lf_s = lfT_s.reshape(FOX_HEADS, bs, ts).transpose(1, 0, 2)

        lb = lbs[l].reshape(1, HG_WIDTH)
        gn = hg_norm[l].astype(F32).reshape(1, HG_DIM)
        ohg_p, s_p = _hgrn_call(zhg_p, s0_prompt, lb, gn, batch=bp, seq=tp)
        ohg_s, s_s = _hgrn_call(zhg_s, state_hgrn[l].astype(F32), lb, gn, batch=bs, seq=ts)

        c_p = jnp.cumsum(lf_p, axis=-1)
        jstart = _attn_block_starts(qb_p, kb_p, c_p, batch=bp, seq=tp, blk=blk)
        ofx_p = _attn_prompt_call(jstart, qb_p, kb_p, vb_p, c_p, batch=bp, seq=tp, blk=blk)
        c_new = c_cache[l][..., -1:] + jnp.cumsum(lf_s, axis=-1)
        c_s = jnp.concatenate([c_cache[l], c_new], axis=-1)
        ofx_s = _attn_decode_call(qb_s, kb_s, vb_s, kc_t, vc_t, c_s, layer=l, batch=bs, tt=ts)

        ohg = jnp.concatenate([ohg_p, ohg_s], axis=0)
        ofx = jnp.concatenate([ofx_p, ofx_s], axis=0)
        w_out_b = w_out[l].astype(BF16)
        x1, h2, idx8, gate8, cnt = _post_call(
            x, ohg, ofx, modg[l, 2:5], norm_ffn[l].reshape(1, d), w_out_b[:HG_WIDTH], w_out_b[HG_WIDTH:],
            w_router[l].T.astype(BF16), b_router[l].astype(F32).reshape(N_EXPERTS, 1), sut)

        dest, tok_buf, blk_e, n_used = _route(idx8, cnt, n)
        xs = h2.at[tok_buf].get(mode="promise_in_bounds")
        wu = _deint_call(w_up, layer=l)
        bu = jnp.concatenate([b_up[l][:, 0::2], b_up[l][:, 1::2]], axis=-1).reshape(N_EXPERTS, 1, f2)
        yb = _expert_call(blk_e, n_used, xs, wu, bu, w_down, b_down, layer=l)
        yg = yb.at[dest].get(mode="promise_in_bounds")
        x = _combine_call(x1, yg, gate8[:TOP_K].T, modg[l], norm_final.reshape(1, d),
                          final=(l == depth - 1))

        new_s_p.append(s_p); new_lf_p.append(lf_p)
        new_s_s.append(s_s); new_lf_s.append(lf_s)

    y_prompt = x[:n_p].reshape(bp, tp, d)
    y_sample = x[n_p:].reshape(bs, ts, d)
    st = jnp.stack
    tr = lambda a: jnp.swapaxes(a, -1, -2)
    return (y_prompt, y_sample, st(new_s_p), tr(k_p), tr(v_p), st(new_lf_p),
            st(new_s_s), k_s, v_s, st(new_lf_s))
```

```python
import functools

import jax
import jax.numpy as jnp
from jax import lax
from jax.experimental import pallas as pl
from jax.experimental.pallas import tpu as pltpu

F32 = jnp.float32
BF16 = jnp.bfloat16
U32 = jnp.uint32

HG_HEADS = 4
HG_DIM = 128
HG_WIDTH = HG_HEADS * HG_DIM
FOX_HEADS = 8
FOX_HD = 64
FOX_WIDTH = FOX_HEADS * FOX_HD
FOX_PAIRS = FOX_HEADS // 2
N_EXPERTS = 32
TOP_K = 4
GLA_BLOCK = 16
MOD_GROUP = 64
SWIGLU_ALPHA = 1.702
SWIGLU_LIMIT = 7.0
EPS = 1e-6
NEG_INF = -1e30
F_FLOOR = 1e-30
EXP_UNDERFLOW = 88.0
NORM_SLACK = 1.01

LANES = 128
MXU_DIM = 256
ROW_TILE = 512
HGRN_CHUNK = 256
ATTN_BLOCK = 256
MOE_TILE = 512
DECODE_HEADS = 4
VMEM_LIMIT = 56 * 1024 * 1024


def _params(*sem):
    return pltpu.CompilerParams(dimension_semantics=sem, vmem_limit_bytes=VMEM_LIMIT)


def _dot(a, b):
    return jnp.dot(a, b, preferred_element_type=F32)


def _dot_nt(a, b):
    return lax.dot_general(a, b, (((1,), (1,)), ((), ())), preferred_element_type=F32)


def _dot_tn(a, b):
    return lax.dot_general(a, b, (((0,), (0,)), ((), ())), preferred_element_type=F32)


def _sigmoid_pair(z):
    e = jnp.exp(-jnp.abs(z))
    r = 1.0 / (1.0 + e)
    er = e * r
    pos = z >= 0.0
    return jnp.where(pos, r, er), jnp.where(pos, er, r)


def _silu(z):
    return z * _sigmoid_pair(z)[0]


def _log_sigmoid(z):
    return jnp.minimum(z, 0.0) - jnp.log(1.0 + jnp.exp(-jnp.abs(z)))


def _rms(x):
    return x * lax.rsqrt(jnp.mean(x * x, axis=-1, keepdims=True) + EPS)


def _modulate(y, scale, shift):
    rows, d = y.shape
    g = rows // MOD_GROUP
    y3 = y.reshape(g, MOD_GROUP, d)
    return (y3 * (1.0 + scale[:, None, :]) + shift[:, None, :]).reshape(rows, d)


def _gated_residual(x, y, gate):
    rows, d = y.shape
    g = rows // MOD_GROUP
    return x + (y.reshape(g, MOD_GROUP, d) * gate[:, None, :]).reshape(rows, d)


def _pack_halves(y):
    w = y.shape[1] // 2
    hi = lax.bitcast_convert_type(y[:, :w].astype(BF16).astype(F32), U32)
    lo = lax.bitcast_convert_type(y[:, w:].astype(BF16).astype(F32), U32)
    return hi | (lo >> 16)


def _unpack_halves(u):
    hi = lax.bitcast_convert_type(u & jnp.uint32(0xFFFF0000), F32)
    lo = lax.bitcast_convert_type(u << 16, F32)
    return jnp.concatenate([hi, lo], axis=-1)


def _ada_kernel(c_ref, w_ref, b_ref, o_ref):
    s = _silu(c_ref[...])
    o_ref[0] = _dot(s.astype(BF16), w_ref[0].astype(BF16)) + b_ref[0]


def _ada_call(c, w_ada, b_ada):
    depth, d, d6 = w_ada.shape
    bc = c.shape[0]
    tn = d6 // 4
    return pl.pallas_call(
        _ada_kernel,
        grid=(depth, d6 // tn),
        in_specs=[
            pl.BlockSpec((bc, d), lambda l, j: (0, 0)),
            pl.BlockSpec((1, d, tn), lambda l, j: (l, 0, j)),
            pl.BlockSpec((1, 1, tn), lambda l, j: (l, 0, j)),
        ],
        out_specs=pl.BlockSpec((1, bc, tn), lambda l, j: (l, 0, j)),
        out_shape=jax.ShapeDtypeStruct((depth, bc, d6), F32),
        compiler_params=_params("parallel", "parallel"),
        name="ada_mod",
    )(c, w_ada, b_ada.reshape(depth, 1, d6))


def _pre_kernel(x_ref, mod_ref, nw_ref, whg_ref, wfx_ref, wff_ref, fb_ref, ind_ref, k_acc, v_acc,
                zhg_ref, q_ref, kb_ref, vb_ref, k_ref, v_ref, lf_ref, nrm_ref, *, nb, tt, time_minor):
    del k_acc, v_acc
    x = x_ref[...]
    h = _modulate(_rms(x) * nw_ref[...], mod_ref[1], mod_ref[0])
    hb = h.astype(BF16)
    zhg_ref[...] = _dot(hb, whg_ref[...])
    zfx = _dot(hb, wfx_ref[...])
    q = zfx[:, :FOX_WIDTH] * (FOX_HD ** -0.5)
    k = zfx[:, FOX_WIDTH:2 * FOX_WIDTH]
    v = zfx[:, 2 * FOX_WIDTH:]
    qb, kb = q.astype(BF16), k.astype(BF16)
    q_ref[...] = qb
    kb_ref[...] = kb
    vb_ref[...] = v.astype(BF16)
    ind = ind_ref[...]
    nrm_ref[:FOX_HEADS, :] = _dot_nt(ind, (qb * qb))
    nrm_ref[FOX_HEADS:, :] = _dot_nt(ind, (kb * kb))
    if time_minor:
        k, v = k.T, v.T
    for b in range(nb):
        for hd in range(FOX_HEADS):
            rows = slice(b * tt, (b + 1) * tt)
            cols = slice(hd * FOX_HD, (hd + 1) * FOX_HD)
            k_ref[0, b, hd] = k[cols, rows] if time_minor else k[rows, cols]
            v_ref[0, b, hd] = v[cols, rows] if time_minor else v[rows, cols]
    ff = _dot_nt(wff_ref[...], hb)
    lf_ref[...] = _log_sigmoid(ff + fb_ref[...])


def _pre_call(x, modg, nw, whg, wfx, wffT, fb, ind, k_acc, v_acc, *, layer, row0, batch, seq):
    d = x.shape[1]
    time_minor = k_acc.shape[-2:] == (FOX_HD, seq) and seq != FOX_HD
    n = batch * seq
    tm = ROW_TILE
    tt = min(seq, tm)
    nb = tm // tt
    assert n % tm == 0 and row0 % tm == 0 and tm % tt == 0 and seq % tt == 0
    i0 = row0 // tm
    g = tm // MOD_GROUP
    tpb = seq // tt

    hm_block = (1, nb, FOX_HEADS, FOX_HD, tt) if time_minor else (1, nb, FOX_HEADS, tt, FOX_HD)

    def head_major_idx(i):
        if nb > 1:
            return (layer, i, 0, 0, 0)
        return (layer, i // tpb, 0, 0, i % tpb) if time_minor else (layer, i // tpb, 0, i % tpb, 0)

    tok = lambda w, dt: jax.ShapeDtypeStruct((n, w), dt)
    hm = jax.ShapeDtypeStruct(k_acc.shape, F32)
    const = lambda shape: pl.BlockSpec(shape, lambda i: (0,) * len(shape))
    any_spec = pl.BlockSpec(memory_space=pl.ANY)
    return pl.pallas_call(
        functools.partial(_pre_kernel, nb=nb, tt=tt, time_minor=time_minor),
        grid=(n // tm,),
        in_specs=[
            pl.BlockSpec((tm, d), lambda i: (i + i0, 0)),
            pl.BlockSpec((2, g, d), lambda i: (0, i + i0, 0)),
            const((1, d)),
            const(whg.shape), const(wfx.shape), const(wffT.shape), const(fb.shape), const(ind.shape),
            any_spec, any_spec,
        ],
        out_specs=[
            pl.BlockSpec((tm, 4 * HG_WIDTH), lambda i: (i, 0)),
            pl.BlockSpec((tm, FOX_WIDTH), lambda i: (i, 0)),
            pl.BlockSpec((tm, FOX_WIDTH), lambda i: (i, 0)),
            pl.BlockSpec((tm, FOX_WIDTH), lambda i: (i, 0)),
            pl.BlockSpec(hm_block, head_major_idx),
            pl.BlockSpec(hm_block, head_major_idx),
            pl.BlockSpec((FOX_HEADS, tm), lambda i: (0, i)),
            pl.BlockSpec((2 * FOX_HEADS, tm), lambda i: (0, i)),
        ],
        out_shape=[tok(4 * HG_WIDTH, F32), tok(FOX_WIDTH, BF16), tok(FOX_WIDTH, BF16),
                   tok(FOX_WIDTH, BF16), hm, hm, jax.ShapeDtypeStruct((FOX_HEADS, n), F32),
                   jax.ShapeDtypeStruct((2 * FOX_HEADS, n), F32)],
        input_output_aliases={8: 4, 9: 5},
        compiler_params=_params("parallel"),
        name="pre_mixer",
    )(x, modg, nw, whg, wfx, wffT, fb, ind, k_acc, v_acc)


def _hgrn_kernel(z_ref, s0_ref, lb_ref, gn_ref, tri_ref, o_acc, o_ref, s_ref,
                 st_ref, q_s, k_s, b_s, oi_s, u_s, sb_s, qe_s, *, chunk):
    del o_acc
    t = pl.program_id(1)
    L = GLA_BLOCK
    W = HG_WIDTH

    @pl.when(t == 0)
    def _():
        for h in range(HG_HEADS):
            st_ref[h] = s0_ref[0, h].T

    lb = lb_ref[...]
    zf = z_ref[:, W:2 * W]
    sig, sig_neg = _sigmoid_pair(zf)
    f = lb + (1.0 - lb) * sig
    g = jnp.log(jnp.maximum(f, F_FLOOR))
    k_s[...] = (1.0 - lb) * sig_neg
    q_s[...] = _silu(z_ref[:, :W]) * (HG_DIM ** -0.5)
    g0 = g.astype(BF16)
    r1 = g - g0.astype(F32)
    g1 = r1.astype(BF16)
    g2 = (r1 - g1.astype(F32)).astype(BF16)
    tri = tri_ref[...]
    cs = tri.shape[0]
    for r in range(chunk // cs):
        rows = slice(r * cs, (r + 1) * cs)
        b_s[rows, :] = _dot(tri, g0[rows]) + _dot(tri, g1[rows]) + _dot(tri, g2[rows])

    H = L // 2
    row_id = lax.broadcasted_iota(jnp.int32, (L, L), 0)
    col_id = lax.broadcasted_iota(jnp.int32, (L, L), 1)
    col_hi = lax.broadcasted_iota(jnp.int32, (H, L), 1)
    gn = gn_ref[...]
    nblk = chunk // L
    block_rows = lambda i: pl.ds(pl.multiple_of(i * L, L), L)

    def intra(i, carry):
        rows = block_rows(i)
        for h in range(HG_HEADS):
            cols = slice(h * HG_DIM, (h + 1) * HG_DIM)
            q = q_s[rows, cols]
            kk = k_s[rows, cols]
            b = b_s[rows, cols]
            vb = z_ref[rows, 2 * W + h * HG_DIM:2 * W + (h + 1) * HG_DIM].astype(BF16)
            att = jnp.zeros((L, L), F32)
            for s in range(H):
                dec = jnp.exp(jnp.minimum(b - b[s:s + 1, :], 0.0))
                col = jnp.sum(q * kk[s:s + 1, :] * dec, axis=-1, keepdims=True)
                att = jnp.where(col_id == s, col, att)
            att_hi = att[H:]
            q_hi, b_hi = q[H:], b[H:]
            for s in range(H, L):
                dec = jnp.exp(jnp.minimum(b_hi - b[s:s + 1, :], 0.0))
                col = jnp.sum(q_hi * kk[s:s + 1, :] * dec, axis=-1, keepdims=True)
                att_hi = jnp.where(col_hi == s, col, att_hi)
            att = jnp.concatenate([att[:H], att_hi], axis=0)
            att = jnp.where(row_id >= col_id, att, 0.0)
            oi_s[rows, cols] = _dot(att.astype(BF16), vb)
            kd = kk * jnp.exp(b[L - 1:L, :] - b)
            u_s[i * HG_HEADS + h] = _dot_tn(vb, kd.astype(BF16))
        return carry

    def scan(i, carry):
        tail = pl.ds(pl.multiple_of(i * L + H, H), H)
        for h in range(HG_HEADS):
            cols = slice(h * HG_DIM, (h + 1) * HG_DIM)
            st = st_ref[h]
            sb_s[i * HG_HEADS + h] = st.astype(BF16)
            st_ref[h] = st * jnp.exp(b_s[tail, cols][H - 1:H, :]) + u_s[i * HG_HEADS + h]
        return carry

    def inter(i, carry):
        rows = block_rows(i)
        for h in range(HG_HEADS):
            cols = slice(h * HG_DIM, (h + 1) * HG_DIM)
            oi_s[rows, cols] += _dot_nt(qe_s[rows, cols], sb_s[i * HG_HEADS + h])
        return carry

    unroll = 2 if nblk % 2 == 0 else 1
    lax.fori_loop(0, nblk, intra, 0, unroll=unroll)
    lax.fori_loop(0, nblk, scan, 0, unroll=unroll)
    qe_s[...] = (q_s[...] * jnp.exp(b_s[...])).astype(BF16)
    lax.fori_loop(0, nblk, inter, 0, unroll=4 if nblk % 4 == 0 else unroll)
    for h in range(HG_HEADS):
        cols = slice(h * HG_DIM, (h + 1) * HG_DIM)
        hg = z_ref[:, 3 * W + h * HG_DIM:3 * W + (h + 1) * HG_DIM]
        o_ref[:, cols] = (_rms(oi_s[:, cols]) * gn * _silu(hg)).astype(BF16)

    @pl.when(t == pl.num_programs(1) - 1)
    def _():
        for h in range(HG_HEADS):
            s_ref[0, h] = st_ref[h].T


def _hgrn_call(zhg, s0, lb, gn, o_acc, *, row0, batch, seq):
    chunk = min(HGRN_CHUNK, seq)
    cs = min(LANES, chunk)
    assert seq % chunk == 0 and chunk % cs == 0 and cs % GLA_BLOCK == 0 and row0 % chunk == 0
    c0 = row0 // chunk
    r = jnp.arange(cs, dtype=jnp.int32)
    tri = ((r[:, None] // GLA_BLOCK == r[None, :] // GLA_BLOCK) & (r[None, :] <= r[:, None])).astype(BF16)
    nt = seq // chunk
    scr = lambda: pltpu.VMEM((chunk, HG_WIDTH), F32)
    per_chunk = chunk // GLA_BLOCK * HG_HEADS
    return pl.pallas_call(
        functools.partial(_hgrn_kernel, chunk=chunk),
        grid=(batch, nt),
        in_specs=[
            pl.BlockSpec((chunk, 4 * HG_WIDTH), lambda b, t: (b * nt + t, 0)),
            pl.BlockSpec((1, HG_HEADS, HG_DIM, HG_DIM), lambda b, t: (b, 0, 0, 0)),
            pl.BlockSpec((1, HG_WIDTH), lambda b, t: (0, 0)),
            pl.BlockSpec((1, HG_DIM), lambda b, t: (0, 0)),
            pl.BlockSpec((cs, cs), lambda b, t: (0, 0)),
            pl.BlockSpec(memory_space=pl.ANY),
        ],
        out_specs=[
            pl.BlockSpec((chunk, HG_WIDTH), lambda b, t: (c0 + b * nt + t, 0)),
            pl.BlockSpec((1, HG_HEADS, HG_DIM, HG_DIM), lambda b, t: (b, 0, 0, 0)),
        ],
        out_shape=[jax.ShapeDtypeStruct(o_acc.shape, BF16),
                   jax.ShapeDtypeStruct((batch, HG_HEADS, HG_DIM, HG_DIM), F32)],
        input_output_aliases={5: 0},
        scratch_shapes=[pltpu.VMEM((HG_HEADS, HG_DIM, HG_DIM), F32), scr(), scr(), scr(), scr(),
                        pltpu.VMEM((per_chunk, HG_DIM, HG_DIM), F32),
                        pltpu.VMEM((per_chunk, HG_DIM, HG_DIM), BF16),
                        pltpu.VMEM((chunk, HG_WIDTH), BF16)],
        compiler_params=_params("parallel", "arbitrary"),
        name="hgrn",
    )(zhg, s0, lb, gn, tri, o_acc)


def _attn_prompt_kernel(jstart_ref, q_ref, k_ref, v_ref, c_ref, cq_ref, o_acc, o_ref, s_scr, *, blk, nq):
    del o_acc
    b, p, i = pl.program_id(0), pl.program_id(1), pl.program_id(2)
    q2 = q_ref[...]
    lane = lax.broadcasted_iota(jnp.int32, (blk, 2 * FOX_HD), 1)
    row_id = lax.broadcasted_iota(jnp.int32, (blk, blk), 0)
    col_id = lax.broadcasted_iota(jnp.int32, (blk, blk), 1)
    key_rows = lambda j: pl.ds(pl.multiple_of(j * blk, blk), blk)
    first = (b * FOX_HEADS + p * 2) * nq + i
    j0 = jnp.minimum(jstart_ref[first], jstart_ref[first + nq])
    qh = [jnp.where((lane >= hh * FOX_HD) & (lane < (hh + 1) * FOX_HD), q2, jnp.zeros_like(q2)) for hh in range(2)]
    c_q = [cq_ref[0, 0, :, hh:hh + 1] for hh in range(2)]

    def fill(j, carry, causal=False):
        out = []
        for hh in range(2):
            m, l = carry[2 * hh], carry[2 * hh + 1]
            s = (_dot_nt(qh[hh], k_ref[key_rows(j), :]) + c_q[hh]) - c_ref[0, 0, hh:hh + 1, key_rows(j)]
            if causal:
                s = jnp.where(col_id <= row_id, s, NEG_INF)
            s_scr[hh, j] = s
            m_new = jnp.maximum(m, jnp.max(s, axis=-1, keepdims=True))
            l = l * jnp.exp(m - m_new) + jnp.sum(jnp.exp(s - m_new), axis=-1, keepdims=True)
            out += [m_new, l]
        return tuple(out)

    neg = jnp.full((blk, 1), NEG_INF, F32)
    zero = jnp.zeros((blk, 1), F32)
    stats = lax.fori_loop(j0, i, fill, (neg, zero, neg, zero))
    m0, l0, m1, l1 = fill(i, stats, causal=True)
    ms = (m0, m1)
    inv_l = (1.0 / l0, 1.0 / l1)

    def weigh(j, accs):
        vblk = v_ref[key_rows(j), :]
        return tuple(accs[hh] + _dot((jnp.exp(s_scr[hh, j] - ms[hh]) * inv_l[hh]).astype(BF16), vblk)
                     for hh in range(2))

    acc = jnp.zeros((blk, 2 * FOX_HD), F32)
    outs = lax.fori_loop(j0, i + 1, weigh, (acc, acc))
    o_ref[...] = jnp.where(lane < FOX_HD, outs[0], outs[1]).astype(BF16)


def _attn_prompt_call(jstart, qb, kb, vb, c, o_acc, *, batch, seq, blk):
    nq = seq // blk
    kv_spec = pl.BlockSpec((seq, 2 * FOX_HD), lambda b, p, i, js: (b, p))
    c4 = c.reshape(batch, FOX_PAIRS, 2, seq)
    c_rows = c4.transpose(0, 1, 3, 2)
    return pl.pallas_call(
        functools.partial(_attn_prompt_kernel, blk=blk, nq=nq),
        grid_spec=pltpu.PrefetchScalarGridSpec(
            num_scalar_prefetch=1,
            grid=(batch, FOX_PAIRS, nq),
            in_specs=[
                pl.BlockSpec((blk, 2 * FOX_HD), lambda b, p, i, js: (b * nq + i, p)),
                kv_spec, kv_spec,
                pl.BlockSpec((1, 1, 2, seq), lambda b, p, i, js: (b, p, 0, 0)),
                pl.BlockSpec((1, 1, blk, 2), lambda b, p, i, js: (b, p, i, 0)),
                pl.BlockSpec(memory_space=pl.ANY),
            ],
            out_specs=pl.BlockSpec((blk, 2 * FOX_HD), lambda b, p, i, js: (b * nq + i, p)),
            scratch_shapes=[pltpu.VMEM((2, nq, blk, blk), F32)],
        ),
        out_shape=jax.ShapeDtypeStruct(o_acc.shape, BF16),
        input_output_aliases={6: 0},
        compiler_params=_params("parallel", "parallel", "arbitrary"),
        name="attn_prompt",
    )(jstart, qb, kb, vb, c4, c_rows, o_acc)


def _attn_block_starts(sq_norms, c, *, batch, seq, blk):
    def norms(sq):
        return jnp.sqrt(sq).reshape(FOX_HEADS, batch, seq).transpose(1, 0, 2) * NORM_SLACK

    qn, kn = norms(sq_norms[:FOX_HEADS]), norms(sq_norms[FOX_HEADS:])
    nq = seq // blk
    blocks = lambda a: a.reshape(batch, FOX_HEADS, nq, blk)
    k_max = jnp.max(kn, axis=-1, keepdims=True)
    need = jnp.max(blocks(qn * (k_max + kn) + c), axis=-1) + EXP_UNDERFLOW
    c_min = jnp.min(blocks(c), axis=-1)
    skip = c_min[..., None, :] > need[..., :, None]
    lead = jnp.sum(jnp.cumprod(skip.astype(jnp.int32), axis=-1), axis=-1)
    start = jnp.minimum(lead, jnp.arange(nq, dtype=jnp.int32))
    return start.reshape(-1).astype(jnp.int32)


def _attn_decode_kernel(q_ref, kn_ref, vn_ref, kc_ref, vc_ref, c_ref, cq_ref, o_acc, o_ref, *, past, tt, heads):
    del o_acc
    row_id = lax.broadcasted_iota(jnp.int32, (tt, tt), 0)
    col_id = lax.broadcasted_iota(jnp.int32, (tt, tt), 1)
    outs = []
    for hh in range(heads):
        cols = slice(hh * FOX_HD, (hh + 1) * FOX_HD)
        q = q_ref[:, cols]
        c = c_ref[0, 0, hh:hh + 1, :]
        c_q = cq_ref[0, 0, :, hh:hh + 1]
        s_c = (_dot(q, kc_ref[0, 0, hh].astype(BF16)) + c_q) - c[:, :past]
        s_n = (_dot_nt(q, kn_ref[:, cols]) + c_q) - c[:, past:]
        s_n = jnp.where(col_id <= row_id, s_n, NEG_INF)
        m = jnp.maximum(jnp.max(s_c, axis=-1, keepdims=True), jnp.max(s_n, axis=-1, keepdims=True))
        e_c = jnp.exp(s_c - m)
        e_n = jnp.exp(s_n - m)
        inv_l = 1.0 / (jnp.sum(e_c, axis=-1, keepdims=True) + jnp.sum(e_n, axis=-1, keepdims=True))
        outs.append(_dot_nt((e_c * inv_l).astype(BF16), vc_ref[0, 0, hh].astype(BF16))
                    + _dot((e_n * inv_l).astype(BF16), vn_ref[:, cols]))
    o_ref[...] = jnp.concatenate(outs, axis=-1).astype(BF16)


def _attn_decode_call(qb, kb, vb, kc_t, vc_t, c, o_acc, *, layer, row0, batch, tt):
    past = kc_t.shape[4]
    heads = DECODE_HEADS
    groups = FOX_HEADS // heads
    assert row0 % tt == 0
    r0 = row0 // tt
    tok_spec = pl.BlockSpec((tt, heads * FOX_HD), lambda b, p: (b, p))
    cache_spec = pl.BlockSpec((1, 1, heads, FOX_HD, past), lambda b, p: (layer, b, p, 0, 0))
    c4 = c.reshape(batch, groups, heads, past + tt)
    c_rows = c4[..., past:].transpose(0, 1, 3, 2)
    return pl.pallas_call(
        functools.partial(_attn_decode_kernel, past=past, tt=tt, heads=heads),
        grid=(batch, groups),
        in_specs=[tok_spec, tok_spec, tok_spec, cache_spec, cache_spec,
                  pl.BlockSpec((1, 1, heads, past + tt), lambda b, p: (b, p, 0, 0)),
                  pl.BlockSpec((1, 1, tt, heads), lambda b, p: (b, p, 0, 0)),
                  pl.BlockSpec(memory_space=pl.ANY)],
        out_specs=pl.BlockSpec((tt, heads * FOX_HD), lambda b, p: (r0 + b, p)),
        out_shape=jax.ShapeDtypeStruct(o_acc.shape, BF16),
        input_output_aliases={7: 0},
        compiler_params=_params("parallel", "parallel"),
        name="attn_decode",
    )(qb, kb, vb, kc_t, vc_t, c4, c_rows, o_acc)


def _post_kernel(x_ref, ohg_ref, ofx_ref, mod_ref, nw_ref, w1_ref, w2_ref,
                 wr_ref, br_ref, sut_ref,
                 x1_ref, h2_ref, idx_ref, gate_ref, cnt_ref, cnt_s):
    i = pl.program_id(0)

    @pl.when(i == 0)
    def _():
        cnt_s[...] = jnp.zeros_like(cnt_s)

    y = _dot(ohg_ref[...], w1_ref[...]) + _dot(ofx_ref[...], w2_ref[...])
    x1 = _gated_residual(x_ref[...], y, mod_ref[0])
    x1_ref[...] = x1
    h2 = _modulate(_rms(x1) * nw_ref[...], mod_ref[2], mod_ref[1])
    h2_ref[...] = _pack_halves(h2)
    logits = _dot_nt(wr_ref[...], h2.astype(BF16)) + br_ref[...]
    e_f = lax.broadcasted_iota(jnp.int32, logits.shape, 0).astype(F32)
    work = logits
    vals, hots = [], []
    for _ in range(TOP_K):
        mx = jnp.max(work, axis=0, keepdims=True)
        ix = jnp.min(jnp.where(work == mx, e_f, float(N_EXPERTS)), axis=0, keepdims=True)
        hot = e_f == ix
        vals.append(mx)
        hots.append(hot)
        work = jnp.where(hot, -jnp.inf, work)
    exps = [jnp.exp(v - vals[0]) for v in vals]
    den = exps[0] + exps[1] + exps[2] + exps[3]
    hot_f = [h.astype(F32) for h in hots]
    hot_all = hot_f[0] + hot_f[1] + hot_f[2] + hot_f[3]
    before = cnt_s[:, 0:1] + _dot(hot_all.astype(BF16), sut_ref[...])
    for k in range(TOP_K):
        idx_ref[k:k + 1, :] = jnp.sum(hot_f[k] * e_f, axis=0, keepdims=True).astype(jnp.int32)
        idx_ref[TOP_K + k:TOP_K + k + 1, :] = jnp.sum(
            hot_f[k] * before, axis=0, keepdims=True).astype(jnp.int32)
        gate_ref[k:k + 1, :] = exps[k] / den
        gate_ref[TOP_K + k:TOP_K + k + 1, :] = jnp.zeros_like(den)
    total = cnt_s[...] + jnp.sum(hot_all, axis=1, keepdims=True)
    cnt_s[...] = total
    cnt_ref[...] = total


def _post_call(x, ohg, ofx, modg, nw, w1, w2, wr, br, sut):
    n, d = x.shape
    tm = ROW_TILE
    g = tm // MOD_GROUP
    const = lambda shape: pl.BlockSpec(shape, lambda i: (0,) * len(shape))
    row = lambda w: pl.BlockSpec((tm, w), lambda i: (i, 0))
    return pl.pallas_call(
        _post_kernel,
        grid=(n // tm,),
        in_specs=[
            row(d), row(HG_WIDTH), row(FOX_WIDTH),
            pl.BlockSpec((3, g, d), lambda i: (0, i, 0)),
            const((1, d)), const(w1.shape), const(w2.shape),
            const(wr.shape), const(br.shape), const(sut.shape),
        ],
        out_specs=[
            row(d), row(d // 2),
            pl.BlockSpec((2 * TOP_K, tm), lambda i: (0, i)),
            pl.BlockSpec((2 * TOP_K, tm), lambda i: (0, i)),
            const((N_EXPERTS, LANES)),
        ],
        out_shape=[
            jax.ShapeDtypeStruct((n, d), F32), jax.ShapeDtypeStruct((n, d // 2), U32),
            jax.ShapeDtypeStruct((2 * TOP_K, n), jnp.int32),
            jax.ShapeDtypeStruct((2 * TOP_K, n), F32),
            jax.ShapeDtypeStruct((N_EXPERTS, LANES), F32),
        ],
        scratch_shapes=[pltpu.VMEM((N_EXPERTS, LANES), F32)],
        compiler_params=_params("arbitrary"),
        name="post_mixer",
    )(x, ohg, ofx, modg, nw, w1, w2, wr, br, sut)


def _deint_kernel(w_ref, perm_ref, o_ref):
    w = w_ref[0, 0].astype(BF16)
    perm = perm_ref[...]
    f = w.shape[1] // 2
    half = MXU_DIM // 2
    for t in range(w.shape[1] // MXU_DIM):
        r = _dot(w[:, t * MXU_DIM:(t + 1) * MXU_DIM], perm).astype(BF16)
        o_ref[0, :, t * half:(t + 1) * half] = r[:, :half]
        o_ref[0, :, f + t * half:f + (t + 1) * half] = r[:, half:]


def _deint_call(w_up, *, layer):
    _, n_e, d, f2 = w_up.shape
    tr = ROW_TILE
    j = jnp.arange(MXU_DIM, dtype=jnp.int32)
    half = MXU_DIM // 2
    src = jnp.where(j < half, 2 * j, 2 * (j - half) + 1)
    perm = (j[:, None] == src[None, :]).astype(BF16)
    return pl.pallas_call(
        _deint_kernel,
        grid=(n_e, d // tr),
        in_specs=[
            pl.BlockSpec((1, 1, tr, f2), lambda e, r: (layer, e, r, 0)),
            pl.BlockSpec((MXU_DIM, MXU_DIM), lambda e, r: (0, 0)),
        ],
        out_specs=pl.BlockSpec((1, tr, f2), lambda e, r: (e, r, 0)),
        out_shape=jax.ShapeDtypeStruct((n_e, d, f2), BF16),
        compiler_params=_params("parallel", "parallel"),
        name="deint_w_up",
    )(w_up, perm)


def _expert_kernel(blk_e_ref, n_used_ref, x_ref, wu_ref, bu_ref, wd_ref, bd_ref, o_ref, wd_s):
    i = pl.program_id(0)
    used = i < n_used_ref[0]
    new_expert = jnp.logical_or(i == 0, blk_e_ref[i] != blk_e_ref[jnp.maximum(i - 1, 0)])

    @pl.when(jnp.logical_and(used, new_expert))
    def _():
        wd_s[...] = wd_ref[0, 0].astype(BF16)

    @pl.when(used)
    def _():
        x = _unpack_halves(x_ref[...]).astype(BF16)
        u = _dot(x, wu_ref[0]) + bu_ref[0]
        f = u.shape[1] // 2
        glu = jnp.minimum(u[:, :f], SWIGLU_LIMIT)
        lin = jnp.clip(u[:, f:], -SWIGLU_LIMIT, SWIGLU_LIMIT)
        a = glu * _sigmoid_pair(SWIGLU_ALPHA * glu)[0] * (lin + 1.0)
        o_ref[...] = _pack_halves(_dot(a.astype(BF16), wd_s[...]) + bd_ref[0, 0])

    @pl.when(jnp.logical_not(used))
    def _():
        o_ref[...] = jnp.zeros_like(o_ref)


def _expert_call(blk_e, n_used, xs, wu, bu, w_down, b_down, *, layer):
    p, dh = xs.shape
    d = 2 * dh
    f2 = wu.shape[2]
    tm = MOE_TILE
    return pl.pallas_call(
        _expert_kernel,
        grid_spec=pltpu.PrefetchScalarGridSpec(
            num_scalar_prefetch=2,
            grid=(p // tm,),
            in_specs=[
                pl.BlockSpec((tm, dh), lambda i, be, nu: (jnp.minimum(i, nu[0] - 1), 0)),
                pl.BlockSpec((1, d, f2), lambda i, be, nu: (be[i], 0, 0)),
                pl.BlockSpec((1, 1, f2), lambda i, be, nu: (be[i], 0, 0)),
                pl.BlockSpec((1, 1, f2 // 2, d), lambda i, be, nu: (layer, be[i], 0, 0)),
                pl.BlockSpec((1, 1, 1, d), lambda i, be, nu: (layer, be[i], 0, 0)),
            ],
            out_specs=pl.BlockSpec((tm, dh), lambda i, be, nu: (i, 0)),
            scratch_shapes=[pltpu.VMEM((f2 // 2, d), BF16)],
        ),
        out_shape=jax.ShapeDtypeStruct((p, dh), U32),
        compiler_params=_params("arbitrary"),
        name="experts",
    )(blk_e, n_used, xs, wu, bu, w_down, b_down.reshape(b_down.shape[0], N_EXPERTS, 1, d))


def _combine_kernel(x_ref, y_ref, gate_ref, mod_ref, nw_ref, o_ref, *, final):
    gates = gate_ref[...]
    y = gates[:, 0:1] * _unpack_halves(y_ref[0])
    for k in range(1, TOP_K):
        y = y + gates[:, k:k + 1] * _unpack_halves(y_ref[k])
    x2 = _gated_residual(x_ref[...], y, mod_ref[0])
    if final:
        x2 = _rms(x2) * nw_ref[...]
    o_ref[...] = x2


def _combine_call(x1, yg, gates, modg, nw, *, final):
    n, d = x1.shape
    tm = ROW_TILE
    g = tm // MOD_GROUP
    return pl.pallas_call(
        functools.partial(_combine_kernel, final=final),
        grid=(n // tm,),
        in_specs=[
            pl.BlockSpec((tm, d), lambda i: (i, 0)),
            pl.BlockSpec((TOP_K, tm, d // 2), lambda i: (0, i, 0)),
            pl.BlockSpec((tm, TOP_K), lambda i: (i, 0)),
            pl.BlockSpec((1, g, d), lambda i: (5, i, 0)),
            pl.BlockSpec((1, d), lambda i: (0, 0)),
        ],
        out_specs=pl.BlockSpec((tm, d), lambda i: (i, 0)),
        out_shape=jax.ShapeDtypeStruct((n, d), F32),
        compiler_params=_params("parallel"),
        name="combine",
    )(x1, yg, gates, modg, nw)


def _hgrn2_lower_bounds(lb_logits):
    p = jax.nn.softmax(lb_logits.astype(F32), axis=0)
    return jnp.cumsum(p, axis=0) - p[0:1]


def _cumsum_last(x):
    t = x.shape[-1]
    nb = t // LANES
    xb = x.reshape(-1, nb, LANES)
    i = jnp.arange(LANES)
    upper = (i[:, None] <= i[None, :]).astype(F32)
    within = jnp.einsum("rnk,kj->rnj", xb, upper, precision=lax.Precision.HIGHEST)
    j = jnp.arange(nb)
    strict = (j[:, None] < j[None, :]).astype(F32)
    offset = jnp.dot(within[..., -1], strict, precision=lax.Precision.HIGHEST)
    return (within + offset[..., None]).reshape(x.shape)


def _route(idx8, cnt, n_tok):
    a = n_tok * TOP_K
    n_blocks = -(-a // MOE_TILE) + N_EXPERTS
    counts = cnt[:, 0].astype(jnp.int32)
    padded = (counts + MOE_TILE - 1) // MOE_TILE * MOE_TILE
    pends = jnp.cumsum(padded)
    pstarts = pends - padded
    e_ids = jnp.arange(N_EXPERTS, dtype=jnp.int32)
    base = jnp.sum(jnp.where(idx8[:TOP_K, :, None] == e_ids, pstarts, 0), axis=-1)
    dest = base + idx8[TOP_K:]
    token = jnp.broadcast_to(jnp.arange(n_tok, dtype=jnp.int32)[None, :], (TOP_K, n_tok))
    tok_buf = jnp.zeros((n_blocks * MOE_TILE,), jnp.int32).at[dest.reshape(-1)].set(
        token.reshape(-1), unique_indices=True, mode="promise_in_bounds")
    blk_start = jnp.arange(n_blocks, dtype=jnp.int32) * MOE_TILE
    blk_e = jnp.minimum(jnp.sum(pends[None, :] <= blk_start[:, None], axis=1), N_EXPERTS - 1).astype(jnp.int32)
    n_used = (pends[-1:] // MOE_TILE).astype(jnp.int32)
    return dest, tok_buf, blk_e, n_used


def kernel(x_prompt, x_sample, c_prompt, c_sample, state_hgrn, cache_fox_k, cache_fox_v, cache_fox_lf,
           w_ada, b_ada, norm_mix, norm_ffn, w_in, lb_logits, hg_norm, fox_fb, w_out,
           w_router, b_router, w_up, b_up, w_down, b_down, norm_final):
    bp, tp, d = x_prompt.shape
    bs, ts, _ = x_sample.shape
    depth = w_in.shape[0]
    n_p, n_s = bp * tp, bs * ts
    n = n_p + n_s
    past = cache_fox_lf.shape[-1]
    assert tp % MOD_GROUP == 0 and ts == MOD_GROUP
    assert n_p % ROW_TILE == 0 and n_s % ROW_TILE == 0

    x = jnp.concatenate([x_prompt.reshape(n_p, d), x_sample.reshape(n_s, d)], axis=0)
    c = jnp.concatenate([c_prompt, c_sample], axis=0)
    bc = c.shape[0]
    c = jnp.pad(c, ((0, (-bc) % 8), (0, 0)))
    mod = _ada_call(c, w_ada, b_ada)
    mod_rows = jnp.concatenate([jnp.repeat(mod[:, :bp], tp // MOD_GROUP, axis=1), mod[:, bp:bp + bs]], axis=1)
    modg = mod_rows.reshape(depth, n // MOD_GROUP, 6, d).transpose(0, 2, 1, 3)

    lbs = _hgrn2_lower_bounds(lb_logits)
    t = jnp.arange(ROW_TILE, dtype=jnp.int32)
    sut = (t[:, None] < t[None, :]).astype(BF16)
    s0_prompt = jnp.zeros((bp, HG_HEADS, HG_DIM, HG_DIM), F32)
    f2 = w_up.shape[-1]
    blk = min(ATTN_BLOCK, tp)
    assert tp % blk == 0
    c_cache = _cumsum_last(cache_fox_lf.astype(F32))
    head_ind = (jnp.arange(FOX_WIDTH)[None, :] // FOX_HD == jnp.arange(FOX_HEADS)[:, None]).astype(BF16)

    kc_t = jnp.swapaxes(cache_fox_k, -1, -2)
    vc_t = jnp.swapaxes(cache_fox_v, -1, -2)
    k_p = jnp.zeros((depth, bp, FOX_HEADS, FOX_HD, tp), F32)
    v_p = jnp.zeros_like(k_p)
    k_s = jnp.zeros((depth, bs, FOX_HEADS, ts, FOX_HD), F32)
    v_s = jnp.zeros_like(k_s)
    new_s_p, new_lf_p, new_s_s, new_lf_s = [], [], [], []
    for l in range(depth):
        o4 = 4 * HG_WIDTH
        w_in_b = w_in[l].astype(BF16)
        whg = w_in_b[:, :o4]
        wfx = w_in_b[:, o4:o4 + 3 * FOX_WIDTH]
        wffT = w_in_b[:, o4 + 3 * FOX_WIDTH:].T
        fb = fox_fb[l].astype(F32).reshape(FOX_HEADS, 1)
        nw_mix = norm_mix[l].reshape(1, d)
        pre = functools.partial(_pre_call, x, modg[l], nw_mix, whg, wfx, wffT, fb, head_ind, layer=l)
        zhg_p, qb_p, kb_p, vb_p, k_p, v_p, lfT_p, sq_p = pre(k_p, v_p, row0=0, batch=bp, seq=tp)
        zhg_s, qb_s, kb_s, vb_s, k_s, v_s, lfT_s, _ = pre(k_s, v_s, row0=n_p, batch=bs, seq=ts)
        lf_p = lfT_p.reshape(FOX_HEADS, bp, tp).transpose(1, 0, 2)
        lf_s = lfT_s.reshape(FOX_HEADS, bs, ts).transpose(1, 0, 2)

        lb = lbs[l].reshape(1, HG_WIDTH)
        gn = hg_norm[l].astype(F32).reshape(1, HG_DIM)
        mix0 = jnp.zeros((n, HG_WIDTH), BF16)
        ohg, s_p = _hgrn_call(zhg_p, s0_prompt, lb, gn, mix0, row0=0, batch=bp, seq=tp)
        ohg, s_s = _hgrn_call(zhg_s, state_hgrn[l].astype(F32), lb, gn, ohg, row0=n_p, batch=bs, seq=ts)

        c_p = jnp.cumsum(lf_p, axis=-1)
        jstart = _attn_block_starts(sq_p, c_p, batch=bp, seq=tp, blk=blk)
        ofx = _attn_prompt_call(jstart, qb_p, kb_p, vb_p, c_p, jnp.zeros((n, FOX_WIDTH), BF16),
                                batch=bp, seq=tp, blk=blk)
        c_new = c_cache[l][..., -1:] + jnp.cumsum(lf_s, axis=-1)
        c_s = jnp.concatenate([c_cache[l], c_new], axis=-1)
        ofx = _attn_decode_call(qb_s, kb_s, vb_s, kc_t, vc_t, c_s, ofx, layer=l, row0=n_p, batch=bs, tt=ts)

        w_out_b = w_out[l].astype(BF16)
        x1, h2, idx8, gate8, cnt = _post_call(
            x, ohg, ofx, modg[l, 2:5], norm_ffn[l].reshape(1, d), w_out_b[:HG_WIDTH], w_out_b[HG_WIDTH:],
            w_router[l].T.astype(BF16), b_router[l].astype(F32).reshape(N_EXPERTS, 1), sut)

        dest, tok_buf, blk_e, n_used = _route(idx8, cnt, n)
        xs = h2.at[tok_buf.reshape(-1, MOE_TILE)].get(mode="promise_in_bounds").reshape(-1, d // 2)
        wu = _deint_call(w_up, layer=l)
        bu = jnp.concatenate([b_up[l][:, 0::2], b_up[l][:, 1::2]], axis=-1).reshape(N_EXPERTS, 1, f2)
        yb = _expert_call(blk_e, n_used, xs, wu, bu, w_down, b_down, layer=l)
        yg = yb.at[dest].get(mode="promise_in_bounds")
        x = _combine_call(x1, yg, gate8[:TOP_K].T, modg[l], norm_final.reshape(1, d),
                          final=(l == depth - 1))

        new_s_p.append(s_p); new_lf_p.append(lf_p)
        new_s_s.append(s_s); new_lf_s.append(lf_s)

    y_prompt = x[:n_p].reshape(bp, tp, d)
    y_sample = x[n_p:].reshape(bs, ts, d)
    st = jnp.stack
    tr = lambda a: jnp.swapaxes(a, -1, -2)
    return (y_prompt, y_sample, st(new_s_p), tr(k_p), tr(v_p), st(new_lf_p),
            st(new_s_s), k_s, v_s, st(new_lf_s))
```

```python
import functools

import jax
import jax.numpy as jnp
from jax import lax
from jax.experimental import pallas as pl
from jax.experimental.pallas import tpu as pltpu

F32 = jnp.float32
BF16 = jnp.bfloat16
U32 = jnp.uint32

HG_HEADS = 4
HG_DIM = 128
HG_WIDTH = HG_HEADS * HG_DIM
FOX_HEADS = 8
FOX_HD = 64
FOX_WIDTH = FOX_HEADS * FOX_HD
FOX_PAIRS = FOX_HEADS // 2
N_EXPERTS = 32
TOP_K = 4
GLA_BLOCK = 16
MOD_GROUP = 64
SWIGLU_ALPHA = 1.702
SWIGLU_LIMIT = 7.0
EPS = 1e-6
NEG_INF = -1e30
F_FLOOR = 1e-30
EXP_UNDERFLOW = 88.0
NORM_SLACK = 1.01

LANES = 128
MXU_DIM = 256
ROW_TILE = 512
HGRN_CHUNK = 256
ATTN_BLOCK = 256
ATTN_WINDOW = 3
MOE_TILE = 512
DECODE_HEADS = 4
VMEM_LIMIT = 56 * 1024 * 1024


def _params(*sem):
    return pltpu.CompilerParams(dimension_semantics=sem, vmem_limit_bytes=VMEM_LIMIT)


def _dot(a, b):
    return jnp.dot(a, b, preferred_element_type=F32)


def _dot_nt(a, b):
    return lax.dot_general(a, b, (((1,), (1,)), ((), ())), preferred_element_type=F32)


def _dot_tn(a, b):
    return lax.dot_general(a, b, (((0,), (0,)), ((), ())), preferred_element_type=F32)


def _sigmoid_pair(z):
    e = jnp.exp(-jnp.abs(z))
    r = 1.0 / (1.0 + e)
    er = e * r
    pos = z >= 0.0
    return jnp.where(pos, r, er), jnp.where(pos, er, r)


def _silu(z):
    return z * _sigmoid_pair(z)[0]


def _log_sigmoid(z):
    return jnp.minimum(z, 0.0) - jnp.log(1.0 + jnp.exp(-jnp.abs(z)))


def _rms(x):
    return x * lax.rsqrt(jnp.mean(x * x, axis=-1, keepdims=True) + EPS)


def _modulate(y, scale, shift):
    rows, d = y.shape
    g = rows // MOD_GROUP
    y3 = y.reshape(g, MOD_GROUP, d)
    return (y3 * (1.0 + scale[:, None, :]) + shift[:, None, :]).reshape(rows, d)


def _gated_residual(x, y, gate):
    rows, d = y.shape
    g = rows // MOD_GROUP
    return x + (y.reshape(g, MOD_GROUP, d) * gate[:, None, :]).reshape(rows, d)


def _pack_halves(y):
    w = y.shape[1] // 2
    hi = lax.bitcast_convert_type(y[:, :w].astype(BF16).astype(F32), U32)
    lo = lax.bitcast_convert_type(y[:, w:].astype(BF16).astype(F32), U32)
    return hi | (lo >> 16)


def _unpack_halves(u):
    hi = lax.bitcast_convert_type(u & jnp.uint32(0xFFFF0000), F32)
    lo = lax.bitcast_convert_type(u << 16, F32)
    return jnp.concatenate([hi, lo], axis=-1)


def _ada_kernel(c_ref, w_ref, b_ref, o_ref):
    s = _silu(c_ref[...])
    o_ref[0] = _dot(s.astype(BF16), w_ref[0].astype(BF16)) + b_ref[0]


def _ada_call(c, w_ada, b_ada):
    depth, d, d6 = w_ada.shape
    bc = c.shape[0]
    tn = d6 // 4
    return pl.pallas_call(
        _ada_kernel,
        grid=(depth, d6 // tn),
        in_specs=[
            pl.BlockSpec((bc, d), lambda l, j: (0, 0)),
            pl.BlockSpec((1, d, tn), lambda l, j: (l, 0, j)),
            pl.BlockSpec((1, 1, tn), lambda l, j: (l, 0, j)),
        ],
        out_specs=pl.BlockSpec((1, bc, tn), lambda l, j: (l, 0, j)),
        out_shape=jax.ShapeDtypeStruct((depth, bc, d6), F32),
        compiler_params=_params("parallel", "parallel"),
        name="ada_mod",
    )(c, w_ada, b_ada.reshape(depth, 1, d6))


def _pre_kernel(x_ref, mod_ref, nw_ref, whg_ref, wfx_ref, wff_ref, fb_ref, ind_ref, k_acc, v_acc,
                zhg_ref, q_ref, kb_ref, vb_ref, k_ref, v_ref, lf_ref, nrm_ref, *, nb, tt, time_minor):
    del k_acc, v_acc
    x = x_ref[...]
    h = _modulate(_rms(x) * nw_ref[...], mod_ref[1], mod_ref[0])
    hb = h.astype(BF16)
    zhg_ref[...] = _dot(hb, whg_ref[...])
    zfx = _dot(hb, wfx_ref[...])
    q = zfx[:, :FOX_WIDTH] * (FOX_HD ** -0.5)
    k = zfx[:, FOX_WIDTH:2 * FOX_WIDTH]
    v = zfx[:, 2 * FOX_WIDTH:]
    qb, kb = q.astype(BF16), k.astype(BF16)
    q_ref[...] = qb
    kb_ref[...] = kb
    vb_ref[...] = v.astype(BF16)
    ind = ind_ref[...]
    nrm_ref[:FOX_HEADS, :] = _dot_nt(ind, (qb * qb))
    nrm_ref[FOX_HEADS:, :] = _dot_nt(ind, (kb * kb))
    if time_minor:
        k, v = k.T, v.T
    for b in range(nb):
        for hd in range(FOX_HEADS):
            rows = slice(b * tt, (b + 1) * tt)
            cols = slice(hd * FOX_HD, (hd + 1) * FOX_HD)
            k_ref[0, b, hd] = k[cols, rows] if time_minor else k[rows, cols]
            v_ref[0, b, hd] = v[cols, rows] if time_minor else v[rows, cols]
    ff = _dot_nt(wff_ref[...], hb)
    lf_ref[...] = _log_sigmoid(ff + fb_ref[...])


def _pre_call(x, modg, nw, whg, wfx, wffT, fb, ind, k_acc, v_acc, *, layer, row0, batch, seq):
    d = x.shape[1]
    time_minor = k_acc.shape[-2:] == (FOX_HD, seq) and seq != FOX_HD
    n = batch * seq
    tm = ROW_TILE
    tt = min(seq, tm)
    nb = tm // tt
    assert n % tm == 0 and row0 % tm == 0 and tm % tt == 0 and seq % tt == 0
    i0 = row0 // tm
    g = tm // MOD_GROUP
    tpb = seq // tt

    hm_block = (1, nb, FOX_HEADS, FOX_HD, tt) if time_minor else (1, nb, FOX_HEADS, tt, FOX_HD)

    def head_major_idx(i):
        if nb > 1:
            return (layer, i, 0, 0, 0)
        return (layer, i // tpb, 0, 0, i % tpb) if time_minor else (layer, i // tpb, 0, i % tpb, 0)

    tok = lambda w, dt: jax.ShapeDtypeStruct((n, w), dt)
    hm = jax.ShapeDtypeStruct(k_acc.shape, F32)
    const = lambda shape: pl.BlockSpec(shape, lambda i: (0,) * len(shape))
    any_spec = pl.BlockSpec(memory_space=pl.ANY)
    return pl.pallas_call(
        functools.partial(_pre_kernel, nb=nb, tt=tt, time_minor=time_minor),
        grid=(n // tm,),
        in_specs=[
            pl.BlockSpec((tm, d), lambda i: (i + i0, 0)),
            pl.BlockSpec((2, g, d), lambda i: (0, i + i0, 0)),
            const((1, d)),
            const(whg.shape), const(wfx.shape), const(wffT.shape), const(fb.shape), const(ind.shape),
            any_spec, any_spec,
        ],
        out_specs=[
            pl.BlockSpec((tm, 4 * HG_WIDTH), lambda i: (i, 0)),
            pl.BlockSpec((tm, FOX_WIDTH), lambda i: (i, 0)),
            pl.BlockSpec((tm, FOX_WIDTH), lambda i: (i, 0)),
            pl.BlockSpec((tm, FOX_WIDTH), lambda i: (i, 0)),
            pl.BlockSpec(hm_block, head_major_idx),
            pl.BlockSpec(hm_block, head_major_idx),
            pl.BlockSpec((FOX_HEADS, tm), lambda i: (0, i)),
            pl.BlockSpec((2 * FOX_HEADS, tm), lambda i: (0, i)),
        ],
        out_shape=[tok(4 * HG_WIDTH, F32), tok(FOX_WIDTH, BF16), tok(FOX_WIDTH, BF16),
                   tok(FOX_WIDTH, BF16), hm, hm, jax.ShapeDtypeStruct((FOX_HEADS, n), F32),
                   jax.ShapeDtypeStruct((2 * FOX_HEADS, n), F32)],
        input_output_aliases={8: 4, 9: 5},
        compiler_params=_params("parallel"),
        name="pre_mixer",
    )(x, modg, nw, whg, wfx, wffT, fb, ind, k_acc, v_acc)


def _hgrn_kernel(z_ref, s0_ref, lb_ref, gn_ref, tri_ref, o_acc, o_ref, s_ref,
                 st_ref, q_s, k_s, b_s, oi_s, u_s, sb_s, qe_s, *, chunk):
    del o_acc
    t = pl.program_id(1)
    L = GLA_BLOCK
    W = HG_WIDTH

    @pl.when(t == 0)
    def _():
        for h in range(HG_HEADS):
            st_ref[h] = s0_ref[0, h].T

    lb = lb_ref[...]
    zf = z_ref[:, W:2 * W]
    sig, sig_neg = _sigmoid_pair(zf)
    f = lb + (1.0 - lb) * sig
    g = jnp.log(jnp.maximum(f, F_FLOOR))
    k_s[...] = (1.0 - lb) * sig_neg
    q_s[...] = _silu(z_ref[:, :W]) * (HG_DIM ** -0.5)
    g0 = g.astype(BF16)
    r1 = g - g0.astype(F32)
    g1 = r1.astype(BF16)
    g2 = (r1 - g1.astype(F32)).astype(BF16)
    tri = tri_ref[...]
    cs = tri.shape[0]
    for r in range(chunk // cs):
        rows = slice(r * cs, (r + 1) * cs)
        b_s[rows, :] = _dot(tri, g0[rows]) + _dot(tri, g1[rows]) + _dot(tri, g2[rows])

    H = L // 2
    row_id = lax.broadcasted_iota(jnp.int32, (L, L), 0)
    col_id = lax.broadcasted_iota(jnp.int32, (L, L), 1)
    col_hi = lax.broadcasted_iota(jnp.int32, (H, L), 1)
    gn = gn_ref[...]
    nblk = chunk // L
    block_rows = lambda i: pl.ds(pl.multiple_of(i * L, L), L)

    def intra(i, carry):
        rows = block_rows(i)
        for h in range(HG_HEADS):
            cols = slice(h * HG_DIM, (h + 1) * HG_DIM)
            q = q_s[rows, cols]
            kk = k_s[rows, cols]
            b = b_s[rows, cols]
            vb = z_ref[rows, 2 * W + h * HG_DIM:2 * W + (h + 1) * HG_DIM].astype(BF16)
            att = jnp.zeros((L, L), F32)
            for s in range(H):
                dec = jnp.exp(jnp.minimum(b - b[s:s + 1, :], 0.0))
                col = jnp.sum(q * kk[s:s + 1, :] * dec, axis=-1, keepdims=True)
                att = jnp.where(col_id == s, col, att)
            att_hi = att[H:]
            q_hi, b_hi = q[H:], b[H:]
            for s in range(H, L):
                dec = jnp.exp(jnp.minimum(b_hi - b[s:s + 1, :], 0.0))
                col = jnp.sum(q_hi * kk[s:s + 1, :] * dec, axis=-1, keepdims=True)
                att_hi = jnp.where(col_hi == s, col, att_hi)
            att = jnp.concatenate([att[:H], att_hi], axis=0)
            att = jnp.where(row_id >= col_id, att, 0.0)
            oi_s[rows, cols] = _dot(att.astype(BF16), vb)
            kd = kk * jnp.exp(b[L - 1:L, :] - b)
            u_s[i * HG_HEADS + h] = _dot_tn(vb, kd.astype(BF16))
        return carry

    def scan(i, carry):
        tail = pl.ds(pl.multiple_of(i * L + H, H), H)
        for h in range(HG_HEADS):
            cols = slice(h * HG_DIM, (h + 1) * HG_DIM)
            st = st_ref[h]
            sb_s[i * HG_HEADS + h] = st.astype(BF16)
            st_ref[h] = st * jnp.exp(b_s[tail, cols][H - 1:H, :]) + u_s[i * HG_HEADS + h]
        return carry

    def inter(i, carry):
        rows = block_rows(i)
        for h in range(HG_HEADS):
            cols = slice(h * HG_DIM, (h + 1) * HG_DIM)
            oi_s[rows, cols] += _dot_nt(qe_s[rows, cols], sb_s[i * HG_HEADS + h])
        return carry

    unroll = 2 if nblk % 2 == 0 else 1
    lax.fori_loop(0, nblk, intra, 0, unroll=unroll)
    lax.fori_loop(0, nblk, scan, 0, unroll=unroll)
    qe_s[...] = (q_s[...] * jnp.exp(b_s[...])).astype(BF16)
    lax.fori_loop(0, nblk, inter, 0, unroll=4 if nblk % 4 == 0 else unroll)
    for h in range(HG_HEADS):
        cols = slice(h * HG_DIM, (h + 1) * HG_DIM)
        hg = z_ref[:, 3 * W + h * HG_DIM:3 * W + (h + 1) * HG_DIM]
        o_ref[:, cols] = (_rms(oi_s[:, cols]) * gn * _silu(hg)).astype(BF16)

    @pl.when(t == pl.num_programs(1) - 1)
    def _():
        for h in range(HG_HEADS):
            s_ref[0, h] = st_ref[h].T


def _hgrn_call(zhg, s0, lb, gn, o_acc, *, row0, batch, seq):
    chunk = min(HGRN_CHUNK, seq)
    cs = min(LANES, chunk)
    assert seq % chunk == 0 and chunk % cs == 0 and cs % GLA_BLOCK == 0 and row0 % chunk == 0
    c0 = row0 // chunk
    r = jnp.arange(cs, dtype=jnp.int32)
    tri = ((r[:, None] // GLA_BLOCK == r[None, :] // GLA_BLOCK) & (r[None, :] <= r[:, None])).astype(BF16)
    nt = seq // chunk
    scr = lambda: pltpu.VMEM((chunk, HG_WIDTH), F32)
    per_chunk = chunk // GLA_BLOCK * HG_HEADS
    return pl.pallas_call(
        functools.partial(_hgrn_kernel, chunk=chunk),
        grid=(batch, nt),
        in_specs=[
            pl.BlockSpec((chunk, 4 * HG_WIDTH), lambda b, t: (b * nt + t, 0)),
            pl.BlockSpec((1, HG_HEADS, HG_DIM, HG_DIM), lambda b, t: (b, 0, 0, 0)),
            pl.BlockSpec((1, HG_WIDTH), lambda b, t: (0, 0)),
            pl.BlockSpec((1, HG_DIM), lambda b, t: (0, 0)),
            pl.BlockSpec((cs, cs), lambda b, t: (0, 0)),
            pl.BlockSpec(memory_space=pl.ANY),
        ],
        out_specs=[
            pl.BlockSpec((chunk, HG_WIDTH), lambda b, t: (c0 + b * nt + t, 0)),
            pl.BlockSpec((1, HG_HEADS, HG_DIM, HG_DIM), lambda b, t: (b, 0, 0, 0)),
        ],
        out_shape=[jax.ShapeDtypeStruct(o_acc.shape, BF16),
                   jax.ShapeDtypeStruct((batch, HG_HEADS, HG_DIM, HG_DIM), F32)],
        input_output_aliases={5: 0},
        scratch_shapes=[pltpu.VMEM((HG_HEADS, HG_DIM, HG_DIM), F32), scr(), scr(), scr(), scr(),
                        pltpu.VMEM((per_chunk, HG_DIM, HG_DIM), F32),
                        pltpu.VMEM((per_chunk, HG_DIM, HG_DIM), BF16),
                        pltpu.VMEM((chunk, HG_WIDTH), BF16)],
        compiler_params=_params("parallel", "arbitrary"),
        name="hgrn",
    )(zhg, s0, lb, gn, tri, o_acc)


def _attn_prompt_kernel(jstart_ref, q_ref, k_ref, v_ref, c_ref, cq_ref, o_acc, o_ref, s_scr, *, blk, nq):
    del o_acc
    b, p, i = pl.program_id(0), pl.program_id(1), pl.program_id(2)
    q2 = q_ref[...]
    lane = lax.broadcasted_iota(jnp.int32, (blk, 2 * FOX_HD), 1)
    row_id = lax.broadcasted_iota(jnp.int32, (blk, blk), 0)
    col_id = lax.broadcasted_iota(jnp.int32, (blk, blk), 1)
    key_rows = lambda j: pl.ds(pl.multiple_of(j * blk, blk), blk)
    first = (b * FOX_HEADS + p * 2) * nq + i
    j0 = jnp.minimum(jstart_ref[first], jstart_ref[first + nq])
    qh = [jnp.where((lane >= hh * FOX_HD) & (lane < (hh + 1) * FOX_HD), q2, jnp.zeros_like(q2)) for hh in range(2)]
    c_q = [cq_ref[0, 0, :, hh:hh + 1] for hh in range(2)]

    def fill(j, carry, causal=False):
        out = []
        for hh in range(2):
            m, l = carry[2 * hh], carry[2 * hh + 1]
            s = (_dot_nt(qh[hh], k_ref[key_rows(j), :]) + c_q[hh]) - c_ref[0, 0, hh:hh + 1, key_rows(j)]
            if causal:
                s = jnp.where(col_id <= row_id, s, NEG_INF)
            s_scr[hh, j] = s
            m_new = jnp.maximum(m, jnp.max(s, axis=-1, keepdims=True))
            l = l * jnp.exp(m - m_new) + jnp.sum(jnp.exp(s - m_new), axis=-1, keepdims=True)
            out += [m_new, l]
        return tuple(out)

    def store(outs):
        o_ref[...] = jnp.where(lane < FOX_HD, outs[0], outs[1]).astype(BF16)

    win = min(ATTN_WINDOW, nq)

    @pl.when(i - j0 >= win)
    def _():
        neg = jnp.full((blk, 1), NEG_INF, F32)
        zero = jnp.zeros((blk, 1), F32)
        stats = lax.fori_loop(j0, i, fill, (neg, zero, neg, zero))
        m0, l0, m1, l1 = fill(i, stats, causal=True)
        ms = (m0, m1)
        inv_l = (1.0 / l0, 1.0 / l1)

        def weigh(j, accs):
            vblk = v_ref[key_rows(j), :]
            return tuple(accs[hh] + _dot((jnp.exp(s_scr[hh, j] - ms[hh]) * inv_l[hh]).astype(BF16), vblk)
                         for hh in range(2))

        acc = jnp.zeros((blk, 2 * FOX_HD), F32)
        store(lax.fori_loop(j0, i + 1, weigh, (acc, acc)))

    @pl.when(i - j0 < win)
    def _():
        start = jnp.maximum(i - (win - 1), 0)
        keys = pl.ds(pl.multiple_of(start * blk, blk), win * blk)
        key_pos = start * blk + lax.broadcasted_iota(jnp.int32, (blk, win * blk), 1)
        visible = key_pos <= i * blk + lax.broadcasted_iota(jnp.int32, (blk, win * blk), 0)
        kwin, vwin = k_ref[keys, :], v_ref[keys, :]
        outs = []
        for hh in range(2):
            s = (_dot_nt(qh[hh], kwin) + c_q[hh]) - c_ref[0, 0, hh:hh + 1, keys]
            s = jnp.where(visible, s, NEG_INF)
            e = jnp.exp(s - jnp.max(s, axis=-1, keepdims=True))
            inv = 1.0 / jnp.sum(e, axis=-1, keepdims=True)
            outs.append(_dot((e * inv).astype(BF16), vwin))
        store(outs)


def _attn_prompt_call(jstart, qb, kb, vb, c, o_acc, *, batch, seq, blk):
    nq = seq // blk
    kv_spec = pl.BlockSpec((seq, 2 * FOX_HD), lambda b, p, i, js: (b, p))
    c4 = c.reshape(batch, FOX_PAIRS, 2, seq)
    c_rows = c4.transpose(0, 1, 3, 2)
    return pl.pallas_call(
        functools.partial(_attn_prompt_kernel, blk=blk, nq=nq),
        grid_spec=pltpu.PrefetchScalarGridSpec(
            num_scalar_prefetch=1,
            grid=(batch, FOX_PAIRS, nq),
            in_specs=[
                pl.BlockSpec((blk, 2 * FOX_HD), lambda b, p, i, js: (b * nq + i, p)),
                kv_spec, kv_spec,
                pl.BlockSpec((1, 1, 2, seq), lambda b, p, i, js: (b, p, 0, 0)),
                pl.BlockSpec((1, 1, blk, 2), lambda b, p, i, js: (b, p, i, 0)),
                pl.BlockSpec(memory_space=pl.ANY),
            ],
            out_specs=pl.BlockSpec((blk, 2 * FOX_HD), lambda b, p, i, js: (b * nq + i, p)),
            scratch_shapes=[pltpu.VMEM((2, nq, blk, blk), F32)],
        ),
        out_shape=jax.ShapeDtypeStruct(o_acc.shape, BF16),
        input_output_aliases={6: 0},
        compiler_params=_params("parallel", "parallel", "arbitrary"),
        name="attn_prompt",
    )(jstart, qb, kb, vb, c4, c_rows, o_acc)


def _attn_block_starts(sq_norms, c, *, batch, seq, blk):
    def norms(sq):
        return jnp.sqrt(sq).reshape(FOX_HEADS, batch, seq).transpose(1, 0, 2) * NORM_SLACK

    qn, kn = norms(sq_norms[:FOX_HEADS]), norms(sq_norms[FOX_HEADS:])
    nq = seq // blk
    blocks = lambda a: a.reshape(batch, FOX_HEADS, nq, blk)
    k_max = jnp.max(kn, axis=-1, keepdims=True)
    need = jnp.max(blocks(qn * (k_max + kn) + c), axis=-1) + EXP_UNDERFLOW
    c_min = jnp.min(blocks(c), axis=-1)
    skip = c_min[..., None, :] > need[..., :, None]
    lead = jnp.sum(jnp.cumprod(skip.astype(jnp.int32), axis=-1), axis=-1)
    start = jnp.minimum(lead, jnp.arange(nq, dtype=jnp.int32))
    return start.reshape(-1).astype(jnp.int32)


def _attn_decode_kernel(q_ref, kn_ref, vn_ref, kc_ref, vc_ref, c_ref, cq_ref, o_acc, o_ref, *, past, tt, heads):
    del o_acc
    row_id = lax.broadcasted_iota(jnp.int32, (tt, tt), 0)
    col_id = lax.broadcasted_iota(jnp.int32, (tt, tt), 1)
    outs = []
    for hh in range(heads):
        cols = slice(hh * FOX_HD, (hh + 1) * FOX_HD)
        q = q_ref[:, cols]
        c = c_ref[0, 0, hh:hh + 1, :]
        c_q = cq_ref[0, 0, :, hh:hh + 1]
        s_c = (_dot(q, kc_ref[0, 0, hh].astype(BF16)) + c_q) - c[:, :past]
        s_n = (_dot_nt(q, kn_ref[:, cols]) + c_q) - c[:, past:]
        s_n = jnp.where(col_id <= row_id, s_n, NEG_INF)
        m = jnp.maximum(jnp.max(s_c, axis=-1, keepdims=True), jnp.max(s_n, axis=-1, keepdims=True))
        e_c = jnp.exp(s_c - m)
        e_n = jnp.exp(s_n - m)
        inv_l = 1.0 / (jnp.sum(e_c, axis=-1, keepdims=True) + jnp.sum(e_n, axis=-1, keepdims=True))
        outs.append(_dot_nt((e_c * inv_l).astype(BF16), vc_ref[0, 0, hh].astype(BF16))
                    + _dot((e_n * inv_l).astype(BF16), vn_ref[:, cols]))
    o_ref[...] = jnp.concatenate(outs, axis=-1).astype(BF16)


def _attn_decode_call(qb, kb, vb, kc_t, vc_t, c, o_acc, *, layer, row0, batch, tt):
    past = kc_t.shape[4]
    heads = DECODE_HEADS
    groups = FOX_HEADS // heads
    assert row0 % tt == 0
    r0 = row0 // tt
    tok_spec = pl.BlockSpec((tt, heads * FOX_HD), lambda b, p: (b, p))
    cache_spec = pl.BlockSpec((1, 1, heads, FOX_HD, past), lambda b, p: (layer, b, p, 0, 0))
    c4 = c.reshape(batch, groups, heads, past + tt)
    c_rows = c4[..., past:].transpose(0, 1, 3, 2)
    return pl.pallas_call(
        functools.partial(_attn_decode_kernel, past=past, tt=tt, heads=heads),
        grid=(batch, groups),
        in_specs=[tok_spec, tok_spec, tok_spec, cache_spec, cache_spec,
                  pl.BlockSpec((1, 1, heads, past + tt), lambda b, p: (b, p, 0, 0)),
                  pl.BlockSpec((1, 1, tt, heads), lambda b, p: (b, p, 0, 0)),
                  pl.BlockSpec(memory_space=pl.ANY)],
        out_specs=pl.BlockSpec((tt, heads * FOX_HD), lambda b, p: (r0 + b, p)),
        out_shape=jax.ShapeDtypeStruct(o_acc.shape, BF16),
        input_output_aliases={7: 0},
        compiler_params=_params("parallel", "parallel"),
        name="attn_decode",
    )(qb, kb, vb, kc_t, vc_t, c4, c_rows, o_acc)


def _post_kernel(x_ref, ohg_ref, ofx_ref, mod_ref, nw_ref, w1_ref, w2_ref,
                 wr_ref, br_ref, sut_ref,
                 x1_ref, h2_ref, idx_ref, gate_ref, cnt_ref, cnt_s):
    i = pl.program_id(0)

    @pl.when(i == 0)
    def _():
        cnt_s[...] = jnp.zeros_like(cnt_s)

    y = _dot(ohg_ref[...], w1_ref[...]) + _dot(ofx_ref[...], w2_ref[...])
    x1 = _gated_residual(x_ref[...], y, mod_ref[0])
    x1_ref[...] = x1
    h2 = _modulate(_rms(x1) * nw_ref[...], mod_ref[2], mod_ref[1])
    h2_ref[...] = _pack_halves(h2)
    logits = _dot_nt(wr_ref[...], h2.astype(BF16)) + br_ref[...]
    e_f = lax.broadcasted_iota(jnp.int32, logits.shape, 0).astype(F32)
    work = logits
    vals, hots = [], []
    for _ in range(TOP_K):
        mx = jnp.max(work, axis=0, keepdims=True)
        ix = jnp.min(jnp.where(work == mx, e_f, float(N_EXPERTS)), axis=0, keepdims=True)
        hot = e_f == ix
        vals.append(mx)
        hots.append(hot)
        work = jnp.where(hot, -jnp.inf, work)
    exps = [jnp.exp(v - vals[0]) for v in vals]
    den = exps[0] + exps[1] + exps[2] + exps[3]
    hot_f = [h.astype(F32) for h in hots]
    hot_all = hot_f[0] + hot_f[1] + hot_f[2] + hot_f[3]
    before = cnt_s[:, 0:1] + _dot(hot_all.astype(BF16), sut_ref[...])
    for k in range(TOP_K):
        idx_ref[k:k + 1, :] = jnp.sum(hot_f[k] * e_f, axis=0, keepdims=True).astype(jnp.int32)
        idx_ref[TOP_K + k:TOP_K + k + 1, :] = jnp.sum(
            hot_f[k] * before, axis=0, keepdims=True).astype(jnp.int32)
        gate_ref[k:k + 1, :] = exps[k] / den
        gate_ref[TOP_K + k:TOP_K + k + 1, :] = jnp.zeros_like(den)
    total = cnt_s[...] + jnp.sum(hot_all, axis=1, keepdims=True)
    cnt_s[...] = total
    cnt_ref[...] = total


def _post_call(x, ohg, ofx, modg, nw, w1, w2, wr, br, sut):
    n, d = x.shape
    tm = ROW_TILE
    g = tm // MOD_GROUP
    const = lambda shape: pl.BlockSpec(shape, lambda i: (0,) * len(shape))
    row = lambda w: pl.BlockSpec((tm, w), lambda i: (i, 0))
    return pl.pallas_call(
        _post_kernel,
        grid=(n // tm,),
        in_specs=[
            row(d), row(HG_WIDTH), row(FOX_WIDTH),
            pl.BlockSpec((3, g, d), lambda i: (0, i, 0)),
            const((1, d)), const(w1.shape), const(w2.shape),
            const(wr.shape), const(br.shape), const(sut.shape),
        ],
        out_specs=[
            row(d), row(d // 2),
            pl.BlockSpec((2 * TOP_K, tm), lambda i: (0, i)),
            pl.BlockSpec((2 * TOP_K, tm), lambda i: (0, i)),
            const((N_EXPERTS, LANES)),
        ],
        out_shape=[
            jax.ShapeDtypeStruct((n, d), F32), jax.ShapeDtypeStruct((n, d // 2), U32),
            jax.ShapeDtypeStruct((2 * TOP_K, n), jnp.int32),
            jax.ShapeDtypeStruct((2 * TOP_K, n), F32),
            jax.ShapeDtypeStruct((N_EXPERTS, LANES), F32),
        ],
        scratch_shapes=[pltpu.VMEM((N_EXPERTS, LANES), F32)],
        compiler_params=_params("arbitrary"),
        name="post_mixer",
    )(x, ohg, ofx, modg, nw, w1, w2, wr, br, sut)


def _expert_kernel(blk_e_ref, n_used_ref, x_ref, wu_ref, bu_ref, wd_ref, bd_ref, perm_ref, o_ref, wu_s, wd_s):
    i = pl.program_id(0)
    used = i < n_used_ref[0]
    new_expert = jnp.logical_or(i == 0, blk_e_ref[i] != blk_e_ref[jnp.maximum(i - 1, 0)])

    @pl.when(jnp.logical_and(used, new_expert))
    def _():
        wd_s[...] = wd_ref[0, 0].astype(BF16)
        perm = perm_ref[...]
        f = wu_s.shape[1] // 2
        half = MXU_DIM // 2
        for t in range(wu_s.shape[1] // MXU_DIM):
            tile = wu_ref[0, 0, :, t * MXU_DIM:(t + 1) * MXU_DIM].astype(BF16)
            r = _dot(tile, perm).astype(BF16)
            wu_s[:, t * half:(t + 1) * half] = r[:, :half]
            wu_s[:, f + t * half:f + (t + 1) * half] = r[:, half:]

    @pl.when(used)
    def _():
        x = _unpack_halves(x_ref[...]).astype(BF16)
        u = _dot(x, wu_s[...]) + bu_ref[0]
        f = u.shape[1] // 2
        glu = jnp.minimum(u[:, :f], SWIGLU_LIMIT)
        lin = jnp.clip(u[:, f:], -SWIGLU_LIMIT, SWIGLU_LIMIT)
        a = glu * _sigmoid_pair(SWIGLU_ALPHA * glu)[0] * (lin + 1.0)
        o_ref[...] = _pack_halves(_dot(a.astype(BF16), wd_s[...]) + bd_ref[0, 0])

    @pl.when(jnp.logical_not(used))
    def _():
        o_ref[...] = jnp.zeros_like(o_ref)


def _expert_call(blk_e, n_used, xs, w_up, bu, w_down, b_down, *, layer):
    p, dh = xs.shape
    d = 2 * dh
    f2 = w_up.shape[3]
    tm = MOE_TILE
    j = jnp.arange(MXU_DIM, dtype=jnp.int32)
    half = MXU_DIM // 2
    src = jnp.where(j < half, 2 * j, 2 * (j - half) + 1)
    perm = (j[:, None] == src[None, :]).astype(BF16)
    return pl.pallas_call(
        _expert_kernel,
        grid_spec=pltpu.PrefetchScalarGridSpec(
            num_scalar_prefetch=2,
            grid=(p // tm,),
            in_specs=[
                pl.BlockSpec((tm, dh), lambda i, be, nu: (jnp.minimum(i, nu[0] - 1), 0)),
                pl.BlockSpec((1, 1, d, f2), lambda i, be, nu: (layer, be[i], 0, 0)),
                pl.BlockSpec((1, 1, f2), lambda i, be, nu: (be[i], 0, 0)),
                pl.BlockSpec((1, 1, f2 // 2, d), lambda i, be, nu: (layer, be[i], 0, 0)),
                pl.BlockSpec((1, 1, 1, d), lambda i, be, nu: (layer, be[i], 0, 0)),
                pl.BlockSpec((MXU_DIM, MXU_DIM), lambda i, be, nu: (0, 0)),
            ],
            out_specs=pl.BlockSpec((tm, dh), lambda i, be, nu: (i, 0)),
            scratch_shapes=[pltpu.VMEM((d, f2), BF16), pltpu.VMEM((f2 // 2, d), BF16)],
        ),
        out_shape=jax.ShapeDtypeStruct((p, dh), U32),
        compiler_params=_params("arbitrary"),
        name="experts",
    )(blk_e, n_used, xs, w_up, bu, w_down, b_down.reshape(b_down.shape[0], N_EXPERTS, 1, d), perm)


def _combine_kernel(x_ref, y_ref, gate_ref, mod_ref, nw_ref, o_ref, *, final):
    gates = gate_ref[...]
    y = gates[:, 0:1] * _unpack_halves(y_ref[0])
    for k in range(1, TOP_K):
        y = y + gates[:, k:k + 1] * _unpack_halves(y_ref[k])
    x2 = _gated_residual(x_ref[...], y, mod_ref[0])
    if final:
        x2 = _rms(x2) * nw_ref[...]
    o_ref[...] = x2


def _combine_call(x1, yg, gates, modg, nw, *, final):
    n, d = x1.shape
    tm = ROW_TILE
    g = tm // MOD_GROUP
    return pl.pallas_call(
        functools.partial(_combine_kernel, final=final),
        grid=(n // tm,),
        in_specs=[
            pl.BlockSpec((tm, d), lambda i: (i, 0)),
            pl.BlockSpec((TOP_K, tm, d // 2), lambda i: (0, i, 0)),
            pl.BlockSpec((tm, TOP_K), lambda i: (i, 0)),
            pl.BlockSpec((1, g, d), lambda i: (5, i, 0)),
            pl.BlockSpec((1, d), lambda i: (0, 0)),
        ],
        out_specs=pl.BlockSpec((tm, d), lambda i: (i, 0)),
        out_shape=jax.ShapeDtypeStruct((n, d), F32),
        compiler_params=_params("parallel"),
        name="combine",
    )(x1, yg, gates, modg, nw)


def _hgrn2_lower_bounds(lb_logits):
    p = jax.nn.softmax(lb_logits.astype(F32), axis=0)
    return jnp.cumsum(p, axis=0) - p[0:1]


def _cumsum_last(x):
    t = x.shape[-1]
    nb = t // LANES
    xb = x.reshape(-1, nb, LANES)
    i = jnp.arange(LANES)
    upper = (i[:, None] <= i[None, :]).astype(F32)
    within = jnp.einsum("rnk,kj->rnj", xb, upper, precision=lax.Precision.HIGHEST)
    j = jnp.arange(nb)
    strict = (j[:, None] < j[None, :]).astype(F32)
    offset = jnp.dot(within[..., -1], strict, precision=lax.Precision.HIGHEST)
    return (within + offset[..., None]).reshape(x.shape)


def _route(idx8, cnt, n_tok):
    a = n_tok * TOP_K
    n_blocks = -(-a // MOE_TILE) + N_EXPERTS
    counts = cnt[:, 0].astype(jnp.int32)
    padded = (counts + MOE_TILE - 1) // MOE_TILE * MOE_TILE
    pends = jnp.cumsum(padded)
    pstarts = pends - padded
    e_ids = jnp.arange(N_EXPERTS, dtype=jnp.int32)
    base = jnp.sum(jnp.where(idx8[:TOP_K, :, None] == e_ids, pstarts, 0), axis=-1)
    dest = base + idx8[TOP_K:]
    token = jnp.broadcast_to(jnp.arange(n_tok, dtype=jnp.int32)[None, :], (TOP_K, n_tok))
    tok_buf = jnp.zeros((n_blocks * MOE_TILE,), jnp.int32).at[dest.reshape(-1)].set(
        token.reshape(-1), unique_indices=True, mode="promise_in_bounds")
    blk_start = jnp.arange(n_blocks, dtype=jnp.int32) * MOE_TILE
    blk_e = jnp.minimum(jnp.sum(pends[None, :] <= blk_start[:, None], axis=1), N_EXPERTS - 1).astype(jnp.int32)
    n_used = (pends[-1:] // MOE_TILE).astype(jnp.int32)
    return dest, tok_buf, blk_e, n_used


def kernel(x_prompt, x_sample, c_prompt, c_sample, state_hgrn, cache_fox_k, cache_fox_v, cache_fox_lf,
           w_ada, b_ada, norm_mix, norm_ffn, w_in, lb_logits, hg_norm, fox_fb, w_out,
           w_router, b_router, w_up, b_up, w_down, b_down, norm_final):
    bp, tp, d = x_prompt.shape
    bs, ts, _ = x_sample.shape
    depth = w_in.shape[0]
    n_p, n_s = bp * tp, bs * ts
    n = n_p + n_s
    past = cache_fox_lf.shape[-1]
    assert tp % MOD_GROUP == 0 and ts == MOD_GROUP
    assert n_p % ROW_TILE == 0 and n_s % ROW_TILE == 0

    x = jnp.concatenate([x_prompt.reshape(n_p, d), x_sample.reshape(n_s, d)], axis=0)
    c = jnp.concatenate([c_prompt, c_sample], axis=0)
    bc = c.shape[0]
    c = jnp.pad(c, ((0, (-bc) % 8), (0, 0)))
    mod = _ada_call(c, w_ada, b_ada)
    mod_rows = jnp.concatenate([jnp.repeat(mod[:, :bp], tp // MOD_GROUP, axis=1), mod[:, bp:bp + bs]], axis=1)
    modg = mod_rows.reshape(depth, n // MOD_GROUP, 6, d).transpose(0, 2, 1, 3)

    lbs = _hgrn2_lower_bounds(lb_logits)
    t = jnp.arange(ROW_TILE, dtype=jnp.int32)
    sut = (t[:, None] < t[None, :]).astype(BF16)
    s0_prompt = jnp.zeros((bp, HG_HEADS, HG_DIM, HG_DIM), F32)
    f2 = w_up.shape[-1]
    blk = min(ATTN_BLOCK, tp)
    assert tp % blk == 0
    c_cache = _cumsum_last(cache_fox_lf.astype(F32))
    head_ind = (jnp.arange(FOX_WIDTH)[None, :] // FOX_HD == jnp.arange(FOX_HEADS)[:, None]).astype(BF16)

    kc_t = jnp.swapaxes(cache_fox_k, -1, -2)
    vc_t = jnp.swapaxes(cache_fox_v, -1, -2)
    k_p = jnp.zeros((depth, bp, FOX_HEADS, FOX_HD, tp), F32)
    v_p = jnp.zeros_like(k_p)
    k_s = jnp.zeros((depth, bs, FOX_HEADS, ts, FOX_HD), F32)
    v_s = jnp.zeros_like(k_s)
    new_s_p, new_lf_p, new_s_s, new_lf_s = [], [], [], []
    for l in range(depth):
        o4 = 4 * HG_WIDTH
        w_in_b = w_in[l].astype(BF16)
        whg = w_in_b[:, :o4]
        wfx = w_in_b[:, o4:o4 + 3 * FOX_WIDTH]
        wffT = w_in_b[:, o4 + 3 * FOX_WIDTH:].T
        fb = fox_fb[l].astype(F32).reshape(FOX_HEADS, 1)
        nw_mix = norm_mix[l].reshape(1, d)
        pre = functools.partial(_pre_call, x, modg[l], nw_mix, whg, wfx, wffT, fb, head_ind, layer=l)
        zhg_p, qb_p, kb_p, vb_p, k_p, v_p, lfT_p, sq_p = pre(k_p, v_p, row0=0, batch=bp, seq=tp)
        zhg_s, qb_s, kb_s, vb_s, k_s, v_s, lfT_s, _ = pre(k_s, v_s, row0=n_p, batch=bs, seq=ts)
        lf_p = lfT_p.reshape(FOX_HEADS, bp, tp).transpose(1, 0, 2)
        lf_s = lfT_s.reshape(FOX_HEADS, bs, ts).transpose(1, 0, 2)

        lb = lbs[l].reshape(1, HG_WIDTH)
        gn = hg_norm[l].astype(F32).reshape(1, HG_DIM)
        mix0 = jnp.zeros((n, HG_WIDTH), BF16)
        ohg, s_p = _hgrn_call(zhg_p, s0_prompt, lb, gn, mix0, row0=0, batch=bp, seq=tp)
        ohg, s_s = _hgrn_call(zhg_s, state_hgrn[l].astype(F32), lb, gn, ohg, row0=n_p, batch=bs, seq=ts)

        c_p = jnp.cumsum(lf_p, axis=-1)
        jstart = _attn_block_starts(sq_p, c_p, batch=bp, seq=tp, blk=blk)
        ofx = _attn_prompt_call(jstart, qb_p, kb_p, vb_p, c_p, jnp.zeros((n, FOX_WIDTH), BF16),
                                batch=bp, seq=tp, blk=blk)
        c_new = c_cache[l][..., -1:] + jnp.cumsum(lf_s, axis=-1)
        c_s = jnp.concatenate([c_cache[l], c_new], axis=-1)
        ofx = _attn_decode_call(qb_s, kb_s, vb_s, kc_t, vc_t, c_s, ofx, layer=l, row0=n_p, batch=bs, tt=ts)

        w_out_b = w_out[l].astype(BF16)
        x1, h2, idx8, gate8, cnt = _post_call(
            x, ohg, ofx, modg[l, 2:5], norm_ffn[l].reshape(1, d), w_out_b[:HG_WIDTH], w_out_b[HG_WIDTH:],
            w_router[l].T.astype(BF16), b_router[l].astype(F32).reshape(N_EXPERTS, 1), sut)

        dest, tok_buf, blk_e, n_used = _route(idx8, cnt, n)
        h2_big = jnp.concatenate([h2, jnp.zeros((tok_buf.shape[0] - n, d // 2), U32)], axis=0)
        xs = h2_big.at[tok_buf].get(mode="promise_in_bounds")
        bu = jnp.concatenate([b_up[l][:, 0::2], b_up[l][:, 1::2]], axis=-1).reshape(N_EXPERTS, 1, f2)
        yb = _expert_call(blk_e, n_used, xs, w_up, bu, w_down, b_down, layer=l)
        yg = yb.at[dest].get(mode="promise_in_bounds")
        x = _combine_call(x1, yg, gate8[:TOP_K].T, modg[l], norm_final.reshape(1, d),
                          final=(l == depth - 1))

        new_s_p.append(s_p); new_lf_p.append(lf_p)
        new_s_s.append(s_s); new_lf_s.append(lf_s)

    y_prompt = x[:n_p].reshape(bp, tp, d)
    y_sample = x[n_p:].reshape(bs, ts, d)
    st = jnp.stack
    tr = lambda a: jnp.swapaxes(a, -1, -2)
    return (y_prompt, y_sample, st(new_s_p), tr(k_p), tr(v_p), st(new_lf_p),
            st(new_s_s), k_s, v_s, st(new_lf_s))
```

```python
import functools

import jax
import jax.numpy as jnp
from jax import lax
from jax.experimental import pallas as pl
from jax.experimental.pallas import tpu as pltpu

F32 = jnp.float32
BF16 = jnp.bfloat16
U32 = jnp.uint32

HG_HEADS = 4
HG_DIM = 128
HG_WIDTH = HG_HEADS * HG_DIM
FOX_HEADS = 8
FOX_HD = 64
FOX_WIDTH = FOX_HEADS * FOX_HD
FOX_PAIRS = FOX_HEADS // 2
N_EXPERTS = 32
TOP_K = 4
GLA_BLOCK = 16
MOD_GROUP = 64
SWIGLU_ALPHA = 1.702
SWIGLU_LIMIT = 7.0
EPS = 1e-6
NEG_INF = -1e30
F_FLOOR = 1e-30
EXP_UNDERFLOW = 88.0
NORM_SLACK = 1.01

LANES = 128
MXU_DIM = 256
ROW_TILE = 512
HGRN_CHUNK = 256
ATTN_BLOCK = 256
ATTN_WINDOW = 3
MOE_TILE = 512
DECODE_HEADS = 4
VMEM_LIMIT = 56 * 1024 * 1024


def _params(*sem):
    return pltpu.CompilerParams(dimension_semantics=sem, vmem_limit_bytes=VMEM_LIMIT)


def _dot(a, b):
    return jnp.dot(a, b, preferred_element_type=F32)


def _dot_nt(a, b):
    return lax.dot_general(a, b, (((1,), (1,)), ((), ())), preferred_element_type=F32)


def _dot_tn(a, b):
    return lax.dot_general(a, b, (((0,), (0,)), ((), ())), preferred_element_type=F32)


def _sigmoid_pair(z):
    e = jnp.exp(-jnp.abs(z))
    r = 1.0 / (1.0 + e)
    er = e * r
    pos = z >= 0.0
    return jnp.where(pos, r, er), jnp.where(pos, er, r)


def _silu(z):
    return z * _sigmoid_pair(z)[0]


def _log_sigmoid(z):
    return jnp.minimum(z, 0.0) - jnp.log(1.0 + jnp.exp(-jnp.abs(z)))


def _rms(x):
    return x * lax.rsqrt(jnp.mean(x * x, axis=-1, keepdims=True) + EPS)


def _modulate(y, scale, shift):
    rows, d = y.shape
    g = rows // MOD_GROUP
    y3 = y.reshape(g, MOD_GROUP, d)
    return (y3 * (1.0 + scale[:, None, :]) + shift[:, None, :]).reshape(rows, d)


def _gated_residual(x, y, gate):
    rows, d = y.shape
    g = rows // MOD_GROUP
    return x + (y.reshape(g, MOD_GROUP, d) * gate[:, None, :]).reshape(rows, d)


def _pack_halves(y):
    w = y.shape[1] // 2
    hi = lax.bitcast_convert_type(y[:, :w].astype(BF16).astype(F32), U32)
    lo = lax.bitcast_convert_type(y[:, w:].astype(BF16).astype(F32), U32)
    return hi | (lo >> 16)


def _unpack_halves(u):
    hi = lax.bitcast_convert_type(u & jnp.uint32(0xFFFF0000), F32)
    lo = lax.bitcast_convert_type(u << 16, F32)
    return jnp.concatenate([hi, lo], axis=-1)


def _ada_kernel(c_ref, w_ref, b_ref, o_ref):
    s = _silu(c_ref[...])
    o_ref[0] = _dot(s.astype(BF16), w_ref[0].astype(BF16)) + b_ref[0]


def _ada_call(c, w_ada, b_ada):
    depth, d, d6 = w_ada.shape
    bc = c.shape[0]
    tn = d6 // 4
    return pl.pallas_call(
        _ada_kernel,
        grid=(depth, d6 // tn),
        in_specs=[
            pl.BlockSpec((bc, d), lambda l, j: (0, 0)),
            pl.BlockSpec((1, d, tn), lambda l, j: (l, 0, j)),
            pl.BlockSpec((1, 1, tn), lambda l, j: (l, 0, j)),
        ],
        out_specs=pl.BlockSpec((1, bc, tn), lambda l, j: (l, 0, j)),
        out_shape=jax.ShapeDtypeStruct((depth, bc, d6), F32),
        compiler_params=_params("parallel", "parallel"),
        name="ada_mod",
    )(c, w_ada, b_ada.reshape(depth, 1, d6))


def _pre_kernel(x_ref, mod_ref, nw_ref, whg_ref, wfx_ref, wff_ref, fb_ref, ind_ref, k_acc, v_acc,
                zhg_ref, q_ref, kb_ref, vb_ref, k_ref, v_ref, lf_ref, nrm_ref, *, nb, tt, time_minor):
    del k_acc, v_acc
    x = x_ref[...]
    h = _modulate(_rms(x) * nw_ref[...], mod_ref[1], mod_ref[0])
    hb = h.astype(BF16)
    zhg_ref[...] = _dot(hb, whg_ref[...])
    zfx = _dot(hb, wfx_ref[...])
    q = zfx[:, :FOX_WIDTH] * (FOX_HD ** -0.5)
    k = zfx[:, FOX_WIDTH:2 * FOX_WIDTH]
    v = zfx[:, 2 * FOX_WIDTH:]
    qb, kb = q.astype(BF16), k.astype(BF16)
    q_ref[...] = qb
    kb_ref[...] = kb
    vb_ref[...] = v.astype(BF16)
    ind = ind_ref[...]
    nrm_ref[:FOX_HEADS, :] = _dot_nt(ind, (qb * qb))
    nrm_ref[FOX_HEADS:, :] = _dot_nt(ind, (kb * kb))
    if time_minor:
        k, v = k.T, v.T
    for b in range(nb):
        for hd in range(FOX_HEADS):
            rows = slice(b * tt, (b + 1) * tt)
            cols = slice(hd * FOX_HD, (hd + 1) * FOX_HD)
            k_ref[0, b, hd] = k[cols, rows] if time_minor else k[rows, cols]
            v_ref[0, b, hd] = v[cols, rows] if time_minor else v[rows, cols]
    ff = _dot_nt(wff_ref[...], hb)
    lf_ref[...] = _log_sigmoid(ff + fb_ref[...])


def _pre_call(x, modg, nw, whg, wfx, wffT, fb, ind, k_acc, v_acc, *, layer, row0, batch, seq):
    d = x.shape[1]
    time_minor = k_acc.shape[-2:] == (FOX_HD, seq) and seq != FOX_HD
    n = batch * seq
    tm = ROW_TILE
    tt = min(seq, tm)
    nb = tm // tt
    assert n % tm == 0 and row0 % tm == 0 and tm % tt == 0 and seq % tt == 0
    i0 = row0 // tm
    g = tm // MOD_GROUP
    tpb = seq // tt

    hm_block = (1, nb, FOX_HEADS, FOX_HD, tt) if time_minor else (1, nb, FOX_HEADS, tt, FOX_HD)

    def head_major_idx(i):
        if nb > 1:
            return (layer, i, 0, 0, 0)
        return (layer, i // tpb, 0, 0, i % tpb) if time_minor else (layer, i // tpb, 0, i % tpb, 0)

    tok = lambda w, dt: jax.ShapeDtypeStruct((n, w), dt)
    hm = jax.ShapeDtypeStruct(k_acc.shape, F32)
    const = lambda shape: pl.BlockSpec(shape, lambda i: (0,) * len(shape))
    any_spec = pl.BlockSpec(memory_space=pl.ANY)
    return pl.pallas_call(
        functools.partial(_pre_kernel, nb=nb, tt=tt, time_minor=time_minor),
        grid=(n // tm,),
        in_specs=[
            pl.BlockSpec((tm, d), lambda i: (i + i0, 0)),
            pl.BlockSpec((2, g, d), lambda i: (0, i + i0, 0)),
            const((1, d)),
            const(whg.shape), const(wfx.shape), const(wffT.shape), const(fb.shape), const(ind.shape),
            any_spec, any_spec,
        ],
        out_specs=[
            pl.BlockSpec((tm, 4 * HG_WIDTH), lambda i: (i, 0)),
            pl.BlockSpec((tm, FOX_WIDTH), lambda i: (i, 0)),
            pl.BlockSpec((tm, FOX_WIDTH), lambda i: (i, 0)),
            pl.BlockSpec((tm, FOX_WIDTH), lambda i: (i, 0)),
            pl.BlockSpec(hm_block, head_major_idx),
            pl.BlockSpec(hm_block, head_major_idx),
            pl.BlockSpec((FOX_HEADS, tm), lambda i: (0, i)),
            pl.BlockSpec((2 * FOX_HEADS, tm), lambda i: (0, i)),
        ],
        out_shape=[tok(4 * HG_WIDTH, F32), tok(FOX_WIDTH, BF16), tok(FOX_WIDTH, BF16),
                   tok(FOX_WIDTH, BF16), hm, hm, jax.ShapeDtypeStruct((FOX_HEADS, n), F32),
                   jax.ShapeDtypeStruct((2 * FOX_HEADS, n), F32)],
        input_output_aliases={8: 4, 9: 5},
        compiler_params=_params("parallel"),
        name="pre_mixer",
    )(x, modg, nw, whg, wfx, wffT, fb, ind, k_acc, v_acc)


def _hgrn_kernel(z_ref, s0_ref, lb_ref, gn_ref, tri_ref, o_acc, o_ref, s_ref,
                 st_ref, q_s, k_s, b_s, oi_s, u_s, sb_s, qe_s, *, chunk):
    del o_acc
    t = pl.program_id(1)
    L = GLA_BLOCK
    W = HG_WIDTH

    @pl.when(t == 0)
    def _():
        for h in range(HG_HEADS):
            st_ref[h] = s0_ref[0, h].T

    lb = lb_ref[...]
    zf = z_ref[:, W:2 * W]
    sig, sig_neg = _sigmoid_pair(zf)
    f = lb + (1.0 - lb) * sig
    g = jnp.log(jnp.maximum(f, F_FLOOR))
    k_s[...] = (1.0 - lb) * sig_neg
    q_s[...] = _silu(z_ref[:, :W]) * (HG_DIM ** -0.5)
    g0 = g.astype(BF16)
    r1 = g - g0.astype(F32)
    g1 = r1.astype(BF16)
    g2 = (r1 - g1.astype(F32)).astype(BF16)
    tri = tri_ref[...]
    cs = tri.shape[0]
    for r in range(chunk // cs):
        rows = slice(r * cs, (r + 1) * cs)
        b_s[rows, :] = _dot(tri, g0[rows]) + _dot(tri, g1[rows]) + _dot(tri, g2[rows])

    H = L // 2
    row_id = lax.broadcasted_iota(jnp.int32, (L, L), 0)
    col_id = lax.broadcasted_iota(jnp.int32, (L, L), 1)
    col_hi = lax.broadcasted_iota(jnp.int32, (H, L), 1)
    gn = gn_ref[...]
    nblk = chunk // L
    block_rows = lambda i: pl.ds(pl.multiple_of(i * L, L), L)

    def intra(i, carry):
        rows = block_rows(i)
        for h in range(HG_HEADS):
            cols = slice(h * HG_DIM, (h + 1) * HG_DIM)
            q = q_s[rows, cols]
            kk = k_s[rows, cols]
            b = b_s[rows, cols]
            vb = z_ref[rows, 2 * W + h * HG_DIM:2 * W + (h + 1) * HG_DIM].astype(BF16)
            att = jnp.zeros((L, L), F32)
            for s in range(H):
                dec = jnp.exp(jnp.minimum(b - b[s:s + 1, :], 0.0))
                col = jnp.sum(q * kk[s:s + 1, :] * dec, axis=-1, keepdims=True)
                att = jnp.where(col_id == s, col, att)
            att_hi = att[H:]
            q_hi, b_hi = q[H:], b[H:]
            for s in range(H, L):
                dec = jnp.exp(jnp.minimum(b_hi - b[s:s + 1, :], 0.0))
                col = jnp.sum(q_hi * kk[s:s + 1, :] * dec, axis=-1, keepdims=True)
                att_hi = jnp.where(col_hi == s, col, att_hi)
            att = jnp.concatenate([att[:H], att_hi], axis=0)
            att = jnp.where(row_id >= col_id, att, 0.0)
            oi_s[rows, cols] = _dot(att.astype(BF16), vb)
            kd = kk * jnp.exp(b[L - 1:L, :] - b)
            u_s[i * HG_HEADS + h] = _dot_tn(vb, kd.astype(BF16))
        return carry

    def scan(i, carry):
        tail = pl.ds(pl.multiple_of(i * L + H, H), H)
        for h in range(HG_HEADS):
            cols = slice(h * HG_DIM, (h + 1) * HG_DIM)
            st = st_ref[h]
            sb_s[i * HG_HEADS + h] = st.astype(BF16)
            st_ref[h] = st * jnp.exp(b_s[tail, cols][H - 1:H, :]) + u_s[i * HG_HEADS + h]
        return carry

    def inter(i, carry):
        rows = block_rows(i)
        for h in range(HG_HEADS):
            cols = slice(h * HG_DIM, (h + 1) * HG_DIM)
            oi_s[rows, cols] += _dot_nt(qe_s[rows, cols], sb_s[i * HG_HEADS + h])
        return carry

    unroll = 2 if nblk % 2 == 0 else 1
    lax.fori_loop(0, nblk, intra, 0, unroll=unroll)
    lax.fori_loop(0, nblk, scan, 0, unroll=unroll)
    qe_s[...] = (q_s[...] * jnp.exp(b_s[...])).astype(BF16)
    lax.fori_loop(0, nblk, inter, 0, unroll=4 if nblk % 4 == 0 else unroll)
    for h in range(HG_HEADS):
        cols = slice(h * HG_DIM, (h + 1) * HG_DIM)
        hg = z_ref[:, 3 * W + h * HG_DIM:3 * W + (h + 1) * HG_DIM]
        o_ref[:, cols] = (_rms(oi_s[:, cols]) * gn * _silu(hg)).astype(BF16)

    @pl.when(t == pl.num_programs(1) - 1)
    def _():
        for h in range(HG_HEADS):
            s_ref[0, h] = st_ref[h].T


def _hgrn_call(zhg, s0, lb, gn, o_acc, *, row0, batch, seq):
    chunk = min(HGRN_CHUNK, seq)
    cs = min(LANES, chunk)
    assert seq % chunk == 0 and chunk % cs == 0 and cs % GLA_BLOCK == 0 and row0 % chunk == 0
    c0 = row0 // chunk
    r = jnp.arange(cs, dtype=jnp.int32)
    tri = ((r[:, None] // GLA_BLOCK == r[None, :] // GLA_BLOCK) & (r[None, :] <= r[:, None])).astype(BF16)
    nt = seq // chunk
    scr = lambda: pltpu.VMEM((chunk, HG_WIDTH), F32)
    per_chunk = chunk // GLA_BLOCK * HG_HEADS
    return pl.pallas_call(
        functools.partial(_hgrn_kernel, chunk=chunk),
        grid=(batch, nt),
        in_specs=[
            pl.BlockSpec((chunk, 4 * HG_WIDTH), lambda b, t: (b * nt + t, 0)),
            pl.BlockSpec((1, HG_HEADS, HG_DIM, HG_DIM), lambda b, t: (b, 0, 0, 0)),
            pl.BlockSpec((1, HG_WIDTH), lambda b, t: (0, 0)),
            pl.BlockSpec((1, HG_DIM), lambda b, t: (0, 0)),
            pl.BlockSpec((cs, cs), lambda b, t: (0, 0)),
            pl.BlockSpec(memory_space=pl.ANY),
        ],
        out_specs=[
            pl.BlockSpec((chunk, HG_WIDTH), lambda b, t: (c0 + b * nt + t, 0)),
            pl.BlockSpec((1, HG_HEADS, HG_DIM, HG_DIM), lambda b, t: (b, 0, 0, 0)),
        ],
        out_shape=[jax.ShapeDtypeStruct(o_acc.shape, BF16),
                   jax.ShapeDtypeStruct((batch, HG_HEADS, HG_DIM, HG_DIM), F32)],
        input_output_aliases={5: 0},
        scratch_shapes=[pltpu.VMEM((HG_HEADS, HG_DIM, HG_DIM), F32), scr(), scr(), scr(), scr(),
                        pltpu.VMEM((per_chunk, HG_DIM, HG_DIM), F32),
                        pltpu.VMEM((per_chunk, HG_DIM, HG_DIM), BF16),
                        pltpu.VMEM((chunk, HG_WIDTH), BF16)],
        compiler_params=_params("parallel", "arbitrary"),
        name="hgrn",
    )(zhg, s0, lb, gn, tri, o_acc)


def _attn_prompt_kernel(jstart_ref, q_ref, k_ref, v_ref, c_ref, cq_ref, o_acc, o_ref, s_scr, *, blk, nq):
    del o_acc
    b, p, i = pl.program_id(0), pl.program_id(1), pl.program_id(2)
    q2 = q_ref[...]
    lane = lax.broadcasted_iota(jnp.int32, (blk, 2 * FOX_HD), 1)
    row_id = lax.broadcasted_iota(jnp.int32, (blk, blk), 0)
    col_id = lax.broadcasted_iota(jnp.int32, (blk, blk), 1)
    key_rows = lambda j: pl.ds(pl.multiple_of(j * blk, blk), blk)
    first = (b * FOX_HEADS + p * 2) * nq + i
    j0 = jnp.minimum(jstart_ref[first], jstart_ref[first + nq])
    qh = [jnp.where((lane >= hh * FOX_HD) & (lane < (hh + 1) * FOX_HD), q2, jnp.zeros_like(q2)) for hh in range(2)]
    c_q = [cq_ref[0, 0, :, hh:hh + 1] for hh in range(2)]

    def fill(j, carry, causal=False):
        out = []
        for hh in range(2):
            m, l = carry[2 * hh], carry[2 * hh + 1]
            s = (_dot_nt(qh[hh], k_ref[key_rows(j), :]) + c_q[hh]) - c_ref[0, 0, hh:hh + 1, key_rows(j)]
            if causal:
                s = jnp.where(col_id <= row_id, s, NEG_INF)
            s_scr[hh, j] = s
            m_new = jnp.maximum(m, jnp.max(s, axis=-1, keepdims=True))
            l = l * jnp.exp(m - m_new) + jnp.sum(jnp.exp(s - m_new), axis=-1, keepdims=True)
            out += [m_new, l]
        return tuple(out)

    def store(outs):
        o_ref[...] = jnp.where(lane < FOX_HD, outs[0], outs[1]).astype(BF16)

    win = min(ATTN_WINDOW, nq)

    @pl.when(i - j0 >= win)
    def _():
        neg = jnp.full((blk, 1), NEG_INF, F32)
        zero = jnp.zeros((blk, 1), F32)
        stats = lax.fori_loop(j0, i, fill, (neg, zero, neg, zero))
        m0, l0, m1, l1 = fill(i, stats, causal=True)
        ms = (m0, m1)
        inv_l = (1.0 / l0, 1.0 / l1)

        def weigh(j, accs):
            vblk = v_ref[key_rows(j), :]
            return tuple(accs[hh] + _dot((jnp.exp(s_scr[hh, j] - ms[hh]) * inv_l[hh]).astype(BF16), vblk)
                         for hh in range(2))

        acc = jnp.zeros((blk, 2 * FOX_HD), F32)
        store(lax.fori_loop(j0, i + 1, weigh, (acc, acc)))

    @pl.when(i - j0 < win)
    def _():
        start = jnp.maximum(i - (win - 1), 0)
        keys = pl.ds(pl.multiple_of(start * blk, blk), win * blk)
        key_pos = start * blk + lax.broadcasted_iota(jnp.int32, (blk, win * blk), 1)
        visible = key_pos <= i * blk + lax.broadcasted_iota(jnp.int32, (blk, win * blk), 0)
        kwin, vwin = k_ref[keys, :], v_ref[keys, :]
        outs = []
        for hh in range(2):
            s = (_dot_nt(qh[hh], kwin) + c_q[hh]) - c_ref[0, 0, hh:hh + 1, keys]
            s = jnp.where(visible, s, NEG_INF)
            e = jnp.exp(s - jnp.max(s, axis=-1, keepdims=True))
            inv = 1.0 / jnp.sum(e, axis=-1, keepdims=True)
            outs.append(_dot((e * inv).astype(BF16), vwin))
        store(outs)


def _attn_prompt_call(jstart, qb, kb, vb, c, o_acc, *, batch, seq, blk):
    nq = seq // blk
    kv_spec = pl.BlockSpec((seq, 2 * FOX_HD), lambda b, p, i, js: (b, p))
    c4 = c.reshape(batch, FOX_PAIRS, 2, seq)
    c_rows = c4.transpose(0, 1, 3, 2)
    return pl.pallas_call(
        functools.partial(_attn_prompt_kernel, blk=blk, nq=nq),
        grid_spec=pltpu.PrefetchScalarGridSpec(
            num_scalar_prefetch=1,
            grid=(batch, FOX_PAIRS, nq),
            in_specs=[
                pl.BlockSpec((blk, 2 * FOX_HD), lambda b, p, i, js: (b * nq + i, p)),
                kv_spec, kv_spec,
                pl.BlockSpec((1, 1, 2, seq), lambda b, p, i, js: (b, p, 0, 0)),
                pl.BlockSpec((1, 1, blk, 2), lambda b, p, i, js: (b, p, i, 0)),
                pl.BlockSpec(memory_space=pl.ANY),
            ],
            out_specs=pl.BlockSpec((blk, 2 * FOX_HD), lambda b, p, i, js: (b * nq + i, p)),
            scratch_shapes=[pltpu.VMEM((2, nq, blk, blk), F32)],
        ),
        out_shape=jax.ShapeDtypeStruct(o_acc.shape, BF16),
        input_output_aliases={6: 0},
        compiler_params=_params("parallel", "parallel", "arbitrary"),
        name="attn_prompt",
    )(jstart, qb, kb, vb, c4, c_rows, o_acc)


def _attn_block_starts(sq_norms, c, *, batch, seq, blk):
    def norms(sq):
        return jnp.sqrt(sq).reshape(FOX_HEADS, batch, seq).transpose(1, 0, 2) * NORM_SLACK

    qn, kn = norms(sq_norms[:FOX_HEADS]), norms(sq_norms[FOX_HEADS:])
    nq = seq // blk
    blocks = lambda a: a.reshape(batch, FOX_HEADS, nq, blk)
    k_max = jnp.max(kn, axis=-1, keepdims=True)
    need = jnp.max(blocks(qn * (k_max + kn) + c), axis=-1) + EXP_UNDERFLOW
    c_min = jnp.min(blocks(c), axis=-1)
    skip = c_min[..., None, :] > need[..., :, None]
    lead = jnp.sum(jnp.cumprod(skip.astype(jnp.int32), axis=-1), axis=-1)
    start = jnp.minimum(lead, jnp.arange(nq, dtype=jnp.int32))
    return start.reshape(-1).astype(jnp.int32)


def _attn_decode_kernel(q_ref, kn_ref, vn_ref, kc_ref, vc_ref, c_ref, cq_ref, o_acc, o_ref, *, past, tt, heads):
    del o_acc
    row_id = lax.broadcasted_iota(jnp.int32, (tt, tt), 0)
    col_id = lax.broadcasted_iota(jnp.int32, (tt, tt), 1)
    outs = []
    for hh in range(heads):
        cols = slice(hh * FOX_HD, (hh + 1) * FOX_HD)
        q = q_ref[:, cols]
        c = c_ref[0, 0, hh:hh + 1, :]
        c_q = cq_ref[0, 0, :, hh:hh + 1]
        s_c = (_dot(q, kc_ref[0, 0, hh].astype(BF16)) + c_q) - c[:, :past]
        s_n = (_dot_nt(q, kn_ref[:, cols]) + c_q) - c[:, past:]
        s_n = jnp.where(col_id <= row_id, s_n, NEG_INF)
        m = jnp.maximum(jnp.max(s_c, axis=-1, keepdims=True), jnp.max(s_n, axis=-1, keepdims=True))
        e_c = jnp.exp(s_c - m)
        e_n = jnp.exp(s_n - m)
        inv_l = 1.0 / (jnp.sum(e_c, axis=-1, keepdims=True) + jnp.sum(e_n, axis=-1, keepdims=True))
        outs.append(_dot_nt((e_c * inv_l).astype(BF16), vc_ref[0, 0, hh].astype(BF16))
                    + _dot((e_n * inv_l).astype(BF16), vn_ref[:, cols]))
    o_ref[...] = jnp.concatenate(outs, axis=-1).astype(BF16)


def _attn_decode_call(qb, kb, vb, kc_t, vc_t, c, o_acc, *, layer, row0, batch, tt):
    past = kc_t.shape[4]
    heads = DECODE_HEADS
    groups = FOX_HEADS // heads
    assert row0 % tt == 0
    r0 = row0 // tt
    tok_spec = pl.BlockSpec((tt, heads * FOX_HD), lambda b, p: (b, p))
    cache_spec = pl.BlockSpec((1, 1, heads, FOX_HD, past), lambda b, p: (layer, b, p, 0, 0))
    c4 = c.reshape(batch, groups, heads, past + tt)
    c_rows = c4[..., past:].transpose(0, 1, 3, 2)
    return pl.pallas_call(
        functools.partial(_attn_decode_kernel, past=past, tt=tt, heads=heads),
        grid=(batch, groups),
        in_specs=[tok_spec, tok_spec, tok_spec, cache_spec, cache_spec,
                  pl.BlockSpec((1, 1, heads, past + tt), lambda b, p: (b, p, 0, 0)),
                  pl.BlockSpec((1, 1, tt, heads), lambda b, p: (b, p, 0, 0)),
                  pl.BlockSpec(memory_space=pl.ANY)],
        out_specs=pl.BlockSpec((tt, heads * FOX_HD), lambda b, p: (r0 + b, p)),
        out_shape=jax.ShapeDtypeStruct(o_acc.shape, BF16),
        input_output_aliases={7: 0},
        compiler_params=_params("parallel", "parallel"),
        name="attn_decode",
    )(qb, kb, vb, kc_t, vc_t, c4, c_rows, o_acc)


def _post_kernel(x_ref, ohg_ref, ofx_ref, mod_ref, nw_ref, w1_ref, w2_ref,
                 wr_ref, br_ref, sut_ref,
                 x1_ref, h2_ref, idx_ref, gate_ref, cnt_ref, cnt_s):
    i = pl.program_id(0)

    @pl.when(i == 0)
    def _():
        cnt_s[...] = jnp.zeros_like(cnt_s)

    y = _dot(ohg_ref[...], w1_ref[...]) + _dot(ofx_ref[...], w2_ref[...])
    x1 = _gated_residual(x_ref[...], y, mod_ref[0])
    x1_ref[...] = x1
    h2 = _modulate(_rms(x1) * nw_ref[...], mod_ref[2], mod_ref[1])
    h2_ref[...] = _pack_halves(h2)
    logits = _dot_nt(wr_ref[...], h2.astype(BF16)) + br_ref[...]
    e_f = lax.broadcasted_iota(jnp.int32, logits.shape, 0).astype(F32)
    work = logits
    vals, hots = [], []
    for _ in range(TOP_K):
        mx = jnp.max(work, axis=0, keepdims=True)
        ix = jnp.min(jnp.where(work == mx, e_f, float(N_EXPERTS)), axis=0, keepdims=True)
        hot = e_f == ix
        vals.append(mx)
        hots.append(hot)
        work = jnp.where(hot, -jnp.inf, work)
    exps = [jnp.exp(v - vals[0]) for v in vals]
    den = exps[0] + exps[1] + exps[2] + exps[3]
    hot_f = [h.astype(F32) for h in hots]
    hot_all = hot_f[0] + hot_f[1] + hot_f[2] + hot_f[3]
    before = cnt_s[:, 0:1] + _dot(hot_all.astype(BF16), sut_ref[...])
    for k in range(TOP_K):
        idx_ref[k:k + 1, :] = jnp.sum(hot_f[k] * e_f, axis=0, keepdims=True).astype(jnp.int32)
        idx_ref[TOP_K + k:TOP_K + k + 1, :] = jnp.sum(
            hot_f[k] * before, axis=0, keepdims=True).astype(jnp.int32)
        gate_ref[k:k + 1, :] = exps[k] / den
        gate_ref[TOP_K + k:TOP_K + k + 1, :] = jnp.zeros_like(den)
    total = cnt_s[...] + jnp.sum(hot_all, axis=1, keepdims=True)
    cnt_s[...] = total
    cnt_ref[...] = total


def _post_call(x, ohg, ofx, modg, nw, w1, w2, wr, br, sut):
    n, d = x.shape
    tm = ROW_TILE
    g = tm // MOD_GROUP
    const = lambda shape: pl.BlockSpec(shape, lambda i: (0,) * len(shape))
    row = lambda w: pl.BlockSpec((tm, w), lambda i: (i, 0))
    return pl.pallas_call(
        _post_kernel,
        grid=(n // tm,),
        in_specs=[
            row(d), row(HG_WIDTH), row(FOX_WIDTH),
            pl.BlockSpec((3, g, d), lambda i: (0, i, 0)),
            const((1, d)), const(w1.shape), const(w2.shape),
            const(wr.shape), const(br.shape), const(sut.shape),
        ],
        out_specs=[
            row(d), row(d // 2),
            pl.BlockSpec((2 * TOP_K, tm), lambda i: (0, i)),
            pl.BlockSpec((2 * TOP_K, tm), lambda i: (0, i)),
            const((N_EXPERTS, LANES)),
        ],
        out_shape=[
            jax.ShapeDtypeStruct((n, d), F32), jax.ShapeDtypeStruct((n, d // 2), U32),
            jax.ShapeDtypeStruct((2 * TOP_K, n), jnp.int32),
            jax.ShapeDtypeStruct((2 * TOP_K, n), F32),
            jax.ShapeDtypeStruct((N_EXPERTS, LANES), F32),
        ],
        scratch_shapes=[pltpu.VMEM((N_EXPERTS, LANES), F32)],
        compiler_params=_params("arbitrary"),
        name="post_mixer",
    )(x, ohg, ofx, modg, nw, w1, w2, wr, br, sut)


def _expert_kernel(blk_e_ref, n_used_ref, x_ref, wu_ref, bu_ref, wd_ref, bd_ref, perm_ref, o_ref, wu_s, wd_s):
    i = pl.program_id(0)
    used = i < n_used_ref[0]
    new_expert = jnp.logical_or(i == 0, blk_e_ref[i] != blk_e_ref[jnp.maximum(i - 1, 0)])

    @pl.when(jnp.logical_and(used, new_expert))
    def _():
        wd_s[...] = wd_ref[0, 0].astype(BF16)
        perm = perm_ref[...]
        f = wu_s.shape[1] // 2
        half = MXU_DIM // 2
        for t in range(wu_s.shape[1] // MXU_DIM):
            tile = wu_ref[0, 0, :, t * MXU_DIM:(t + 1) * MXU_DIM].astype(BF16)
            r = _dot(tile, perm).astype(BF16)
            wu_s[:, t * half:(t + 1) * half] = r[:, :half]
            wu_s[:, f + t * half:f + (t + 1) * half] = r[:, half:]

    @pl.when(used)
    def _():
        x = _unpack_halves(x_ref[...]).astype(BF16)
        f = wu_s.shape[1] // 2
        y = bd_ref[0, 0]
        for c in range(f // MXU_DIM):
            g_cols = slice(c * MXU_DIM, (c + 1) * MXU_DIM)
            l_cols = slice(f + c * MXU_DIM, f + (c + 1) * MXU_DIM)
            glu = jnp.minimum(_dot(x, wu_s[:, g_cols]) + bu_ref[0, :, g_cols], SWIGLU_LIMIT)
            lin = jnp.clip(_dot(x, wu_s[:, l_cols]) + bu_ref[0, :, l_cols], -SWIGLU_LIMIT, SWIGLU_LIMIT)
            a = glu * _sigmoid_pair(SWIGLU_ALPHA * glu)[0] * (lin + 1.0)
            y = y + _dot(a.astype(BF16), wd_s[g_cols, :])
        o_ref[...] = _pack_halves(y)

    @pl.when(jnp.logical_not(used))
    def _():
        o_ref[...] = jnp.zeros_like(o_ref)


def _expert_call(blk_e, n_used, xs, w_up, bu, w_down, b_down, *, layer):
    p, dh = xs.shape
    d = 2 * dh
    f2 = w_up.shape[3]
    tm = MOE_TILE
    j = jnp.arange(MXU_DIM, dtype=jnp.int32)
    half = MXU_DIM // 2
    src = jnp.where(j < half, 2 * j, 2 * (j - half) + 1)
    perm = (j[:, None] == src[None, :]).astype(BF16)
    return pl.pallas_call(
        _expert_kernel,
        grid_spec=pltpu.PrefetchScalarGridSpec(
            num_scalar_prefetch=2,
            grid=(p // tm,),
            in_specs=[
                pl.BlockSpec((tm, dh), lambda i, be, nu: (jnp.minimum(i, nu[0] - 1), 0)),
                pl.BlockSpec((1, 1, d, f2), lambda i, be, nu: (layer, be[i], 0, 0)),
                pl.BlockSpec((1, 1, f2), lambda i, be, nu: (be[i], 0, 0)),
                pl.BlockSpec((1, 1, f2 // 2, d), lambda i, be, nu: (layer, be[i], 0, 0)),
                pl.BlockSpec((1, 1, 1, d), lambda i, be, nu: (layer, be[i], 0, 0)),
                pl.BlockSpec((MXU_DIM, MXU_DIM), lambda i, be, nu: (0, 0)),
            ],
            out_specs=pl.BlockSpec((tm, dh), lambda i, be, nu: (i, 0)),
            scratch_shapes=[pltpu.VMEM((d, f2), BF16), pltpu.VMEM((f2 // 2, d), BF16)],
        ),
        out_shape=jax.ShapeDtypeStruct((p, dh), U32),
        compiler_params=_params("arbitrary"),
        name="experts",
    )(blk_e, n_used, xs, w_up, bu, w_down, b_down.reshape(b_down.shape[0], N_EXPERTS, 1, d), perm)


def _combine_kernel(x_ref, y_ref, gate_ref, mod_ref, nw_ref, o_ref, *, final):
    gates = gate_ref[...]
    y = gates[:, 0:1] * _unpack_halves(y_ref[0])
    for k in range(1, TOP_K):
        y = y + gates[:, k:k + 1] * _unpack_halves(y_ref[k])
    x2 = _gated_residual(x_ref[...], y, mod_ref[0])
    if final:
        x2 = _rms(x2) * nw_ref[...]
    o_ref[...] = x2


def _combine_call(x1, yg, gates, modg, nw, *, final):
    n, d = x1.shape
    tm = ROW_TILE
    g = tm // MOD_GROUP
    return pl.pallas_call(
        functools.partial(_combine_kernel, final=final),
        grid=(n // tm,),
        in_specs=[
            pl.BlockSpec((tm, d), lambda i: (i, 0)),
            pl.BlockSpec((TOP_K, tm, d // 2), lambda i: (0, i, 0)),
            pl.BlockSpec((tm, TOP_K), lambda i: (i, 0)),
            pl.BlockSpec((1, g, d), lambda i: (5, i, 0)),
            pl.BlockSpec((1, d), lambda i: (0, 0)),
        ],
        out_specs=pl.BlockSpec((tm, d), lambda i: (i, 0)),
        out_shape=jax.ShapeDtypeStruct((n, d), F32),
        compiler_params=_params("parallel"),
        name="combine",
    )(x1, yg, gates, modg, nw)


def _hgrn2_lower_bounds(lb_logits):
    p = jax.nn.softmax(lb_logits.astype(F32), axis=0)
    return jnp.cumsum(p, axis=0) - p[0:1]


def _cumsum_last(x):
    t = x.shape[-1]
    nb = t // LANES
    xb = x.reshape(-1, nb, LANES)
    i = jnp.arange(LANES)
    upper = (i[:, None] <= i[None, :]).astype(F32)
    within = jnp.einsum("rnk,kj->rnj", xb, upper, precision=lax.Precision.HIGHEST)
    j = jnp.arange(nb)
    strict = (j[:, None] < j[None, :]).astype(F32)
    offset = jnp.dot(within[..., -1], strict, precision=lax.Precision.HIGHEST)
    return (within + offset[..., None]).reshape(x.shape)


def _route(idx8, cnt, n_tok):
    a = n_tok * TOP_K
    n_blocks = -(-a // MOE_TILE) + N_EXPERTS
    counts = cnt[:, 0].astype(jnp.int32)
    padded = (counts + MOE_TILE - 1) // MOE_TILE * MOE_TILE
    pends = jnp.cumsum(padded)
    pstarts = pends - padded
    e_ids = jnp.arange(N_EXPERTS, dtype=jnp.int32)
    base = jnp.sum(jnp.where(idx8[:TOP_K, :, None] == e_ids, pstarts, 0), axis=-1)
    dest = base + idx8[TOP_K:]
    token = jnp.broadcast_to(jnp.arange(n_tok, dtype=jnp.int32)[None, :], (TOP_K, n_tok))
    filler = jnp.arange(n_blocks * MOE_TILE, dtype=jnp.int32) % n_tok
    tok_buf = filler.at[dest.reshape(-1)].set(token.reshape(-1), unique_indices=True, mode="promise_in_bounds")
    blk_start = jnp.arange(n_blocks, dtype=jnp.int32) * MOE_TILE
    blk_e = jnp.minimum(jnp.sum(pends[None, :] <= blk_start[:, None], axis=1), N_EXPERTS - 1).astype(jnp.int32)
    n_used = (pends[-1:] // MOE_TILE).astype(jnp.int32)
    return dest, tok_buf, blk_e, n_used


def kernel(x_prompt, x_sample, c_prompt, c_sample, state_hgrn, cache_fox_k, cache_fox_v, cache_fox_lf,
           w_ada, b_ada, norm_mix, norm_ffn, w_in, lb_logits, hg_norm, fox_fb, w_out,
           w_router, b_router, w_up, b_up, w_down, b_down, norm_final):
    bp, tp, d = x_prompt.shape
    bs, ts, _ = x_sample.shape
    depth = w_in.shape[0]
    n_p, n_s = bp * tp, bs * ts
    n = n_p + n_s
    past = cache_fox_lf.shape[-1]
    assert tp % MOD_GROUP == 0 and ts == MOD_GROUP
    assert n_p % ROW_TILE == 0 and n_s % ROW_TILE == 0

    x = jnp.concatenate([x_prompt.reshape(n_p, d), x_sample.reshape(n_s, d)], axis=0)
    c = jnp.concatenate([c_prompt, c_sample], axis=0)
    bc = c.shape[0]
    c = jnp.pad(c, ((0, (-bc) % 8), (0, 0)))
    mod = _ada_call(c, w_ada, b_ada)
    mod_rows = jnp.concatenate([jnp.repeat(mod[:, :bp], tp // MOD_GROUP, axis=1), mod[:, bp:bp + bs]], axis=1)
    modg = mod_rows.reshape(depth, n // MOD_GROUP, 6, d).transpose(0, 2, 1, 3)

    lbs = _hgrn2_lower_bounds(lb_logits)
    t = jnp.arange(ROW_TILE, dtype=jnp.int32)
    sut = (t[:, None] < t[None, :]).astype(BF16)
    s0_prompt = jnp.zeros((bp, HG_HEADS, HG_DIM, HG_DIM), F32)
    f2 = w_up.shape[-1]
    blk = min(ATTN_BLOCK, tp)
    assert tp % blk == 0
    c_cache = _cumsum_last(cache_fox_lf.astype(F32))
    head_ind = (jnp.arange(FOX_WIDTH)[None, :] // FOX_HD == jnp.arange(FOX_HEADS)[:, None]).astype(BF16)

    kc_t = jnp.swapaxes(cache_fox_k, -1, -2)
    vc_t = jnp.swapaxes(cache_fox_v, -1, -2)
    k_p = jnp.zeros((depth, bp, FOX_HEADS, FOX_HD, tp), F32)
    v_p = jnp.zeros_like(k_p)
    k_s = jnp.zeros((depth, bs, FOX_HEADS, ts, FOX_HD), F32)
    v_s = jnp.zeros_like(k_s)
    new_s_p, new_lf_p, new_s_s, new_lf_s = [], [], [], []
    for l in range(depth):
        o4 = 4 * HG_WIDTH
        w_in_b = w_in[l].astype(BF16)
        whg = w_in_b[:, :o4]
        wfx = w_in_b[:, o4:o4 + 3 * FOX_WIDTH]
        wffT = w_in_b[:, o4 + 3 * FOX_WIDTH:].T
        fb = fox_fb[l].astype(F32).reshape(FOX_HEADS, 1)
        nw_mix = norm_mix[l].reshape(1, d)
        pre = functools.partial(_pre_call, x, modg[l], nw_mix, whg, wfx, wffT, fb, head_ind, layer=l)
        zhg_p, qb_p, kb_p, vb_p, k_p, v_p, lfT_p, sq_p = pre(k_p, v_p, row0=0, batch=bp, seq=tp)
        zhg_s, qb_s, kb_s, vb_s, k_s, v_s, lfT_s, _ = pre(k_s, v_s, row0=n_p, batch=bs, seq=ts)
        lf_p = lfT_p.reshape(FOX_HEADS, bp, tp).transpose(1, 0, 2)
        lf_s = lfT_s.reshape(FOX_HEADS, bs, ts).transpose(1, 0, 2)

        lb = lbs[l].reshape(1, HG_WIDTH)
        gn = hg_norm[l].astype(F32).reshape(1, HG_DIM)
        mix0 = jnp.zeros((n, HG_WIDTH), BF16)
        ohg, s_p = _hgrn_call(zhg_p, s0_prompt, lb, gn, mix0, row0=0, batch=bp, seq=tp)
        ohg, s_s = _hgrn_call(zhg_s, state_hgrn[l].astype(F32), lb, gn, ohg, row0=n_p, batch=bs, seq=ts)

        c_p = jnp.cumsum(lf_p, axis=-1)
        jstart = _attn_block_starts(sq_p, c_p, batch=bp, seq=tp, blk=blk)
        ofx = _attn_prompt_call(jstart, qb_p, kb_p, vb_p, c_p, jnp.zeros((n, FOX_WIDTH), BF16),
                                batch=bp, seq=tp, blk=blk)
        c_new = c_cache[l][..., -1:] + jnp.cumsum(lf_s, axis=-1)
        c_s = jnp.concatenate([c_cache[l], c_new], axis=-1)
        ofx = _attn_decode_call(qb_s, kb_s, vb_s, kc_t, vc_t, c_s, ofx, layer=l, row0=n_p, batch=bs, tt=ts)

        w_out_b = w_out[l].astype(BF16)
        x1, h2, idx8, gate8, cnt = _post_call(
            x, ohg, ofx, modg[l, 2:5], norm_ffn[l].reshape(1, d), w_out_b[:HG_WIDTH], w_out_b[HG_WIDTH:],
            w_router[l].T.astype(BF16), b_router[l].astype(F32).reshape(N_EXPERTS, 1), sut)

        dest, tok_buf, blk_e, n_used = _route(idx8, cnt, n)
        h2_big = jnp.concatenate([h2, jnp.zeros((tok_buf.shape[0] - n, d // 2), U32)], axis=0)
        xs = h2_big.at[tok_buf].get(mode="promise_in_bounds")
        bu = jnp.concatenate([b_up[l][:, 0::2], b_up[l][:, 1::2]], axis=-1).reshape(N_EXPERTS, 1, f2)
        yb = _expert_call(blk_e, n_used, xs, w_up, bu, w_down, b_down, layer=l)
        yg = yb.at[dest].get(mode="promise_in_bounds")
        x = _combine_call(x1, yg, gate8[:TOP_K].T, modg[l], norm_final.reshape(1, d),
                          final=(l == depth - 1))

        new_s_p.append(s_p); new_lf_p.append(lf_p)
        new_s_s.append(s_s); new_lf_s.append(lf_s)

    y_prompt = x[:n_p].reshape(bp, tp, d)
    y_sample = x[n_p:].reshape(bs, ts, d)
    st = jnp.stack
    tr = lambda a: jnp.swapaxes(a, -1, -2)
    return (y_prompt, y_sample, st(new_s_p), tr(k_p), tr(v_p), st(new_lf_p),
            st(new_s_s), k_s, v_s, st(new_lf_s))
```

```python
import functools

import jax
import jax.numpy as jnp
from jax import lax
from jax.experimental import pallas as pl
from jax.experimental.pallas import tpu as pltpu

F32 = jnp.float32
BF16 = jnp.bfloat16
U32 = jnp.uint32

HG_HEADS = 4
HG_DIM = 128
HG_WIDTH = HG_HEADS * HG_DIM
FOX_HEADS = 8
FOX_HD = 64
FOX_WIDTH = FOX_HEADS * FOX_HD
FOX_PAIRS = FOX_HEADS // 2
N_EXPERTS = 32
TOP_K = 4
GLA_BLOCK = 16
MOD_GROUP = 64
SWIGLU_ALPHA = 1.702
SWIGLU_LIMIT = 7.0
EPS = 1e-6
NEG_INF = -1e30
F_FLOOR = 1e-30
EXP_UNDERFLOW = 88.0
NORM_SLACK = 1.01

LANES = 128
MXU_DIM = 256
ROW_TILE = 512
HGRN_CHUNK = 256
ATTN_BLOCK = 256
ATTN_WINDOW = 3
ATTN_PAIRS = 2
MOE_TILE = 512
DECODE_HEADS = 8
VMEM_LIMIT = 56 * 1024 * 1024


def _params(*sem):
    return pltpu.CompilerParams(dimension_semantics=sem, vmem_limit_bytes=VMEM_LIMIT)


def _dot(a, b):
    return jnp.dot(a, b, preferred_element_type=F32)


def _dot_nt(a, b):
    return lax.dot_general(a, b, (((1,), (1,)), ((), ())), preferred_element_type=F32)


def _dot_tn(a, b):
    return lax.dot_general(a, b, (((0,), (0,)), ((), ())), preferred_element_type=F32)


def _sigmoid_pair(z):
    e = jnp.exp(-jnp.abs(z))
    r = 1.0 / (1.0 + e)
    er = e * r
    pos = z >= 0.0
    return jnp.where(pos, r, er), jnp.where(pos, er, r)


def _silu(z):
    return z * _sigmoid_pair(z)[0]


def _log_sigmoid(z):
    return jnp.minimum(z, 0.0) - jnp.log(1.0 + jnp.exp(-jnp.abs(z)))


def _rms(x):
    return x * lax.rsqrt(jnp.mean(x * x, axis=-1, keepdims=True) + EPS)


def _modulate(y, scale, shift):
    rows, d = y.shape
    g = rows // MOD_GROUP
    y3 = y.reshape(g, MOD_GROUP, d)
    return (y3 * (1.0 + scale[:, None, :]) + shift[:, None, :]).reshape(rows, d)


def _gated_residual(x, y, gate):
    rows, d = y.shape
    g = rows // MOD_GROUP
    return x + (y.reshape(g, MOD_GROUP, d) * gate[:, None, :]).reshape(rows, d)


def _pack_halves(y):
    w = y.shape[1] // 2
    hi = lax.bitcast_convert_type(y[:, :w].astype(BF16).astype(F32), U32)
    lo = lax.bitcast_convert_type(y[:, w:].astype(BF16).astype(F32), U32)
    return hi | (lo >> 16)


def _unpack_halves(u):
    hi = lax.bitcast_convert_type(u & jnp.uint32(0xFFFF0000), F32)
    lo = lax.bitcast_convert_type(u << 16, F32)
    return jnp.concatenate([hi, lo], axis=-1)


def _ada_kernel(c_ref, w_ref, b_ref, o_ref):
    s = _silu(c_ref[...])
    o_ref[0] = _dot(s.astype(BF16), w_ref[0].astype(BF16)) + b_ref[0]


def _ada_call(c, w_ada, b_ada):
    depth, d, d6 = w_ada.shape
    bc = c.shape[0]
    tn = d6 // 4
    return pl.pallas_call(
        _ada_kernel,
        grid=(depth, d6 // tn),
        in_specs=[
            pl.BlockSpec((bc, d), lambda l, j: (0, 0)),
            pl.BlockSpec((1, d, tn), lambda l, j: (l, 0, j)),
            pl.BlockSpec((1, 1, tn), lambda l, j: (l, 0, j)),
        ],
        out_specs=pl.BlockSpec((1, bc, tn), lambda l, j: (l, 0, j)),
        out_shape=jax.ShapeDtypeStruct((depth, bc, d6), F32),
        compiler_params=_params("parallel", "parallel"),
        name="ada_mod",
    )(c, w_ada, b_ada.reshape(depth, 1, d6))


def _pre_kernel(x_ref, mod_ref, nw_ref, whg_ref, wfx_ref, wff_ref, fb_ref, ind_ref, k_acc, v_acc,
                zhg_ref, q_ref, kb_ref, vb_ref, k_ref, v_ref, lf_ref, nrm_ref, *, nb, tt, time_minor):
    del k_acc, v_acc
    x = x_ref[...]
    h = _modulate(_rms(x) * nw_ref[...], mod_ref[1], mod_ref[0])
    hb = h.astype(BF16)
    zhg_ref[...] = _dot(hb, whg_ref[...])
    zfx = _dot(hb, wfx_ref[...])
    q = zfx[:, :FOX_WIDTH] * (FOX_HD ** -0.5)
    k = zfx[:, FOX_WIDTH:2 * FOX_WIDTH]
    v = zfx[:, 2 * FOX_WIDTH:]
    qb, kb = q.astype(BF16), k.astype(BF16)
    q_ref[...] = qb
    kb_ref[...] = kb
    vb_ref[...] = v.astype(BF16)
    ind = ind_ref[...]
    nrm_ref[:FOX_HEADS, :] = _dot_nt(ind, (qb * qb))
    nrm_ref[FOX_HEADS:, :] = _dot_nt(ind, (kb * kb))
    if time_minor:
        k, v = k.T, v.T
    for b in range(nb):
        for hd in range(FOX_HEADS):
            rows = slice(b * tt, (b + 1) * tt)
            cols = slice(hd * FOX_HD, (hd + 1) * FOX_HD)
            k_ref[0, b, hd] = k[cols, rows] if time_minor else k[rows, cols]
            v_ref[0, b, hd] = v[cols, rows] if time_minor else v[rows, cols]
    ff = _dot_nt(wff_ref[...], hb)
    lf_ref[...] = _log_sigmoid(ff + fb_ref[...])


def _pre_call(x, modg, nw, whg, wfx, wffT, fb, ind, k_acc, v_acc, *, layer, row0, batch, seq):
    d = x.shape[1]
    time_minor = k_acc.shape[-2:] == (FOX_HD, seq) and seq != FOX_HD
    n = batch * seq
    tm = ROW_TILE
    tt = min(seq, tm)
    nb = tm // tt
    assert n % tm == 0 and row0 % tm == 0 and tm % tt == 0 and seq % tt == 0
    i0 = row0 // tm
    g = tm // MOD_GROUP
    tpb = seq // tt

    hm_block = (1, nb, FOX_HEADS, FOX_HD, tt) if time_minor else (1, nb, FOX_HEADS, tt, FOX_HD)

    def head_major_idx(i):
        if nb > 1:
            return (layer, i, 0, 0, 0)
        return (layer, i // tpb, 0, 0, i % tpb) if time_minor else (layer, i // tpb, 0, i % tpb, 0)

    tok = lambda w, dt: jax.ShapeDtypeStruct((n, w), dt)
    hm = jax.ShapeDtypeStruct(k_acc.shape, F32)
    const = lambda shape: pl.BlockSpec(shape, lambda i: (0,) * len(shape))
    any_spec = pl.BlockSpec(memory_space=pl.ANY)
    return pl.pallas_call(
        functools.partial(_pre_kernel, nb=nb, tt=tt, time_minor=time_minor),
        grid=(n // tm,),
        in_specs=[
            pl.BlockSpec((tm, d), lambda i: (i + i0, 0)),
            pl.BlockSpec((2, g, d), lambda i: (0, i + i0, 0)),
            const((1, d)),
            const(whg.shape), const(wfx.shape), const(wffT.shape), const(fb.shape), const(ind.shape),
            any_spec, any_spec,
        ],
        out_specs=[
            pl.BlockSpec((tm, 4 * HG_WIDTH), lambda i: (i, 0)),
            pl.BlockSpec((tm, FOX_WIDTH), lambda i: (i, 0)),
            pl.BlockSpec((tm, FOX_WIDTH), lambda i: (i, 0)),
            pl.BlockSpec((tm, FOX_WIDTH), lambda i: (i, 0)),
            pl.BlockSpec(hm_block, head_major_idx),
            pl.BlockSpec(hm_block, head_major_idx),
            pl.BlockSpec((FOX_HEADS, tm), lambda i: (0, i)),
            pl.BlockSpec((2 * FOX_HEADS, tm), lambda i: (0, i)),
        ],
        out_shape=[tok(4 * HG_WIDTH, F32), tok(FOX_WIDTH, BF16), tok(FOX_WIDTH, BF16),
                   tok(FOX_WIDTH, BF16), hm, hm, jax.ShapeDtypeStruct((FOX_HEADS, n), F32),
                   jax.ShapeDtypeStruct((2 * FOX_HEADS, n), F32)],
        input_output_aliases={8: 4, 9: 5},
        compiler_params=_params("parallel"),
        name="pre_mixer",
    )(x, modg, nw, whg, wfx, wffT, fb, ind, k_acc, v_acc)


def _hgrn_kernel(z_ref, s0_ref, lb_ref, gn_ref, tri_ref, o_acc, o_ref, s_ref,
                 st_ref, q_s, k_s, b_s, oi_s, u_s, sb_s, qe_s, *, chunk):
    del o_acc
    t = pl.program_id(1)
    L = GLA_BLOCK
    W = HG_WIDTH

    @pl.when(t == 0)
    def _():
        for h in range(HG_HEADS):
            st_ref[h] = s0_ref[0, h].T

    lb = lb_ref[...]
    zf = z_ref[:, W:2 * W]
    sig, sig_neg = _sigmoid_pair(zf)
    f = lb + (1.0 - lb) * sig
    g = jnp.log(jnp.maximum(f, F_FLOOR))
    k_s[...] = (1.0 - lb) * sig_neg
    q_s[...] = _silu(z_ref[:, :W]) * (HG_DIM ** -0.5)
    g0 = g.astype(BF16)
    r1 = g - g0.astype(F32)
    g1 = r1.astype(BF16)
    g2 = (r1 - g1.astype(F32)).astype(BF16)
    tri = tri_ref[...]
    cs = tri.shape[0]
    for r in range(chunk // cs):
        rows = slice(r * cs, (r + 1) * cs)
        b_s[rows, :] = _dot(tri, g0[rows]) + _dot(tri, g1[rows]) + _dot(tri, g2[rows])

    H = L // 2
    row_id = lax.broadcasted_iota(jnp.int32, (L, L), 0)
    col_id = lax.broadcasted_iota(jnp.int32, (L, L), 1)
    col_hi = lax.broadcasted_iota(jnp.int32, (H, L), 1)
    gn = gn_ref[...]
    nblk = chunk // L
    block_rows = lambda i: pl.ds(pl.multiple_of(i * L, L), L)

    def intra(i, carry):
        rows = block_rows(i)
        for h in range(HG_HEADS):
            cols = slice(h * HG_DIM, (h + 1) * HG_DIM)
            q = q_s[rows, cols]
            kk = k_s[rows, cols]
            b = b_s[rows, cols]
            vb = z_ref[rows, 2 * W + h * HG_DIM:2 * W + (h + 1) * HG_DIM].astype(BF16)
            att = jnp.zeros((L, L), F32)
            for s in range(H):
                dec = jnp.exp(jnp.minimum(b - b[s:s + 1, :], 0.0))
                col = jnp.sum(q * kk[s:s + 1, :] * dec, axis=-1, keepdims=True)
                att = jnp.where(col_id == s, col, att)
            att_hi = att[H:]
            q_hi, b_hi = q[H:], b[H:]
            for s in range(H, L):
                dec = jnp.exp(jnp.minimum(b_hi - b[s:s + 1, :], 0.0))
                col = jnp.sum(q_hi * kk[s:s + 1, :] * dec, axis=-1, keepdims=True)
                att_hi = jnp.where(col_hi == s, col, att_hi)
            att = jnp.concatenate([att[:H], att_hi], axis=0)
            att = jnp.where(row_id >= col_id, att, 0.0)
            oi_s[rows, cols] = _dot(att.astype(BF16), vb)
            kd = kk * jnp.exp(b[L - 1:L, :] - b)
            u_s[i * HG_HEADS + h] = _dot_tn(vb, kd.astype(BF16))
        return carry

    def scan(i, carry):
        tail = pl.ds(pl.multiple_of(i * L + H, H), H)
        for h in range(HG_HEADS):
            cols = slice(h * HG_DIM, (h + 1) * HG_DIM)
            st = st_ref[h]
            sb_s[i * HG_HEADS + h] = st.astype(BF16)
            st_ref[h] = st * jnp.exp(b_s[tail, cols][H - 1:H, :]) + u_s[i * HG_HEADS + h]
        return carry

    def inter(i, carry):
        rows = block_rows(i)
        for h in range(HG_HEADS):
            cols = slice(h * HG_DIM, (h + 1) * HG_DIM)
            oi_s[rows, cols] += _dot_nt(qe_s[rows, cols], sb_s[i * HG_HEADS + h])
        return carry

    unroll = 2 if nblk % 2 == 0 else 1
    lax.fori_loop(0, nblk, intra, 0, unroll=unroll)
    lax.fori_loop(0, nblk, scan, 0, unroll=unroll)
    qe_s[...] = (q_s[...] * jnp.exp(b_s[...])).astype(BF16)
    lax.fori_loop(0, nblk, inter, 0, unroll=4 if nblk % 4 == 0 else unroll)
    for h in range(HG_HEADS):
        cols = slice(h * HG_DIM, (h + 1) * HG_DIM)
        hg = z_ref[:, 3 * W + h * HG_DIM:3 * W + (h + 1) * HG_DIM]
        o_ref[:, cols] = (_rms(oi_s[:, cols]) * gn * _silu(hg)).astype(BF16)

    @pl.when(t == pl.num_programs(1) - 1)
    def _():
        for h in range(HG_HEADS):
            s_ref[0, h] = st_ref[h].T


def _hgrn_call(zhg, s0, lb, gn, o_acc, *, row0, batch, seq):
    chunk = min(HGRN_CHUNK, seq)
    cs = min(LANES, chunk)
    assert seq % chunk == 0 and chunk % cs == 0 and cs % GLA_BLOCK == 0 and row0 % chunk == 0
    c0 = row0 // chunk
    r = jnp.arange(cs, dtype=jnp.int32)
    tri = ((r[:, None] // GLA_BLOCK == r[None, :] // GLA_BLOCK) & (r[None, :] <= r[:, None])).astype(BF16)
    nt = seq // chunk
    scr = lambda: pltpu.VMEM((chunk, HG_WIDTH), F32)
    per_chunk = chunk // GLA_BLOCK * HG_HEADS
    return pl.pallas_call(
        functools.partial(_hgrn_kernel, chunk=chunk),
        grid=(batch, nt),
        in_specs=[
            pl.BlockSpec((chunk, 4 * HG_WIDTH), lambda b, t: (b * nt + t, 0)),
            pl.BlockSpec((1, HG_HEADS, HG_DIM, HG_DIM), lambda b, t: (b, 0, 0, 0)),
            pl.BlockSpec((1, HG_WIDTH), lambda b, t: (0, 0)),
            pl.BlockSpec((1, HG_DIM), lambda b, t: (0, 0)),
            pl.BlockSpec((cs, cs), lambda b, t: (0, 0)),
            pl.BlockSpec(memory_space=pl.ANY),
        ],
        out_specs=[
            pl.BlockSpec((chunk, HG_WIDTH), lambda b, t: (c0 + b * nt + t, 0)),
            pl.BlockSpec((1, HG_HEADS, HG_DIM, HG_DIM), lambda b, t: (b, 0, 0, 0)),
        ],
        out_shape=[jax.ShapeDtypeStruct(o_acc.shape, BF16),
                   jax.ShapeDtypeStruct((batch, HG_HEADS, HG_DIM, HG_DIM), F32)],
        input_output_aliases={5: 0},
        scratch_shapes=[pltpu.VMEM((HG_HEADS, HG_DIM, HG_DIM), F32), scr(), scr(), scr(), scr(),
                        pltpu.VMEM((per_chunk, HG_DIM, HG_DIM), F32),
                        pltpu.VMEM((per_chunk, HG_DIM, HG_DIM), BF16),
                        pltpu.VMEM((chunk, HG_WIDTH), BF16)],
        compiler_params=_params("parallel", "arbitrary"),
        name="hgrn",
    )(zhg, s0, lb, gn, tri, o_acc)


def _attn_prompt_kernel(jstart_ref, q_ref, k_ref, v_ref, c_ref, cq_ref, o_acc, o_ref, s_scr, *, blk, nq, pairs):
    del o_acc
    b, g, i = pl.program_id(0), pl.program_id(1), pl.program_id(2)
    width = 2 * FOX_HD
    lane = lax.broadcasted_iota(jnp.int32, (blk, width), 1)
    row_id = lax.broadcasted_iota(jnp.int32, (blk, blk), 0)
    col_id = lax.broadcasted_iota(jnp.int32, (blk, blk), 1)
    key_rows = lambda j: pl.ds(pl.multiple_of(j * blk, blk), blk)
    first = (b * FOX_HEADS + g * pairs * 2) * nq + i
    j0 = jstart_ref[first]
    for h in range(1, 2 * pairs):
        j0 = jnp.minimum(j0, jstart_ref[first + h * nq])
    lanes_of = lambda pr: slice(pr * width, (pr + 1) * width)
    qh, c_q = [], []
    for pr in range(pairs):
        q2 = q_ref[:, lanes_of(pr)]
        for hh in range(2):
            qh.append(jnp.where((lane >= hh * FOX_HD) & (lane < (hh + 1) * FOX_HD), q2, jnp.zeros_like(q2)))
            c_q.append(cq_ref[0, pr, :, hh:hh + 1])

    def store(pr, outs):
        o_ref[:, lanes_of(pr)] = jnp.where(lane < FOX_HD, outs[0], outs[1]).astype(BF16)

    win = min(ATTN_WINDOW, nq)

    @pl.when(i - j0 >= win)
    def _():
        for pr in range(pairs):
            def fill(j, carry, causal=False):
                out = []
                for hh in range(2):
                    m, l = carry[2 * hh], carry[2 * hh + 1]
                    s = ((_dot_nt(qh[2 * pr + hh], k_ref[key_rows(j), lanes_of(pr)]) + c_q[2 * pr + hh])
                         - c_ref[0, pr, hh:hh + 1, key_rows(j)])
                    if causal:
                        s = jnp.where(col_id <= row_id, s, NEG_INF)
                    s_scr[hh, j] = s
                    m_new = jnp.maximum(m, jnp.max(s, axis=-1, keepdims=True))
                    l = l * jnp.exp(m - m_new) + jnp.sum(jnp.exp(s - m_new), axis=-1, keepdims=True)
                    out += [m_new, l]
                return tuple(out)

            neg = jnp.full((blk, 1), NEG_INF, F32)
            zero = jnp.zeros((blk, 1), F32)
            stats = lax.fori_loop(j0, i, fill, (neg, zero, neg, zero))
            m0, l0, m1, l1 = fill(i, stats, causal=True)
            ms = (m0, m1)
            inv_l = (1.0 / l0, 1.0 / l1)

            def weigh(j, accs):
                vblk = v_ref[key_rows(j), lanes_of(pr)]
                return tuple(accs[hh] + _dot((jnp.exp(s_scr[hh, j] - ms[hh]) * inv_l[hh]).astype(BF16), vblk)
                             for hh in range(2))

            acc = jnp.zeros((blk, width), F32)
            store(pr, lax.fori_loop(j0, i + 1, weigh, (acc, acc)))

    @pl.when(i - j0 < win)
    def _():
        start = jnp.maximum(i - (win - 1), 0)
        keys = pl.ds(pl.multiple_of(start * blk, blk), win * blk)
        key_pos = start * blk + lax.broadcasted_iota(jnp.int32, (blk, win * blk), 1)
        visible = key_pos <= i * blk + lax.broadcasted_iota(jnp.int32, (blk, win * blk), 0)
        for pr in range(pairs):
            kwin, vwin = k_ref[keys, lanes_of(pr)], v_ref[keys, lanes_of(pr)]
            outs = []
            for hh in range(2):
                s = (_dot_nt(qh[2 * pr + hh], kwin) + c_q[2 * pr + hh]) - c_ref[0, pr, hh:hh + 1, keys]
                s = jnp.where(visible, s, NEG_INF)
                e = jnp.exp(s - jnp.max(s, axis=-1, keepdims=True))
                inv = 1.0 / jnp.sum(e, axis=-1, keepdims=True)
                outs.append(_dot((e * inv).astype(BF16), vwin))
            store(pr, outs)


def _attn_prompt_call(jstart, qb, kb, vb, c, o_acc, *, batch, seq, blk):
    nq = seq // blk
    pairs = ATTN_PAIRS
    groups = FOX_PAIRS // pairs
    width = pairs * 2 * FOX_HD
    kv_spec = pl.BlockSpec((seq, width), lambda b, g, i, js: (b, g))
    c4 = c.reshape(batch, FOX_PAIRS, 2, seq)
    c_rows = c4.transpose(0, 1, 3, 2)
    return pl.pallas_call(
        functools.partial(_attn_prompt_kernel, blk=blk, nq=nq, pairs=pairs),
        grid_spec=pltpu.PrefetchScalarGridSpec(
            num_scalar_prefetch=1,
            grid=(batch, groups, nq),
            in_specs=[
                pl.BlockSpec((blk, width), lambda b, g, i, js: (b * nq + i, g)),
                kv_spec, kv_spec,
                pl.BlockSpec((1, pairs, 2, seq), lambda b, g, i, js: (b, g, 0, 0)),
                pl.BlockSpec((1, pairs, blk, 2), lambda b, g, i, js: (b, g, i, 0)),
                pl.BlockSpec(memory_space=pl.ANY),
            ],
            out_specs=pl.BlockSpec((blk, width), lambda b, g, i, js: (b * nq + i, g)),
            scratch_shapes=[pltpu.VMEM((2, nq, blk, blk), F32)],
        ),
        out_shape=jax.ShapeDtypeStruct(o_acc.shape, BF16),
        input_output_aliases={6: 0},
        compiler_params=_params("parallel", "parallel", "arbitrary"),
        name="attn_prompt",
    )(jstart, qb, kb, vb, c4, c_rows, o_acc)


def _attn_block_starts(sq_norms, c, *, batch, seq, blk):
    def norms(sq):
        return jnp.sqrt(sq).reshape(FOX_HEADS, batch, seq).transpose(1, 0, 2) * NORM_SLACK

    qn, kn = norms(sq_norms[:FOX_HEADS]), norms(sq_norms[FOX_HEADS:])
    nq = seq // blk
    blocks = lambda a: a.reshape(batch, FOX_HEADS, nq, blk)
    k_max = jnp.max(kn, axis=-1, keepdims=True)
    need = jnp.max(blocks(qn * (k_max + kn) + c), axis=-1) + EXP_UNDERFLOW
    c_min = jnp.min(blocks(c), axis=-1)
    skip = c_min[..., None, :] > need[..., :, None]
    lead = jnp.sum(jnp.cumprod(skip.astype(jnp.int32), axis=-1), axis=-1)
    start = jnp.minimum(lead, jnp.arange(nq, dtype=jnp.int32))
    return start.reshape(-1).astype(jnp.int32)


def _attn_decode_kernel(q_ref, kn_ref, vn_ref, kc_ref, vc_ref, c_ref, cq_ref, o_acc, o_ref, *, past, tt, heads):
    del o_acc
    row_id = lax.broadcasted_iota(jnp.int32, (tt, tt), 0)
    col_id = lax.broadcasted_iota(jnp.int32, (tt, tt), 1)
    outs = []
    for hh in range(heads):
        cols = slice(hh * FOX_HD, (hh + 1) * FOX_HD)
        q = q_ref[:, cols]
        c = c_ref[0, 0, hh:hh + 1, :]
        c_q = cq_ref[0, 0, :, hh:hh + 1]
        s_c = (_dot(q, kc_ref[0, 0, hh].astype(BF16)) + c_q) - c[:, :past]
        s_n = (_dot_nt(q, kn_ref[:, cols]) + c_q) - c[:, past:]
        s_n = jnp.where(col_id <= row_id, s_n, NEG_INF)
        m = jnp.maximum(jnp.max(s_c, axis=-1, keepdims=True), jnp.max(s_n, axis=-1, keepdims=True))
        e_c = jnp.exp(s_c - m)
        e_n = jnp.exp(s_n - m)
        inv_l = 1.0 / (jnp.sum(e_c, axis=-1, keepdims=True) + jnp.sum(e_n, axis=-1, keepdims=True))
        outs.append(_dot_nt((e_c * inv_l).astype(BF16), vc_ref[0, 0, hh].astype(BF16))
                    + _dot((e_n * inv_l).astype(BF16), vn_ref[:, cols]))
    o_ref[...] = jnp.concatenate(outs, axis=-1).astype(BF16)


def _attn_decode_call(qb, kb, vb, kc_t, vc_t, c, o_acc, *, layer, row0, batch, tt):
    past = kc_t.shape[4]
    heads = DECODE_HEADS
    groups = FOX_HEADS // heads
    assert row0 % tt == 0
    r0 = row0 // tt
    tok_spec = pl.BlockSpec((tt, heads * FOX_HD), lambda b, p: (b, p))
    cache_spec = pl.BlockSpec((1, 1, heads, FOX_HD, past), lambda b, p: (layer, b, p, 0, 0))
    c4 = c.reshape(batch, groups, heads, past + tt)
    c_rows = c4[..., past:].transpose(0, 1, 3, 2)
    return pl.pallas_call(
        functools.partial(_attn_decode_kernel, past=past, tt=tt, heads=heads),
        grid=(batch, groups),
        in_specs=[tok_spec, tok_spec, tok_spec, cache_spec, cache_spec,
                  pl.BlockSpec((1, 1, heads, past + tt), lambda b, p: (b, p, 0, 0)),
                  pl.BlockSpec((1, 1, tt, heads), lambda b, p: (b, p, 0, 0)),
                  pl.BlockSpec(memory_space=pl.ANY)],
        out_specs=pl.BlockSpec((tt, heads * FOX_HD), lambda b, p: (r0 + b, p)),
        out_shape=jax.ShapeDtypeStruct(o_acc.shape, BF16),
        input_output_aliases={7: 0},
        compiler_params=_params("parallel", "parallel"),
        name="attn_decode",
    )(qb, kb, vb, kc_t, vc_t, c4, c_rows, o_acc)


def _post_kernel(x_ref, ohg_ref, ofx_ref, mod_ref, nw_ref, w1_ref, w2_ref,
                 wr_ref, br_ref, sut_ref,
                 x1_ref, h2_ref, idx_ref, gate_ref, cnt_ref, cnt_s):
    i = pl.program_id(0)

    @pl.when(i == 0)
    def _():
        cnt_s[...] = jnp.zeros_like(cnt_s)

    y = _dot(ohg_ref[...], w1_ref[...]) + _dot(ofx_ref[...], w2_ref[...])
    x1 = _gated_residual(x_ref[...], y, mod_ref[0])
    x1_ref[...] = x1
    h2 = _modulate(_rms(x1) * nw_ref[...], mod_ref[2], mod_ref[1])
    h2_ref[...] = _pack_halves(h2)
    logits = _dot_nt(wr_ref[...], h2.astype(BF16)) + br_ref[...]
    e_f = lax.broadcasted_iota(jnp.int32, logits.shape, 0).astype(F32)
    work = logits
    vals, hots = [], []
    for _ in range(TOP_K):
        mx = jnp.max(work, axis=0, keepdims=True)
        ix = jnp.min(jnp.where(work == mx, e_f, float(N_EXPERTS)), axis=0, keepdims=True)
        hot = e_f == ix
        vals.append(mx)
        hots.append(hot)
        work = jnp.where(hot, -jnp.inf, work)
    exps = [jnp.exp(v - vals[0]) for v in vals]
    den = exps[0] + exps[1] + exps[2] + exps[3]
    hot_f = [h.astype(F32) for h in hots]
    hot_all = hot_f[0] + hot_f[1] + hot_f[2] + hot_f[3]
    before = cnt_s[:, 0:1] + _dot(hot_all.astype(BF16), sut_ref[...])
    for k in range(TOP_K):
        idx_ref[k:k + 1, :] = jnp.sum(hot_f[k] * e_f, axis=0, keepdims=True).astype(jnp.int32)
        idx_ref[TOP_K + k:TOP_K + k + 1, :] = jnp.sum(
            hot_f[k] * before, axis=0, keepdims=True).astype(jnp.int32)
        gate_ref[k:k + 1, :] = exps[k] / den
        gate_ref[TOP_K + k:TOP_K + k + 1, :] = jnp.zeros_like(den)
    total = cnt_s[...] + jnp.sum(hot_all, axis=1, keepdims=True)
    cnt_s[...] = total
    cnt_ref[...] = total


def _post_call(x, ohg, ofx, modg, nw, w1, w2, wr, br, sut):
    n, d = x.shape
    tm = ROW_TILE
    g = tm // MOD_GROUP
    const = lambda shape: pl.BlockSpec(shape, lambda i: (0,) * len(shape))
    row = lambda w: pl.BlockSpec((tm, w), lambda i: (i, 0))
    return pl.pallas_call(
        _post_kernel,
        grid=(n // tm,),
        in_specs=[
            row(d), row(HG_WIDTH), row(FOX_WIDTH),
            pl.BlockSpec((3, g, d), lambda i: (0, i, 0)),
            const((1, d)), const(w1.shape), const(w2.shape),
            const(wr.shape), const(br.shape), const(sut.shape),
        ],
        out_specs=[
            row(d), row(d // 2),
            pl.BlockSpec((2 * TOP_K, tm), lambda i: (0, i)),
            pl.BlockSpec((2 * TOP_K, tm), lambda i: (0, i)),
            const((N_EXPERTS, LANES)),
        ],
        out_shape=[
            jax.ShapeDtypeStruct((n, d), F32), jax.ShapeDtypeStruct((n, d // 2), U32),
            jax.ShapeDtypeStruct((2 * TOP_K, n), jnp.int32),
            jax.ShapeDtypeStruct((2 * TOP_K, n), F32),
            jax.ShapeDtypeStruct((N_EXPERTS, LANES), F32),
        ],
        scratch_shapes=[pltpu.VMEM((N_EXPERTS, LANES), F32)],
        compiler_params=_params("arbitrary"),
        name="post_mixer",
    )(x, ohg, ofx, modg, nw, w1, w2, wr, br, sut)


def _expert_kernel(blk_e_ref, n_used_ref, x_ref, wu_ref, bu_ref, wd_ref, bd_ref, perm_ref, o_ref, wu_s, wd_s):
    i = pl.program_id(0)
    used = i < n_used_ref[0]
    new_expert = jnp.logical_or(i == 0, blk_e_ref[i] != blk_e_ref[jnp.maximum(i - 1, 0)])

    @pl.when(jnp.logical_and(used, new_expert))
    def _():
        wd_s[...] = wd_ref[0, 0].astype(BF16)
        perm = perm_ref[...]
        f = wu_s.shape[1] // 2
        half = MXU_DIM // 2
        for t in range(wu_s.shape[1] // MXU_DIM):
            tile = wu_ref[0, 0, :, t * MXU_DIM:(t + 1) * MXU_DIM].astype(BF16)
            r = _dot(tile, perm).astype(BF16)
            wu_s[:, t * half:(t + 1) * half] = r[:, :half]
            wu_s[:, f + t * half:f + (t + 1) * half] = r[:, half:]

    @pl.when(used)
    def _():
        x = _unpack_halves(x_ref[...]).astype(BF16)
        u = _dot(x, wu_s[...]) + bu_ref[0]
        f = u.shape[1] // 2
        glu = jnp.minimum(u[:, :f], SWIGLU_LIMIT)
        lin = jnp.clip(u[:, f:], -SWIGLU_LIMIT, SWIGLU_LIMIT)
        a = glu * _sigmoid_pair(SWIGLU_ALPHA * glu)[0] * (lin + 1.0)
        o_ref[...] = _pack_halves(_dot(a.astype(BF16), wd_s[...]) + bd_ref[0, 0])

    @pl.when(jnp.logical_not(used))
    def _():
        o_ref[...] = jnp.zeros_like(o_ref)


def _expert_call(blk_e, n_used, xs, w_up, bu, w_down, b_down, *, layer):
    p, dh = xs.shape
    d = 2 * dh
    f2 = w_up.shape[3]
    tm = MOE_TILE
    j = jnp.arange(MXU_DIM, dtype=jnp.int32)
    half = MXU_DIM // 2
    src = jnp.where(j < half, 2 * j, 2 * (j - half) + 1)
    perm = (j[:, None] == src[None, :]).astype(BF16)
    return pl.pallas_call(
        _expert_kernel,
        grid_spec=pltpu.PrefetchScalarGridSpec(
            num_scalar_prefetch=2,
            grid=(p // tm,),
            in_specs=[
                pl.BlockSpec((tm, dh), lambda i, be, nu: (jnp.minimum(i, nu[0] - 1), 0)),
                pl.BlockSpec((1, 1, d, f2), lambda i, be, nu: (layer, be[i], 0, 0)),
                pl.BlockSpec((1, 1, f2), lambda i, be, nu: (be[i], 0, 0)),
                pl.BlockSpec((1, 1, f2 // 2, d), lambda i, be, nu: (layer, be[i], 0, 0)),
                pl.BlockSpec((1, 1, 1, d), lambda i, be, nu: (layer, be[i], 0, 0)),
                pl.BlockSpec((MXU_DIM, MXU_DIM), lambda i, be, nu: (0, 0)),
            ],
            out_specs=pl.BlockSpec((tm, dh), lambda i, be, nu: (i, 0)),
            scratch_shapes=[pltpu.VMEM((d, f2), BF16), pltpu.VMEM((f2 // 2, d), BF16)],
        ),
        out_shape=jax.ShapeDtypeStruct((p, dh), U32),
        compiler_params=_params("arbitrary"),
        name="experts",
    )(blk_e, n_used, xs, w_up, bu, w_down, b_down.reshape(b_down.shape[0], N_EXPERTS, 1, d), perm)


def _combine_kernel(x_ref, y_ref, gate_ref, mod_ref, nw_ref, o_ref, *, final):
    gates = gate_ref[...]
    y = gates[:, 0:1] * _unpack_halves(y_ref[0])
    for k in range(1, TOP_K):
        y = y + gates[:, k:k + 1] * _unpack_halves(y_ref[k])
    x2 = _gated_residual(x_ref[...], y, mod_ref[0])
    if final:
        x2 = _rms(x2) * nw_ref[...]
    o_ref[...] = x2


def _combine_call(x1, yg, gates, modg, nw, *, final):
    n, d = x1.shape
    tm = ROW_TILE
    g = tm // MOD_GROUP
    return pl.pallas_call(
        functools.partial(_combine_kernel, final=final),
        grid=(n // tm,),
        in_specs=[
            pl.BlockSpec((tm, d), lambda i: (i, 0)),
            pl.BlockSpec((TOP_K, tm, d // 2), lambda i: (0, i, 0)),
            pl.BlockSpec((tm, TOP_K), lambda i: (i, 0)),
            pl.BlockSpec((1, g, d), lambda i: (5, i, 0)),
            pl.BlockSpec((1, d), lambda i: (0, 0)),
        ],
        out_specs=pl.BlockSpec((tm, d), lambda i: (i, 0)),
        out_shape=jax.ShapeDtypeStruct((n, d), F32),
        compiler_params=_params("parallel"),
        name="combine",
    )(x1, yg, gates, modg, nw)


def _hgrn2_lower_bounds(lb_logits):
    p = jax.nn.softmax(lb_logits.astype(F32), axis=0)
    return jnp.cumsum(p, axis=0) - p[0:1]


def _cumsum_last(x):
    t = x.shape[-1]
    nb = t // LANES
    xb = x.reshape(-1, nb, LANES)
    i = jnp.arange(LANES)
    upper = (i[:, None] <= i[None, :]).astype(F32)
    within = jnp.einsum("rnk,kj->rnj", xb, upper, precision=lax.Precision.HIGHEST)
    j = jnp.arange(nb)
    strict = (j[:, None] < j[None, :]).astype(F32)
    offset = jnp.dot(within[..., -1], strict, precision=lax.Precision.HIGHEST)
    return (within + offset[..., None]).reshape(x.shape)


def _route(idx8, cnt, n_tok):
    a = n_tok * TOP_K
    n_blocks = -(-a // MOE_TILE) + N_EXPERTS
    counts = cnt[:, 0].astype(jnp.int32)
    padded = (counts + MOE_TILE - 1) // MOE_TILE * MOE_TILE
    pends = jnp.cumsum(padded)
    pstarts = pends - padded
    e_ids = jnp.arange(N_EXPERTS, dtype=jnp.int32)
    base = jnp.sum(jnp.where(idx8[:TOP_K, :, None] == e_ids, pstarts, 0), axis=-1)
    dest = base + idx8[TOP_K:]
    token = jnp.broadcast_to(jnp.arange(n_tok, dtype=jnp.int32)[None, :], (TOP_K, n_tok))
    filler = jnp.arange(n_blocks * MOE_TILE, dtype=jnp.int32) % n_tok
    tok_buf = filler.at[dest.reshape(-1)].set(token.reshape(-1), unique_indices=True, mode="promise_in_bounds")
    blk_start = jnp.arange(n_blocks, dtype=jnp.int32) * MOE_TILE
    blk_e = jnp.minimum(jnp.sum(pends[None, :] <= blk_start[:, None], axis=1), N_EXPERTS - 1).astype(jnp.int32)
    n_used = (pends[-1:] // MOE_TILE).astype(jnp.int32)
    return dest, tok_buf, blk_e, n_used


def kernel(x_prompt, x_sample, c_prompt, c_sample, state_hgrn, cache_fox_k, cache_fox_v, cache_fox_lf,
           w_ada, b_ada, norm_mix, norm_ffn, w_in, lb_logits, hg_norm, fox_fb, w_out,
           w_router, b_router, w_up, b_up, w_down, b_down, norm_final):
    bp, tp, d = x_prompt.shape
    bs, ts, _ = x_sample.shape
    depth = w_in.shape[0]
    n_p, n_s = bp * tp, bs * ts
    n = n_p + n_s
    past = cache_fox_lf.shape[-1]
    assert tp % MOD_GROUP == 0 and ts == MOD_GROUP
    assert n_p % ROW_TILE == 0 and n_s % ROW_TILE == 0

    x = jnp.concatenate([x_prompt.reshape(n_p, d), x_sample.reshape(n_s, d)], axis=0)
    c = jnp.concatenate([c_prompt, c_sample], axis=0)
    bc = c.shape[0]
    c = jnp.pad(c, ((0, (-bc) % 8), (0, 0)))
    mod = _ada_call(c, w_ada, b_ada)
    mod_rows = jnp.concatenate([jnp.repeat(mod[:, :bp], tp // MOD_GROUP, axis=1), mod[:, bp:bp + bs]], axis=1)
    modg = mod_rows.reshape(depth, n // MOD_GROUP, 6, d).transpose(0, 2, 1, 3)

    lbs = _hgrn2_lower_bounds(lb_logits)
    t = jnp.arange(ROW_TILE, dtype=jnp.int32)
    sut = (t[:, None] < t[None, :]).astype(BF16)
    s0_prompt = jnp.zeros((bp, HG_HEADS, HG_DIM, HG_DIM), F32)
    f2 = w_up.shape[-1]
    blk = min(ATTN_BLOCK, tp)
    assert tp % blk == 0
    c_cache = _cumsum_last(cache_fox_lf.astype(F32))
    head_ind = (jnp.arange(FOX_WIDTH)[None, :] // FOX_HD == jnp.arange(FOX_HEADS)[:, None]).astype(BF16)

    kc_t = jnp.swapaxes(cache_fox_k, -1, -2)
    vc_t = jnp.swapaxes(cache_fox_v, -1, -2)
    k_p = jnp.zeros((depth, bp, FOX_HEADS, FOX_HD, tp), F32)
    v_p = jnp.zeros_like(k_p)
    k_s = jnp.zeros((depth, bs, FOX_HEADS, ts, FOX_HD), F32)
    v_s = jnp.zeros_like(k_s)
    new_s_p, new_lf_p, new_s_s, new_lf_s = [], [], [], []
    for l in range(depth):
        o4 = 4 * HG_WIDTH
        w_in_b = w_in[l].astype(BF16)
        whg = w_in_b[:, :o4]
        wfx = w_in_b[:, o4:o4 + 3 * FOX_WIDTH]
        wffT = w_in_b[:, o4 + 3 * FOX_WIDTH:].T
        fb = fox_fb[l].astype(F32).reshape(FOX_HEADS, 1)
        nw_mix = norm_mix[l].reshape(1, d)
        pre = functools.partial(_pre_call, x, modg[l], nw_mix, whg, wfx, wffT, fb, head_ind, layer=l)
        zhg_p, qb_p, kb_p, vb_p, k_p, v_p, lfT_p, sq_p = pre(k_p, v_p, row0=0, batch=bp, seq=tp)
        zhg_s, qb_s, kb_s, vb_s, k_s, v_s, lfT_s, _ = pre(k_s, v_s, row0=n_p, batch=bs, seq=ts)
        lf_p = lfT_p.reshape(FOX_HEADS, bp, tp).transpose(1, 0, 2)
        lf_s = lfT_s.reshape(FOX_HEADS, bs, ts).transpose(1, 0, 2)

        lb = lbs[l].reshape(1, HG_WIDTH)
        gn = hg_norm[l].astype(F32).reshape(1, HG_DIM)
        mix0 = jnp.zeros((n, HG_WIDTH), BF16)
        ohg, s_p = _hgrn_call(zhg_p, s0_prompt, lb, gn, mix0, row0=0, batch=bp, seq=tp)
        ohg, s_s = _hgrn_call(zhg_s, state_hgrn[l].astype(F32), lb, gn, ohg, row0=n_p, batch=bs, seq=ts)

        c_p = jnp.cumsum(lf_p, axis=-1)
        jstart = _attn_block_starts(sq_p, c_p, batch=bp, seq=tp, blk=blk)
        ofx = _attn_prompt_call(jstart, qb_p, kb_p, vb_p, c_p, jnp.zeros((n, FOX_WIDTH), BF16),
                                batch=bp, seq=tp, blk=blk)
        c_new = c_cache[l][..., -1:] + jnp.cumsum(lf_s, axis=-1)
        c_s = jnp.concatenate([c_cache[l], c_new], axis=-1)
        ofx = _attn_decode_call(qb_s, kb_s, vb_s, kc_t, vc_t, c_s, ofx, layer=l, row0=n_p, batch=bs, tt=ts)

        w_out_b = w_out[l].astype(BF16)
        x1, h2, idx8, gate8, cnt = _post_call(
            x, ohg, ofx, modg[l, 2:5], norm_ffn[l].reshape(1, d), w_out_b[:HG_WIDTH], w_out_b[HG_WIDTH:],
            w_router[l].T.astype(BF16), b_router[l].astype(F32).reshape(N_EXPERTS, 1), sut)

        dest, tok_buf, blk_e, n_used = _route(idx8, cnt, n)
        h2_big = jnp.concatenate([h2, jnp.zeros((tok_buf.shape[0] - n, d // 2), U32)], axis=0)
        xs = h2_big.at[tok_buf].get(mode="promise_in_bounds")
        bu = jnp.concatenate([b_up[l][:, 0::2], b_up[l][:, 1::2]], axis=-1).reshape(N_EXPERTS, 1, f2)
        yb = _expert_call(blk_e, n_used, xs, w_up, bu, w_down, b_down, layer=l)
        yg = yb.at[dest].get(mode="promise_in_bounds")
        x = _combine_call(x1, yg, gate8[:TOP_K].T, modg[l], norm_final.reshape(1, d),
                          final=(l == depth - 1))

        new_s_p.append(s_p); new_lf_p.append(lf_p)
        new_s_s.append(s_s); new_lf_s.append(lf_s)

    y_prompt = x[:n_p].reshape(bp, tp, d)
    y_sample = x[n_p:].reshape(bs, ts, d)
    st = jnp.stack
    tr = lambda a: jnp.swapaxes(a, -1, -2)
    return (y_prompt, y_sample, st(new_s_p), tr(k_p), tr(v_p), st(new_lf_p),
            st(new_s_s), k_s, v_s, st(new_lf_s))
```

```python
import functools

import jax
import jax.numpy as jnp
from jax import lax
from jax.experimental import pallas as pl
from jax.experimental.pallas import tpu as pltpu

F32 = jnp.float32
BF16 = jnp.bfloat16
U32 = jnp.uint32

HG_HEADS = 4
HG_DIM = 128
HG_WIDTH = HG_HEADS * HG_DIM
FOX_HEADS = 8
FOX_HD = 64
FOX_WIDTH = FOX_HEADS * FOX_HD
FOX_PAIRS = FOX_HEADS // 2
N_EXPERTS = 32
TOP_K = 4
GLA_BLOCK = 16
MOD_GROUP = 64
SWIGLU_ALPHA = 1.702
SWIGLU_LIMIT = 7.0
EPS = 1e-6
NEG_INF = -1e30
F_FLOOR = 1e-30
EXP_UNDERFLOW = 88.0
NORM_SLACK = 1.01

LANES = 128
MXU_DIM = 256
ROW_TILE = 512
HGRN_CHUNK = 256
ATTN_BLOCK = 256
ATTN_WINDOW = 3
ATTN_PAIRS = 2
MOE_TILE = 512
DECODE_HEADS = 8
VMEM_LIMIT = 56 * 1024 * 1024


def _params(*sem):
    return pltpu.CompilerParams(dimension_semantics=sem, vmem_limit_bytes=VMEM_LIMIT)


def _dot(a, b):
    return jnp.dot(a, b, preferred_element_type=F32)


def _dot_nt(a, b):
    return lax.dot_general(a, b, (((1,), (1,)), ((), ())), preferred_element_type=F32)


def _dot_tn(a, b):
    return lax.dot_general(a, b, (((0,), (0,)), ((), ())), preferred_element_type=F32)


def _sigmoid_pair(z):
    e = jnp.exp(-jnp.abs(z))
    r = 1.0 / (1.0 + e)
    er = e * r
    pos = z >= 0.0
    return jnp.where(pos, r, er), jnp.where(pos, er, r)


def _silu(z):
    return z * _sigmoid_pair(z)[0]


def _log_sigmoid(z):
    return jnp.minimum(z, 0.0) - jnp.log(1.0 + jnp.exp(-jnp.abs(z)))


def _rms(x):
    return x * lax.rsqrt(jnp.mean(x * x, axis=-1, keepdims=True) + EPS)


def _modulate(y, scale, shift):
    rows, d = y.shape
    g = rows // MOD_GROUP
    y3 = y.reshape(g, MOD_GROUP, d)
    return (y3 * (1.0 + scale[:, None, :]) + shift[:, None, :]).reshape(rows, d)


def _gated_residual(x, y, gate):
    rows, d = y.shape
    g = rows // MOD_GROUP
    return x + (y.reshape(g, MOD_GROUP, d) * gate[:, None, :]).reshape(rows, d)


def _pack_halves(y):
    w = y.shape[1] // 2
    hi = lax.bitcast_convert_type(y[:, :w].astype(BF16).astype(F32), U32)
    lo = lax.bitcast_convert_type(y[:, w:].astype(BF16).astype(F32), U32)
    return hi | (lo >> 16)


def _unpack_halves(u):
    hi = lax.bitcast_convert_type(u & jnp.uint32(0xFFFF0000), F32)
    lo = lax.bitcast_convert_type(u << 16, F32)
    return jnp.concatenate([hi, lo], axis=-1)


def _ada_kernel(c_ref, w_ref, b_ref, o_ref):
    s = _silu(c_ref[...])
    o_ref[0] = _dot(s.astype(BF16), w_ref[0].astype(BF16)) + b_ref[0]


def _ada_call(c, w_ada, b_ada):
    depth, d, d6 = w_ada.shape
    bc = c.shape[0]
    tn = d6 // 4
    return pl.pallas_call(
        _ada_kernel,
        grid=(depth, d6 // tn),
        in_specs=[
            pl.BlockSpec((bc, d), lambda l, j: (0, 0)),
            pl.BlockSpec((1, d, tn), lambda l, j: (l, 0, j)),
            pl.BlockSpec((1, 1, tn), lambda l, j: (l, 0, j)),
        ],
        out_specs=pl.BlockSpec((1, bc, tn), lambda l, j: (l, 0, j)),
        out_shape=jax.ShapeDtypeStruct((depth, bc, d6), F32),
        compiler_params=_params("parallel", "parallel"),
        name="ada_mod",
    )(c, w_ada, b_ada.reshape(depth, 1, d6))


def _pre_kernel(x_ref, mod_ref, nw_ref, whg_ref, wfx_ref, wff_ref, fb_ref, ind_ref, k_acc, v_acc,
                zhg_ref, q_ref, kb_ref, vb_ref, k_ref, v_ref, lf_ref, nrm_ref, *, nb, tt, time_minor):
    del k_acc, v_acc
    x = x_ref[...]
    h = _modulate(_rms(x) * nw_ref[...], mod_ref[1], mod_ref[0])
    hb = h.astype(BF16)
    zhg_ref[...] = _dot(hb, whg_ref[...])
    zfx = _dot(hb, wfx_ref[...])
    q = zfx[:, :FOX_WIDTH] * (FOX_HD ** -0.5)
    k = zfx[:, FOX_WIDTH:2 * FOX_WIDTH]
    v = zfx[:, 2 * FOX_WIDTH:]
    qb, kb = q.astype(BF16), k.astype(BF16)
    q_ref[...] = qb
    kb_ref[...] = kb
    vb_ref[...] = v.astype(BF16)
    ind = ind_ref[...]
    nrm_ref[:FOX_HEADS, :] = _dot_nt(ind, (qb * qb))
    nrm_ref[FOX_HEADS:, :] = _dot_nt(ind, (kb * kb))
    if time_minor:
        k, v = k.T, v.T
    for b in range(nb):
        for hd in range(FOX_HEADS):
            rows = slice(b * tt, (b + 1) * tt)
            cols = slice(hd * FOX_HD, (hd + 1) * FOX_HD)
            k_ref[0, b, hd] = k[cols, rows] if time_minor else k[rows, cols]
            v_ref[0, b, hd] = v[cols, rows] if time_minor else v[rows, cols]
    ff = _dot_nt(wff_ref[...], hb)
    lf_ref[...] = _log_sigmoid(ff + fb_ref[...])


def _pre_call(x, modg, nw, whg, wfx, wffT, fb, ind, k_acc, v_acc, *, layer, row0, batch, seq):
    d = x.shape[1]
    time_minor = k_acc.shape[-2:] == (FOX_HD, seq) and seq != FOX_HD
    n = batch * seq
    tm = ROW_TILE
    tt = min(seq, tm)
    nb = tm // tt
    assert n % tm == 0 and row0 % tm == 0 and tm % tt == 0 and seq % tt == 0
    i0 = row0 // tm
    g = tm // MOD_GROUP
    tpb = seq // tt

    hm_block = (1, nb, FOX_HEADS, FOX_HD, tt) if time_minor else (1, nb, FOX_HEADS, tt, FOX_HD)

    def head_major_idx(i):
        if nb > 1:
            return (layer, i, 0, 0, 0)
        return (layer, i // tpb, 0, 0, i % tpb) if time_minor else (layer, i // tpb, 0, i % tpb, 0)

    tok = lambda w, dt: jax.ShapeDtypeStruct((n, w), dt)
    hm = jax.ShapeDtypeStruct(k_acc.shape, F32)
    const = lambda shape: pl.BlockSpec(shape, lambda i: (0,) * len(shape))
    any_spec = pl.BlockSpec(memory_space=pl.ANY)
    return pl.pallas_call(
        functools.partial(_pre_kernel, nb=nb, tt=tt, time_minor=time_minor),
        grid=(n // tm,),
        in_specs=[
            pl.BlockSpec((tm, d), lambda i: (i + i0, 0)),
            pl.BlockSpec((2, g, d), lambda i: (0, i + i0, 0)),
            const((1, d)),
            const(whg.shape), const(wfx.shape), const(wffT.shape), const(fb.shape), const(ind.shape),
            any_spec, any_spec,
        ],
        out_specs=[
            pl.BlockSpec((tm, 4 * HG_WIDTH), lambda i: (i, 0)),
            pl.BlockSpec((tm, FOX_WIDTH), lambda i: (i, 0)),
            pl.BlockSpec((tm, FOX_WIDTH), lambda i: (i, 0)),
            pl.BlockSpec((tm, FOX_WIDTH), lambda i: (i, 0)),
            pl.BlockSpec(hm_block, head_major_idx),
            pl.BlockSpec(hm_block, head_major_idx),
            pl.BlockSpec((FOX_HEADS, tm), lambda i: (0, i)),
            pl.BlockSpec((2 * FOX_HEADS, tm), lambda i: (0, i)),
        ],
        out_shape=[tok(4 * HG_WIDTH, F32), tok(FOX_WIDTH, BF16), tok(FOX_WIDTH, BF16),
                   tok(FOX_WIDTH, BF16), hm, hm, jax.ShapeDtypeStruct((FOX_HEADS, n), F32),
                   jax.ShapeDtypeStruct((2 * FOX_HEADS, n), F32)],
        input_output_aliases={8: 4, 9: 5},
        compiler_params=_params("parallel"),
        name="pre_mixer",
    )(x, modg, nw, whg, wfx, wffT, fb, ind, k_acc, v_acc)


def _hgrn_kernel(z_ref, s0_ref, lb_ref, gn_ref, tri_ref, o_acc, o_ref, s_ref,
                 st_ref, q_s, k_s, b_s, oi_s, u_s, sb_s, qe_s, *, chunk):
    del o_acc
    t = pl.program_id(1)
    L = GLA_BLOCK
    W = HG_WIDTH

    @pl.when(t == 0)
    def _():
        for h in range(HG_HEADS):
            st_ref[h] = s0_ref[0, h].T

    lb = lb_ref[...]
    zf = z_ref[:, W:2 * W]
    sig, sig_neg = _sigmoid_pair(zf)
    f = lb + (1.0 - lb) * sig
    g = jnp.log(jnp.maximum(f, F_FLOOR))
    k_s[...] = (1.0 - lb) * sig_neg
    q_s[...] = _silu(z_ref[:, :W]) * (HG_DIM ** -0.5)
    g0 = g.astype(BF16)
    r1 = g - g0.astype(F32)
    g1 = r1.astype(BF16)
    g2 = (r1 - g1.astype(F32)).astype(BF16)
    tri = tri_ref[...]
    cs = tri.shape[0]
    for r in range(chunk // cs):
        rows = slice(r * cs, (r + 1) * cs)
        b_s[rows, :] = _dot(tri, g0[rows]) + _dot(tri, g1[rows]) + _dot(tri, g2[rows])

    H = L // 2
    row_id = lax.broadcasted_iota(jnp.int32, (L, L), 0)
    col_id = lax.broadcasted_iota(jnp.int32, (L, L), 1)
    col_hi = lax.broadcasted_iota(jnp.int32, (H, L), 1)
    gn = gn_ref[...]
    nblk = chunk // L
    block_rows = lambda i: pl.ds(pl.multiple_of(i * L, L), L)

    def intra(i, carry):
        rows = block_rows(i)
        for h in range(HG_HEADS):
            cols = slice(h * HG_DIM, (h + 1) * HG_DIM)
            q = q_s[rows, cols]
            kk = k_s[rows, cols]
            b = b_s[rows, cols]
            vb = z_ref[rows, 2 * W + h * HG_DIM:2 * W + (h + 1) * HG_DIM].astype(BF16)
            att = jnp.zeros((L, L), F32)
            for s in range(H):
                dec = jnp.exp(jnp.minimum(b - b[s:s + 1, :], 0.0))
                col = jnp.sum(q * kk[s:s + 1, :] * dec, axis=-1, keepdims=True)
                att = jnp.where(col_id == s, col, att)
            att_hi = att[H:]
            q_hi, b_hi = q[H:], b[H:]
            for s in range(H, L):
                dec = jnp.exp(jnp.minimum(b_hi - b[s:s + 1, :], 0.0))
                col = jnp.sum(q_hi * kk[s:s + 1, :] * dec, axis=-1, keepdims=True)
                att_hi = jnp.where(col_hi == s, col, att_hi)
            att = jnp.concatenate([att[:H], att_hi], axis=0)
            att = jnp.where(row_id >= col_id, att, 0.0)
            oi_s[rows, cols] = _dot(att.astype(BF16), vb)
            kd = kk * jnp.exp(b[L - 1:L, :] - b)
            u_s[i * HG_HEADS + h] = _dot_tn(vb, kd.astype(BF16))
        return carry

    def scan(i, carry):
        tail = pl.ds(pl.multiple_of(i * L + H, H), H)
        for h in range(HG_HEADS):
            cols = slice(h * HG_DIM, (h + 1) * HG_DIM)
            st = st_ref[h]
            sb_s[i * HG_HEADS + h] = st.astype(BF16)
            st_ref[h] = st * jnp.exp(b_s[tail, cols][H - 1:H, :]) + u_s[i * HG_HEADS + h]
        return carry

    def inter(i, carry):
        rows = block_rows(i)
        for h in range(HG_HEADS):
            cols = slice(h * HG_DIM, (h + 1) * HG_DIM)
            oi_s[rows, cols] += _dot_nt(qe_s[rows, cols], sb_s[i * HG_HEADS + h])
        return carry

    unroll = 2 if nblk % 2 == 0 else 1
    lax.fori_loop(0, nblk, intra, 0, unroll=8 if nblk % 8 == 0 else unroll)
    lax.fori_loop(0, nblk, scan, 0, unroll=unroll)
    qe_s[...] = (q_s[...] * jnp.exp(b_s[...])).astype(BF16)
    lax.fori_loop(0, nblk, inter, 0, unroll=8 if nblk % 8 == 0 else unroll)
    for h in range(HG_HEADS):
        cols = slice(h * HG_DIM, (h + 1) * HG_DIM)
        hg = z_ref[:, 3 * W + h * HG_DIM:3 * W + (h + 1) * HG_DIM]
        o_ref[:, cols] = (_rms(oi_s[:, cols]) * gn * _silu(hg)).astype(BF16)

    @pl.when(t == pl.num_programs(1) - 1)
    def _():
        for h in range(HG_HEADS):
            s_ref[0, h] = st_ref[h].T


def _hgrn_call(zhg, s0, lb, gn, o_acc, *, row0, batch, seq):
    chunk = min(HGRN_CHUNK, seq)
    cs = min(LANES, chunk)
    assert seq % chunk == 0 and chunk % cs == 0 and cs % GLA_BLOCK == 0 and row0 % chunk == 0
    c0 = row0 // chunk
    r = jnp.arange(cs, dtype=jnp.int32)
    tri = ((r[:, None] // GLA_BLOCK == r[None, :] // GLA_BLOCK) & (r[None, :] <= r[:, None])).astype(BF16)
    nt = seq // chunk
    scr = lambda: pltpu.VMEM((chunk, HG_WIDTH), F32)
    per_chunk = chunk // GLA_BLOCK * HG_HEADS
    return pl.pallas_call(
        functools.partial(_hgrn_kernel, chunk=chunk),
        grid=(batch, nt),
        in_specs=[
            pl.BlockSpec((chunk, 4 * HG_WIDTH), lambda b, t: (b * nt + t, 0)),
            pl.BlockSpec((1, HG_HEADS, HG_DIM, HG_DIM), lambda b, t: (b, 0, 0, 0)),
            pl.BlockSpec((1, HG_WIDTH), lambda b, t: (0, 0)),
            pl.BlockSpec((1, HG_DIM), lambda b, t: (0, 0)),
            pl.BlockSpec((cs, cs), lambda b, t: (0, 0)),
            pl.BlockSpec(memory_space=pl.ANY),
        ],
        out_specs=[
            pl.BlockSpec((chunk, HG_WIDTH), lambda b, t: (c0 + b * nt + t, 0)),
            pl.BlockSpec((1, HG_HEADS, HG_DIM, HG_DIM), lambda b, t: (b, 0, 0, 0)),
        ],
        out_shape=[jax.ShapeDtypeStruct(o_acc.shape, BF16),
                   jax.ShapeDtypeStruct((batch, HG_HEADS, HG_DIM, HG_DIM), F32)],
        input_output_aliases={5: 0},
        scratch_shapes=[pltpu.VMEM((HG_HEADS, HG_DIM, HG_DIM), F32), scr(), scr(), scr(), scr(),
                        pltpu.VMEM((per_chunk, HG_DIM, HG_DIM), F32),
                        pltpu.VMEM((per_chunk, HG_DIM, HG_DIM), BF16),
                        pltpu.VMEM((chunk, HG_WIDTH), BF16)],
        compiler_params=_params("parallel", "arbitrary"),
        name="hgrn",
    )(zhg, s0, lb, gn, tri, o_acc)


def _attn_prompt_kernel(jstart_ref, q_ref, k_ref, v_ref, c_ref, cq_ref, o_acc, o_ref, s_scr, *, blk, nq, pairs):
    del o_acc
    b, g, i = pl.program_id(0), pl.program_id(1), pl.program_id(2)
    width = 2 * FOX_HD
    lane = lax.broadcasted_iota(jnp.int32, (blk, width), 1)
    row_id = lax.broadcasted_iota(jnp.int32, (blk, blk), 0)
    col_id = lax.broadcasted_iota(jnp.int32, (blk, blk), 1)
    key_rows = lambda j: pl.ds(pl.multiple_of(j * blk, blk), blk)
    first = (b * FOX_HEADS + g * pairs * 2) * nq + i
    j0 = jstart_ref[first]
    for h in range(1, 2 * pairs):
        j0 = jnp.minimum(j0, jstart_ref[first + h * nq])
    lanes_of = lambda pr: slice(pr * width, (pr + 1) * width)
    qh, c_q = [], []
    for pr in range(pairs):
        q2 = q_ref[:, lanes_of(pr)]
        for hh in range(2):
            qh.append(jnp.where((lane >= hh * FOX_HD) & (lane < (hh + 1) * FOX_HD), q2, jnp.zeros_like(q2)))
            c_q.append(cq_ref[0, pr, :, hh:hh + 1])

    def store(pr, outs):
        o_ref[:, lanes_of(pr)] = jnp.where(lane < FOX_HD, outs[0], outs[1]).astype(BF16)

    win = min(ATTN_WINDOW, nq)

    @pl.when(i - j0 >= win)
    def _():
        for pr in range(pairs):
            def fill(j, carry, causal=False):
                out = []
                for hh in range(2):
                    m, l = carry[2 * hh], carry[2 * hh + 1]
                    s = ((_dot_nt(qh[2 * pr + hh], k_ref[key_rows(j), lanes_of(pr)]) + c_q[2 * pr + hh])
                         - c_ref[0, pr, hh:hh + 1, key_rows(j)])
                    if causal:
                        s = jnp.where(col_id <= row_id, s, NEG_INF)
                    s_scr[hh, j] = s
                    m_new = jnp.maximum(m, jnp.max(s, axis=-1, keepdims=True))
                    l = l * jnp.exp(m - m_new) + jnp.sum(jnp.exp(s - m_new), axis=-1, keepdims=True)
                    out += [m_new, l]
                return tuple(out)

            neg = jnp.full((blk, 1), NEG_INF, F32)
            zero = jnp.zeros((blk, 1), F32)
            stats = lax.fori_loop(j0, i, fill, (neg, zero, neg, zero))
            m0, l0, m1, l1 = fill(i, stats, causal=True)
            ms = (m0, m1)
            inv_l = (1.0 / l0, 1.0 / l1)

            def weigh(j, accs):
                vblk = v_ref[key_rows(j), lanes_of(pr)]
                return tuple(accs[hh] + _dot((jnp.exp(s_scr[hh, j] - ms[hh]) * inv_l[hh]).astype(BF16), vblk)
                             for hh in range(2))

            acc = jnp.zeros((blk, width), F32)
            store(pr, lax.fori_loop(j0, i + 1, weigh, (acc, acc)))

    @pl.when(i - j0 < win)
    def _():
        start = jnp.maximum(i - (win - 1), 0)
        keys = pl.ds(pl.multiple_of(start * blk, blk), win * blk)
        key_pos = start * blk + lax.broadcasted_iota(jnp.int32, (blk, win * blk), 1)
        visible = key_pos <= i * blk + lax.broadcasted_iota(jnp.int32, (blk, win * blk), 0)
        for pr in range(pairs):
            kwin, vwin = k_ref[keys, lanes_of(pr)], v_ref[keys, lanes_of(pr)]
            outs = []
            for hh in range(2):
                s = (_dot_nt(qh[2 * pr + hh], kwin) + c_q[2 * pr + hh]) - c_ref[0, pr, hh:hh + 1, keys]
                s = jnp.where(visible, s, NEG_INF)
                e = jnp.exp(s - jnp.max(s, axis=-1, keepdims=True))
                inv = 1.0 / jnp.sum(e, axis=-1, keepdims=True)
                outs.append(_dot((e * inv).astype(BF16), vwin))
            store(pr, outs)


def _attn_prompt_call(jstart, qb, kb, vb, c, o_acc, *, batch, seq, blk):
    nq = seq // blk
    pairs = ATTN_PAIRS
    groups = FOX_PAIRS // pairs
    width = pairs * 2 * FOX_HD
    kv_spec = pl.BlockSpec((seq, width), lambda b, g, i, js: (b, g))
    c4 = c.reshape(batch, FOX_PAIRS, 2, seq)
    c_rows = c4.transpose(0, 1, 3, 2)
    return pl.pallas_call(
        functools.partial(_attn_prompt_kernel, blk=blk, nq=nq, pairs=pairs),
        grid_spec=pltpu.PrefetchScalarGridSpec(
            num_scalar_prefetch=1,
            grid=(batch, groups, nq),
            in_specs=[
                pl.BlockSpec((blk, width), lambda b, g, i, js: (b * nq + i, g)),
                kv_spec, kv_spec,
                pl.BlockSpec((1, pairs, 2, seq), lambda b, g, i, js: (b, g, 0, 0)),
                pl.BlockSpec((1, pairs, blk, 2), lambda b, g, i, js: (b, g, i, 0)),
                pl.BlockSpec(memory_space=pl.ANY),
            ],
            out_specs=pl.BlockSpec((blk, width), lambda b, g, i, js: (b * nq + i, g)),
            scratch_shapes=[pltpu.VMEM((2, nq, blk, blk), F32)],
        ),
        out_shape=jax.ShapeDtypeStruct(o_acc.shape, BF16),
        input_output_aliases={6: 0},
        compiler_params=_params("parallel", "parallel", "arbitrary"),
        name="attn_prompt",
    )(jstart, qb, kb, vb, c4, c_rows, o_acc)


def _attn_block_starts(sq_norms, c, *, batch, seq, blk):
    def norms(sq):
        return jnp.sqrt(sq).reshape(FOX_HEADS, batch, seq).transpose(1, 0, 2) * NORM_SLACK

    qn, kn = norms(sq_norms[:FOX_HEADS]), norms(sq_norms[FOX_HEADS:])
    nq = seq // blk
    blocks = lambda a: a.reshape(batch, FOX_HEADS, nq, blk)
    k_max = jnp.max(kn, axis=-1, keepdims=True)
    need = jnp.max(blocks(qn * (k_max + kn) + c), axis=-1) + EXP_UNDERFLOW
    c_min = jnp.min(blocks(c), axis=-1)
    skip = c_min[..., None, :] > need[..., :, None]
    lead = jnp.sum(jnp.cumprod(skip.astype(jnp.int32), axis=-1), axis=-1)
    start = jnp.minimum(lead, jnp.arange(nq, dtype=jnp.int32))
    return start.reshape(-1).astype(jnp.int32)


def _attn_decode_kernel(q_ref, kn_ref, vn_ref, kc_ref, vc_ref, cc_ref, cn_ref, cq_ref, o_acc, o_ref, *, tt, heads):
    del o_acc
    row_id = lax.broadcasted_iota(jnp.int32, (tt, tt), 0)
    col_id = lax.broadcasted_iota(jnp.int32, (tt, tt), 1)
    outs = []
    for hh in range(heads):
        cols = slice(hh * FOX_HD, (hh + 1) * FOX_HD)
        q = q_ref[:, cols]
        c_q = cq_ref[0, :, hh:hh + 1]
        s_c = (_dot(q, kc_ref[0, 0, hh].astype(BF16)) + c_q) - cc_ref[0, 0, hh:hh + 1, :]
        s_n = (_dot_nt(q, kn_ref[:, cols]) + c_q) - cn_ref[0, hh:hh + 1, :]
        s_n = jnp.where(col_id <= row_id, s_n, NEG_INF)
        m = jnp.maximum(jnp.max(s_c, axis=-1, keepdims=True), jnp.max(s_n, axis=-1, keepdims=True))
        e_c = jnp.exp(s_c - m)
        e_n = jnp.exp(s_n - m)
        inv_l = 1.0 / (jnp.sum(e_c, axis=-1, keepdims=True) + jnp.sum(e_n, axis=-1, keepdims=True))
        outs.append(_dot_nt((e_c * inv_l).astype(BF16), vc_ref[0, 0, hh].astype(BF16))
                    + _dot((e_n * inv_l).astype(BF16), vn_ref[:, cols]))
    o_ref[...] = jnp.concatenate(outs, axis=-1).astype(BF16)


def _attn_decode_call(qb, kb, vb, kc_t, vc_t, c_cache, c_new, o_acc, *, layer, row0, batch, tt):
    past = kc_t.shape[4]
    heads = DECODE_HEADS
    groups = FOX_HEADS // heads
    assert row0 % tt == 0
    r0 = row0 // tt
    tok_spec = pl.BlockSpec((tt, heads * FOX_HD), lambda b, p: (b, p))
    cache_spec = pl.BlockSpec((1, 1, heads, FOX_HD, past), lambda b, p: (layer, b, p, 0, 0))
    c_rows = c_new.transpose(0, 2, 1)
    return pl.pallas_call(
        functools.partial(_attn_decode_kernel, tt=tt, heads=heads),
        grid=(batch, groups),
        in_specs=[tok_spec, tok_spec, tok_spec, cache_spec, cache_spec,
                  pl.BlockSpec((1, 1, heads, past), lambda b, p: (layer, b, p, 0)),
                  pl.BlockSpec((1, heads, tt), lambda b, p: (b, p, 0)),
                  pl.BlockSpec((1, tt, heads), lambda b, p: (b, 0, p)),
                  pl.BlockSpec(memory_space=pl.ANY)],
        out_specs=pl.BlockSpec((tt, heads * FOX_HD), lambda b, p: (r0 + b, p)),
        out_shape=jax.ShapeDtypeStruct(o_acc.shape, BF16),
        input_output_aliases={8: 0},
        compiler_params=_params("parallel", "parallel"),
        name="attn_decode",
    )(qb, kb, vb, kc_t, vc_t, c_cache, c_new, c_rows, o_acc)


def _post_kernel(x_ref, ohg_ref, ofx_ref, mod_ref, nw_ref, w1_ref, w2_ref,
                 wr_ref, br_ref, sut_ref,
                 x1_ref, h2_ref, idx_ref, gate_ref, cnt_ref, cnt_s):
    i = pl.program_id(0)

    @pl.when(i == 0)
    def _():
        cnt_s[...] = jnp.zeros_like(cnt_s)

    y = _dot(ohg_ref[...], w1_ref[...]) + _dot(ofx_ref[...], w2_ref[...])
    x1 = _gated_residual(x_ref[...], y, mod_ref[0])
    x1_ref[...] = x1
    h2 = _modulate(_rms(x1) * nw_ref[...], mod_ref[2], mod_ref[1])
    h2_ref[...] = _pack_halves(h2)
    logits = _dot_nt(wr_ref[...], h2.astype(BF16)) + br_ref[...]
    e_f = lax.broadcasted_iota(jnp.int32, logits.shape, 0).astype(F32)
    work = logits
    vals, hots = [], []
    for _ in range(TOP_K):
        mx = jnp.max(work, axis=0, keepdims=True)
        ix = jnp.min(jnp.where(work == mx, e_f, float(N_EXPERTS)), axis=0, keepdims=True)
        hot = e_f == ix
        vals.append(mx)
        hots.append(hot)
        work = jnp.where(hot, -jnp.inf, work)
    exps = [jnp.exp(v - vals[0]) for v in vals]
    den = exps[0] + exps[1] + exps[2] + exps[3]
    hot_f = [h.astype(F32) for h in hots]
    hot_all = hot_f[0] + hot_f[1] + hot_f[2] + hot_f[3]
    before = cnt_s[:, 0:1] + _dot(hot_all.astype(BF16), sut_ref[...])
    for k in range(TOP_K):
        idx_ref[k:k + 1, :] = jnp.sum(hot_f[k] * e_f, axis=0, keepdims=True).astype(jnp.int32)
        idx_ref[TOP_K + k:TOP_K + k + 1, :] = jnp.sum(
            hot_f[k] * before, axis=0, keepdims=True).astype(jnp.int32)
        gate_ref[k:k + 1, :] = exps[k] / den
        gate_ref[TOP_K + k:TOP_K + k + 1, :] = jnp.zeros_like(den)
    total = cnt_s[...] + jnp.sum(hot_all, axis=1, keepdims=True)
    cnt_s[...] = total
    cnt_ref[...] = total


def _post_call(x, ohg, ofx, modg, nw, w1, w2, wr, br, sut):
    n, d = x.shape
    tm = ROW_TILE
    g = tm // MOD_GROUP
    const = lambda shape: pl.BlockSpec(shape, lambda i: (0,) * len(shape))
    row = lambda w: pl.BlockSpec((tm, w), lambda i: (i, 0))
    return pl.pallas_call(
        _post_kernel,
        grid=(n // tm,),
        in_specs=[
            row(d), row(HG_WIDTH), row(FOX_WIDTH),
            pl.BlockSpec((3, g, d), lambda i: (0, i, 0)),
            const((1, d)), const(w1.shape), const(w2.shape),
            const(wr.shape), const(br.shape), const(sut.shape),
        ],
        out_specs=[
            row(d), row(d // 2),
            pl.BlockSpec((2 * TOP_K, tm), lambda i: (0, i)),
            pl.BlockSpec((2 * TOP_K, tm), lambda i: (0, i)),
            const((N_EXPERTS, LANES)),
        ],
        out_shape=[
            jax.ShapeDtypeStruct((n, d), F32), jax.ShapeDtypeStruct((n, d // 2), U32),
            jax.ShapeDtypeStruct((2 * TOP_K, n), jnp.int32),
            jax.ShapeDtypeStruct((2 * TOP_K, n), F32),
            jax.ShapeDtypeStruct((N_EXPERTS, LANES), F32),
        ],
        scratch_shapes=[pltpu.VMEM((N_EXPERTS, LANES), F32)],
        compiler_params=_params("arbitrary"),
        name="post_mixer",
    )(x, ohg, ofx, modg, nw, w1, w2, wr, br, sut)


def _expert_kernel(blk_e_ref, n_used_ref, x_ref, wu_ref, bu_ref, wd_ref, bd_ref, perm_ref, o_ref, wu_s, wd_s):
    i = pl.program_id(0)
    used = i < n_used_ref[0]
    new_expert = jnp.logical_or(i == 0, blk_e_ref[i] != blk_e_ref[jnp.maximum(i - 1, 0)])

    @pl.when(jnp.logical_and(used, new_expert))
    def _():
        wd_s[...] = wd_ref[0, 0].astype(BF16)
        perm = perm_ref[...]
        f = wu_s.shape[1] // 2
        half = MXU_DIM // 2
        for t in range(wu_s.shape[1] // MXU_DIM):
            tile = wu_ref[0, 0, :, t * MXU_DIM:(t + 1) * MXU_DIM].astype(BF16)
            r = _dot(tile, perm).astype(BF16)
            wu_s[:, t * half:(t + 1) * half] = r[:, :half]
            wu_s[:, f + t * half:f + (t + 1) * half] = r[:, half:]

    @pl.when(used)
    def _():
        x = _unpack_halves(x_ref[...]).astype(BF16)
        u = _dot(x, wu_s[...]) + bu_ref[0]
        f = u.shape[1] // 2
        glu = jnp.minimum(u[:, :f], SWIGLU_LIMIT)
        lin = jnp.clip(u[:, f:], -SWIGLU_LIMIT, SWIGLU_LIMIT)
        a = glu * _sigmoid_pair(SWIGLU_ALPHA * glu)[0] * (lin + 1.0)
        o_ref[...] = _pack_halves(_dot(a.astype(BF16), wd_s[...]) + bd_ref[0, 0])

    @pl.when(jnp.logical_not(used))
    def _():
        o_ref[...] = jnp.zeros_like(o_ref)


def _expert_call(blk_e, n_used, xs, w_up, bu, w_down, b_down, *, layer):
    p, dh = xs.shape
    d = 2 * dh
    f2 = w_up.shape[3]
    tm = MOE_TILE
    j = jnp.arange(MXU_DIM, dtype=jnp.int32)
    half = MXU_DIM // 2
    src = jnp.where(j < half, 2 * j, 2 * (j - half) + 1)
    perm = (j[:, None] == src[None, :]).astype(BF16)
    return pl.pallas_call(
        _expert_kernel,
        grid_spec=pltpu.PrefetchScalarGridSpec(
            num_scalar_prefetch=2,
            grid=(p // tm,),
            in_specs=[
                pl.BlockSpec((tm, dh), lambda i, be, nu: (jnp.minimum(i, nu[0] - 1), 0)),
                pl.BlockSpec((1, 1, d, f2), lambda i, be, nu: (layer, be[i], 0, 0)),
                pl.BlockSpec((1, 1, f2), lambda i, be, nu: (be[i], 0, 0)),
                pl.BlockSpec((1, 1, f2 // 2, d), lambda i, be, nu: (layer, be[i], 0, 0)),
                pl.BlockSpec((1, 1, 1, d), lambda i, be, nu: (layer, be[i], 0, 0)),
                pl.BlockSpec((MXU_DIM, MXU_DIM), lambda i, be, nu: (0, 0)),
            ],
            out_specs=pl.BlockSpec((tm, dh), lambda i, be, nu: (i, 0)),
            scratch_shapes=[pltpu.VMEM((d, f2), BF16), pltpu.VMEM((f2 // 2, d), BF16)],
        ),
        out_shape=jax.ShapeDtypeStruct((p, dh), U32),
        compiler_params=_params("arbitrary"),
        name="experts",
    )(blk_e, n_used, xs, w_up, bu, w_down, b_down.reshape(b_down.shape[0], N_EXPERTS, 1, d), perm)


def _combine_kernel(x_ref, y_ref, gate_ref, mod_ref, nw_ref, o_ref, *, final):
    gates = gate_ref[...]
    y = gates[:, 0:1] * _unpack_halves(y_ref[0])
    for k in range(1, TOP_K):
        y = y + gates[:, k:k + 1] * _unpack_halves(y_ref[k])
    x2 = _gated_residual(x_ref[...], y, mod_ref[0])
    if final:
        x2 = _rms(x2) * nw_ref[...]
    o_ref[...] = x2


def _combine_call(x1, yg, gates, modg, nw, *, final, row0=0, rows=None):
    n, d = x1.shape
    rows = n if rows is None else rows
    tm = ROW_TILE
    g = tm // MOD_GROUP
    assert row0 % tm == 0 and rows % tm == 0
    i0 = row0 // tm
    return pl.pallas_call(
        functools.partial(_combine_kernel, final=final),
        grid=(rows // tm,),
        in_specs=[
            pl.BlockSpec((tm, d), lambda i: (i + i0, 0)),
            pl.BlockSpec((TOP_K, tm, d // 2), lambda i: (0, i + i0, 0)),
            pl.BlockSpec((tm, TOP_K), lambda i: (i + i0, 0)),
            pl.BlockSpec((1, g, d), lambda i: (5, i + i0, 0)),
            pl.BlockSpec((1, d), lambda i: (0, 0)),
        ],
        out_specs=pl.BlockSpec((tm, d), lambda i: (i, 0)),
        out_shape=jax.ShapeDtypeStruct((rows, d), F32),
        compiler_params=_params("parallel"),
        name="combine",
    )(x1, yg, gates, modg, nw)


def _hgrn2_lower_bounds(lb_logits):
    p = jax.nn.softmax(lb_logits.astype(F32), axis=0)
    return jnp.cumsum(p, axis=0) - p[0:1]


def _cumsum_last(x):
    t = x.shape[-1]
    nb = t // LANES
    xb = x.reshape(-1, nb, LANES)
    i = jnp.arange(LANES)
    upper = (i[:, None] <= i[None, :]).astype(F32)
    within = jnp.einsum("rnk,kj->rnj", xb, upper, precision=lax.Precision.HIGHEST)
    j = jnp.arange(nb)
    strict = (j[:, None] < j[None, :]).astype(F32)
    offset = jnp.dot(within[..., -1], strict, precision=lax.Precision.HIGHEST)
    return (within + offset[..., None]).reshape(x.shape)


def _route(idx8, cnt, n_tok):
    a = n_tok * TOP_K
    n_blocks = -(-a // MOE_TILE) + N_EXPERTS
    counts = cnt[:, 0].astype(jnp.int32)
    padded = (counts + MOE_TILE - 1) // MOE_TILE * MOE_TILE
    pends = jnp.cumsum(padded)
    pstarts = pends - padded
    e_ids = jnp.arange(N_EXPERTS, dtype=jnp.int32)
    base = jnp.sum(jnp.where(idx8[:TOP_K, :, None] == e_ids, pstarts, 0), axis=-1)
    dest = base + idx8[TOP_K:]
    token = jnp.broadcast_to(jnp.arange(n_tok, dtype=jnp.int32)[None, :], (TOP_K, n_tok))
    filler = jnp.arange(n_blocks * MOE_TILE, dtype=jnp.int32) % n_tok
    tok_buf = filler.at[dest.reshape(-1)].set(token.reshape(-1), unique_indices=True, mode="promise_in_bounds")
    blk_start = jnp.arange(n_blocks, dtype=jnp.int32) * MOE_TILE
    blk_e = jnp.minimum(jnp.sum(pends[None, :] <= blk_start[:, None], axis=1), N_EXPERTS - 1).astype(jnp.int32)
    n_used = (pends[-1:] // MOE_TILE).astype(jnp.int32)
    return dest, tok_buf, blk_e, n_used


def kernel(x_prompt, x_sample, c_prompt, c_sample, state_hgrn, cache_fox_k, cache_fox_v, cache_fox_lf,
           w_ada, b_ada, norm_mix, norm_ffn, w_in, lb_logits, hg_norm, fox_fb, w_out,
           w_router, b_router, w_up, b_up, w_down, b_down, norm_final):
    bp, tp, d = x_prompt.shape
    bs, ts, _ = x_sample.shape
    depth = w_in.shape[0]
    n_p, n_s = bp * tp, bs * ts
    n = n_p + n_s
    past = cache_fox_lf.shape[-1]
    assert tp % MOD_GROUP == 0 and ts == MOD_GROUP
    assert n_p % ROW_TILE == 0 and n_s % ROW_TILE == 0

    x = jnp.concatenate([x_prompt.reshape(n_p, d), x_sample.reshape(n_s, d)], axis=0)
    c = jnp.concatenate([c_prompt, c_sample], axis=0)
    bc = c.shape[0]
    c = jnp.pad(c, ((0, (-bc) % 8), (0, 0)))
    mod = _ada_call(c, w_ada, b_ada)
    mod_rows = jnp.concatenate([jnp.repeat(mod[:, :bp], tp // MOD_GROUP, axis=1), mod[:, bp:bp + bs]], axis=1)
    modg = mod_rows.reshape(depth, n // MOD_GROUP, 6, d).transpose(0, 2, 1, 3)

    lbs = _hgrn2_lower_bounds(lb_logits)
    t = jnp.arange(ROW_TILE, dtype=jnp.int32)
    sut = (t[:, None] < t[None, :]).astype(BF16)
    s0_prompt = jnp.zeros((bp, HG_HEADS, HG_DIM, HG_DIM), F32)
    f2 = w_up.shape[-1]
    blk = min(ATTN_BLOCK, tp)
    assert tp % blk == 0
    c_cache = _cumsum_last(cache_fox_lf.astype(F32))
    head_ind = (jnp.arange(FOX_WIDTH)[None, :] // FOX_HD == jnp.arange(FOX_HEADS)[:, None]).astype(BF16)

    kc_t = jnp.swapaxes(cache_fox_k, -1, -2)
    vc_t = jnp.swapaxes(cache_fox_v, -1, -2)
    k_p = jnp.zeros((depth, bp, FOX_HEADS, FOX_HD, tp), F32)
    v_p = jnp.zeros_like(k_p)
    k_s = jnp.zeros((depth, bs, FOX_HEADS, ts, FOX_HD), F32)
    v_s = jnp.zeros_like(k_s)
    new_s_p, new_lf_p, new_s_s, new_lf_s = [], [], [], []
    for l in range(depth):
        o4 = 4 * HG_WIDTH
        w_in_b = w_in[l].astype(BF16)
        whg = w_in_b[:, :o4]
        wfx = w_in_b[:, o4:o4 + 3 * FOX_WIDTH]
        wffT = w_in_b[:, o4 + 3 * FOX_WIDTH:].T
        fb = fox_fb[l].astype(F32).reshape(FOX_HEADS, 1)
        nw_mix = norm_mix[l].reshape(1, d)
        pre = functools.partial(_pre_call, x, modg[l], nw_mix, whg, wfx, wffT, fb, head_ind, layer=l)
        zhg_p, qb_p, kb_p, vb_p, k_p, v_p, lfT_p, sq_p = pre(k_p, v_p, row0=0, batch=bp, seq=tp)
        zhg_s, qb_s, kb_s, vb_s, k_s, v_s, lfT_s, _ = pre(k_s, v_s, row0=n_p, batch=bs, seq=ts)
        lf_p = lfT_p.reshape(FOX_HEADS, bp, tp).transpose(1, 0, 2)
        lf_s = lfT_s.reshape(FOX_HEADS, bs, ts).transpose(1, 0, 2)

        lb = lbs[l].reshape(1, HG_WIDTH)
        gn = hg_norm[l].astype(F32).reshape(1, HG_DIM)
        mix0 = jnp.zeros((n, HG_WIDTH), BF16)
        ohg, s_p = _hgrn_call(zhg_p, s0_prompt, lb, gn, mix0, row0=0, batch=bp, seq=tp)
        ohg, s_s = _hgrn_call(zhg_s, state_hgrn[l].astype(F32), lb, gn, ohg, row0=n_p, batch=bs, seq=ts)

        c_p = jnp.cumsum(lf_p, axis=-1)
        jstart = _attn_block_starts(sq_p, c_p, batch=bp, seq=tp, blk=blk)
        ofx = _attn_prompt_call(jstart, qb_p, kb_p, vb_p, c_p, jnp.zeros((n, FOX_WIDTH), BF16),
                                batch=bp, seq=tp, blk=blk)
        c_new = c_cache[l][..., -1:] + jnp.cumsum(lf_s, axis=-1)
        ofx = _attn_decode_call(qb_s, kb_s, vb_s, kc_t, vc_t, c_cache, c_new, ofx,
                                layer=l, row0=n_p, batch=bs, tt=ts)

        w_out_b = w_out[l].astype(BF16)
        x1, h2, idx8, gate8, cnt = _post_call(
            x, ohg, ofx, modg[l, 2:5], norm_ffn[l].reshape(1, d), w_out_b[:HG_WIDTH], w_out_b[HG_WIDTH:],
            w_router[l].T.astype(BF16), b_router[l].astype(F32).reshape(N_EXPERTS, 1), sut)

        dest, tok_buf, blk_e, n_used = _route(idx8, cnt, n)
        h2_big = jnp.concatenate([h2, jnp.zeros((tok_buf.shape[0] - n, d // 2), U32)], axis=0)
        xs = h2_big.at[tok_buf].get(mode="promise_in_bounds")
        bu = jnp.concatenate([b_up[l][:, 0::2], b_up[l][:, 1::2]], axis=-1).reshape(N_EXPERTS, 1, f2)
        yb = _expert_call(blk_e, n_used, xs, w_up, bu, w_down, b_down, layer=l)
        yg = yb.at[dest].get(mode="promise_in_bounds")
        combine = functools.partial(_combine_call, x1, yg, gate8[:TOP_K].T, modg[l], norm_final.reshape(1, d))
        if l < depth - 1:
            x = combine(final=False)
        else:
            y_prompt = combine(final=True, row0=0, rows=n_p).reshape(bp, tp, d)
            y_sample = combine(final=True, row0=n_p, rows=n_s).reshape(bs, ts, d)

        new_s_p.append(s_p); new_lf_p.append(lf_p)
        new_s_s.append(s_s); new_lf_s.append(lf_s)

    st = jnp.stack
    tr = lambda a: jnp.swapaxes(a, -1, -2)
    return (y_prompt, y_sample, st(new_s_p), tr(k_p), tr(v_p), st(new_lf_p),
            st(new_s_s), k_s, v_s, st(new_lf_s))
```

```python
import functools

import jax
import jax.numpy as jnp
from jax import lax
from jax.experimental import pallas as pl
from jax.experimental.pallas import tpu as pltpu

F32 = jnp.float32
BF16 = jnp.bfloat16
U32 = jnp.uint32

HG_HEADS = 4
HG_DIM = 128
HG_WIDTH = HG_HEADS * HG_DIM
FOX_HEADS = 8
FOX_HD = 64
FOX_WIDTH = FOX_HEADS * FOX_HD
FOX_PAIRS = FOX_HEADS // 2
N_EXPERTS = 32
TOP_K = 4
GLA_BLOCK = 16
MOD_GROUP = 64
SWIGLU_ALPHA = 1.702
SWIGLU_LIMIT = 7.0
EPS = 1e-6
NEG_INF = -1e30
F_FLOOR = 1e-30
EXP_UNDERFLOW = 88.0
NORM_SLACK = 1.01

LANES = 128
MXU_DIM = 256
ROW_TILE = 512
HGRN_CHUNK = 512
ATTN_BLOCK = 256
ATTN_WINDOW = 3
ATTN_PAIRS = 4
MOE_TILE = 512
DECODE_HEADS = 8
VMEM_LIMIT = 56 * 1024 * 1024


def _params(*sem):
    return pltpu.CompilerParams(dimension_semantics=sem, vmem_limit_bytes=VMEM_LIMIT)


def _dot(a, b):
    return jnp.dot(a, b, preferred_element_type=F32)


def _dot_nt(a, b):
    return lax.dot_general(a, b, (((1,), (1,)), ((), ())), preferred_element_type=F32)


def _dot_tn(a, b):
    return lax.dot_general(a, b, (((0,), (0,)), ((), ())), preferred_element_type=F32)


def _sigmoid_pair(z):
    e = jnp.exp(-jnp.abs(z))
    r = 1.0 / (1.0 + e)
    er = e * r
    pos = z >= 0.0
    return jnp.where(pos, r, er), jnp.where(pos, er, r)


def _silu(z):
    return z * _sigmoid_pair(z)[0]


def _log_sigmoid(z):
    return jnp.minimum(z, 0.0) - jnp.log(1.0 + jnp.exp(-jnp.abs(z)))


def _rms(x):
    return x * lax.rsqrt(jnp.mean(x * x, axis=-1, keepdims=True) + EPS)


def _modulate(y, scale, shift):
    rows, d = y.shape
    g = rows // MOD_GROUP
    y3 = y.reshape(g, MOD_GROUP, d)
    return (y3 * (1.0 + scale[:, None, :]) + shift[:, None, :]).reshape(rows, d)


def _gated_residual(x, y, gate):
    rows, d = y.shape
    g = rows // MOD_GROUP
    return x + (y.reshape(g, MOD_GROUP, d) * gate[:, None, :]).reshape(rows, d)


def _pack_halves(y):
    w = y.shape[1] // 2
    hi = lax.bitcast_convert_type(y[:, :w].astype(BF16).astype(F32), U32)
    lo = lax.bitcast_convert_type(y[:, w:].astype(BF16).astype(F32), U32)
    return hi | (lo >> 16)


def _unpack_halves(u):
    hi = lax.bitcast_convert_type(u & jnp.uint32(0xFFFF0000), F32)
    lo = lax.bitcast_convert_type(u << 16, F32)
    return jnp.concatenate([hi, lo], axis=-1)


def _ada_kernel(c_ref, w_ref, b_ref, o_ref):
    s = _silu(c_ref[...])
    o_ref[0] = _dot(s.astype(BF16), w_ref[0].astype(BF16)) + b_ref[0]


def _ada_call(c, w_ada, b_ada):
    depth, d, d6 = w_ada.shape
    bc = c.shape[0]
    tn = d6 // 4
    return pl.pallas_call(
        _ada_kernel,
        grid=(depth, d6 // tn),
        in_specs=[
            pl.BlockSpec((bc, d), lambda l, j: (0, 0)),
            pl.BlockSpec((1, d, tn), lambda l, j: (l, 0, j)),
            pl.BlockSpec((1, 1, tn), lambda l, j: (l, 0, j)),
        ],
        out_specs=pl.BlockSpec((1, bc, tn), lambda l, j: (l, 0, j)),
        out_shape=jax.ShapeDtypeStruct((depth, bc, d6), F32),
        compiler_params=_params("parallel", "parallel"),
        name="ada_mod",
    )(c, w_ada, b_ada.reshape(depth, 1, d6))


def _pre_kernel(x_ref, mod_ref, nw_ref, whg_ref, wfx_ref, wff_ref, fb_ref, ind_ref, k_acc, v_acc,
                zhg_ref, q_ref, kb_ref, vb_ref, k_ref, v_ref, lf_ref, nrm_ref, *, nb, tt, time_minor):
    del k_acc, v_acc
    x = x_ref[...]
    h = _modulate(_rms(x) * nw_ref[...], mod_ref[1], mod_ref[0])
    hb = h.astype(BF16)
    zhg_ref[...] = _dot(hb, whg_ref[...])
    zfx = _dot(hb, wfx_ref[...])
    q = zfx[:, :FOX_WIDTH] * (FOX_HD ** -0.5)
    k = zfx[:, FOX_WIDTH:2 * FOX_WIDTH]
    v = zfx[:, 2 * FOX_WIDTH:]
    qb, kb = q.astype(BF16), k.astype(BF16)
    q_ref[...] = qb
    kb_ref[...] = kb
    vb_ref[...] = v.astype(BF16)
    ind = ind_ref[...]
    nrm_ref[:FOX_HEADS, :] = _dot_nt(ind, (qb * qb))
    nrm_ref[FOX_HEADS:, :] = _dot_nt(ind, (kb * kb))
    if time_minor:
        k, v = k.T, v.T
    for b in range(nb):
        for hd in range(FOX_HEADS):
            rows = slice(b * tt, (b + 1) * tt)
            cols = slice(hd * FOX_HD, (hd + 1) * FOX_HD)
            k_ref[0, b, hd] = k[cols, rows] if time_minor else k[rows, cols]
            v_ref[0, b, hd] = v[cols, rows] if time_minor else v[rows, cols]
    ff = _dot_nt(wff_ref[...], hb)
    lf_ref[...] = _log_sigmoid(ff + fb_ref[...])


def _pre_call(x, modg, nw, whg, wfx, wffT, fb, ind, k_acc, v_acc, *, layer, row0, batch, seq):
    d = x.shape[1]
    time_minor = k_acc.shape[-2:] == (FOX_HD, seq) and seq != FOX_HD
    n = batch * seq
    tm = ROW_TILE
    tt = min(seq, tm)
    nb = tm // tt
    assert n % tm == 0 and row0 % tm == 0 and tm % tt == 0 and seq % tt == 0
    i0 = row0 // tm
    g = tm // MOD_GROUP
    tpb = seq // tt

    hm_block = (1, nb, FOX_HEADS, FOX_HD, tt) if time_minor else (1, nb, FOX_HEADS, tt, FOX_HD)

    def head_major_idx(i):
        if nb > 1:
            return (layer, i, 0, 0, 0)
        return (layer, i // tpb, 0, 0, i % tpb) if time_minor else (layer, i // tpb, 0, i % tpb, 0)

    tok = lambda w, dt: jax.ShapeDtypeStruct((n, w), dt)
    hm = jax.ShapeDtypeStruct(k_acc.shape, F32)
    const = lambda shape: pl.BlockSpec(shape, lambda i: (0,) * len(shape))
    any_spec = pl.BlockSpec(memory_space=pl.ANY)
    return pl.pallas_call(
        functools.partial(_pre_kernel, nb=nb, tt=tt, time_minor=time_minor),
        grid=(n // tm,),
        in_specs=[
            pl.BlockSpec((tm, d), lambda i: (i + i0, 0)),
            pl.BlockSpec((2, g, d), lambda i: (0, i + i0, 0)),
            const((1, d)),
            const(whg.shape), const(wfx.shape), const(wffT.shape), const(fb.shape), const(ind.shape),
            any_spec, any_spec,
        ],
        out_specs=[
            pl.BlockSpec((tm, 4 * HG_WIDTH), lambda i: (i, 0)),
            pl.BlockSpec((tm, FOX_WIDTH), lambda i: (i, 0)),
            pl.BlockSpec((tm, FOX_WIDTH), lambda i: (i, 0)),
            pl.BlockSpec((tm, FOX_WIDTH), lambda i: (i, 0)),
            pl.BlockSpec(hm_block, head_major_idx),
            pl.BlockSpec(hm_block, head_major_idx),
            pl.BlockSpec((FOX_HEADS, tm), lambda i: (0, i)),
            pl.BlockSpec((2 * FOX_HEADS, tm), lambda i: (0, i)),
        ],
        out_shape=[tok(4 * HG_WIDTH, F32), tok(FOX_WIDTH, BF16), tok(FOX_WIDTH, BF16),
                   tok(FOX_WIDTH, BF16), hm, hm, jax.ShapeDtypeStruct((FOX_HEADS, n), F32),
                   jax.ShapeDtypeStruct((2 * FOX_HEADS, n), F32)],
        input_output_aliases={8: 4, 9: 5},
        compiler_params=_params("parallel"),
        name="pre_mixer",
    )(x, modg, nw, whg, wfx, wffT, fb, ind, k_acc, v_acc)


def _hgrn_kernel(z_ref, s0_ref, lb_ref, gn_ref, tri_ref, o_acc, o_ref, s_ref,
                 st_ref, q_s, k_s, b_s, oi_s, u_s, sb_s, qe_s, *, chunk):
    del o_acc
    t = pl.program_id(1)
    L = GLA_BLOCK
    W = HG_WIDTH

    @pl.when(t == 0)
    def _():
        for h in range(HG_HEADS):
            st_ref[h] = s0_ref[0, h].T

    lb = lb_ref[...]
    zf = z_ref[:, W:2 * W]
    sig, sig_neg = _sigmoid_pair(zf)
    f = lb + (1.0 - lb) * sig
    g = jnp.log(jnp.maximum(f, F_FLOOR))
    k_s[...] = (1.0 - lb) * sig_neg
    q_s[...] = _silu(z_ref[:, :W]) * (HG_DIM ** -0.5)
    g0 = g.astype(BF16)
    r1 = g - g0.astype(F32)
    g1 = r1.astype(BF16)
    g2 = (r1 - g1.astype(F32)).astype(BF16)
    tri = tri_ref[...]
    cs = tri.shape[0]
    for r in range(chunk // cs):
        rows = slice(r * cs, (r + 1) * cs)
        b_s[rows, :] = _dot(tri, g0[rows]) + _dot(tri, g1[rows]) + _dot(tri, g2[rows])

    H = L // 2
    row_id = lax.broadcasted_iota(jnp.int32, (L, L), 0)
    col_id = lax.broadcasted_iota(jnp.int32, (L, L), 1)
    col_hi = lax.broadcasted_iota(jnp.int32, (H, L), 1)
    gn = gn_ref[...]
    nblk = chunk // L
    block_rows = lambda i: pl.ds(pl.multiple_of(i * L, L), L)

    def intra(i, carry):
        rows = block_rows(i)
        for h in range(HG_HEADS):
            cols = slice(h * HG_DIM, (h + 1) * HG_DIM)
            q = q_s[rows, cols]
            kk = k_s[rows, cols]
            b = b_s[rows, cols]
            vb = z_ref[rows, 2 * W + h * HG_DIM:2 * W + (h + 1) * HG_DIM].astype(BF16)
            att = jnp.zeros((L, L), F32)
            for s in range(H):
                dec = jnp.exp(jnp.minimum(b - b[s:s + 1, :], 0.0))
                col = jnp.sum(q * kk[s:s + 1, :] * dec, axis=-1, keepdims=True)
                att = jnp.where(col_id == s, col, att)
            att_hi = att[H:]
            q_hi, b_hi = q[H:], b[H:]
            for s in range(H, L):
                dec = jnp.exp(jnp.minimum(b_hi - b[s:s + 1, :], 0.0))
                col = jnp.sum(q_hi * kk[s:s + 1, :] * dec, axis=-1, keepdims=True)
                att_hi = jnp.where(col_hi == s, col, att_hi)
            att = jnp.concatenate([att[:H], att_hi], axis=0)
            att = jnp.where(row_id >= col_id, att, 0.0)
            oi_s[rows, cols] = _dot(att.astype(BF16), vb)
            kd = kk * jnp.exp(b[L - 1:L, :] - b)
            u_s[i * HG_HEADS + h] = _dot_tn(vb, kd.astype(BF16))
        return carry

    def scan(i, carry):
        tail = pl.ds(pl.multiple_of(i * L + H, H), H)
        for h in range(HG_HEADS):
            cols = slice(h * HG_DIM, (h + 1) * HG_DIM)
            st = st_ref[h]
            sb_s[i * HG_HEADS + h] = st.astype(BF16)
            st_ref[h] = st * jnp.exp(b_s[tail, cols][H - 1:H, :]) + u_s[i * HG_HEADS + h]
        return carry

    def inter(i, carry):
        rows = block_rows(i)
        for h in range(HG_HEADS):
            cols = slice(h * HG_DIM, (h + 1) * HG_DIM)
            oi_s[rows, cols] += _dot_nt(qe_s[rows, cols], sb_s[i * HG_HEADS + h])
        return carry

    unroll = 2 if nblk % 2 == 0 else 1
    lax.fori_loop(0, nblk, intra, 0, unroll=8 if nblk % 8 == 0 else unroll)
    lax.fori_loop(0, nblk, scan, 0, unroll=unroll)
    qe_s[...] = (q_s[...] * jnp.exp(b_s[...])).astype(BF16)
    lax.fori_loop(0, nblk, inter, 0, unroll=8 if nblk % 8 == 0 else unroll)
    for h in range(HG_HEADS):
        cols = slice(h * HG_DIM, (h + 1) * HG_DIM)
        hg = z_ref[:, 3 * W + h * HG_DIM:3 * W + (h + 1) * HG_DIM]
        o_ref[:, cols] = (_rms(oi_s[:, cols]) * gn * _silu(hg)).astype(BF16)

    @pl.when(t == pl.num_programs(1) - 1)
    def _():
        for h in range(HG_HEADS):
            s_ref[0, h] = st_ref[h].T


def _hgrn_call(zhg, s0, lb, gn, o_acc, *, row0, batch, seq):
    chunk = min(HGRN_CHUNK, seq)
    cs = min(LANES, chunk)
    assert seq % chunk == 0 and chunk % cs == 0 and cs % GLA_BLOCK == 0 and row0 % chunk == 0
    c0 = row0 // chunk
    r = jnp.arange(cs, dtype=jnp.int32)
    tri = ((r[:, None] // GLA_BLOCK == r[None, :] // GLA_BLOCK) & (r[None, :] <= r[:, None])).astype(BF16)
    nt = seq // chunk
    scr = lambda: pltpu.VMEM((chunk, HG_WIDTH), F32)
    per_chunk = chunk // GLA_BLOCK * HG_HEADS
    return pl.pallas_call(
        functools.partial(_hgrn_kernel, chunk=chunk),
        grid=(batch, nt),
        in_specs=[
            pl.BlockSpec((chunk, 4 * HG_WIDTH), lambda b, t: (b * nt + t, 0)),
            pl.BlockSpec((1, HG_HEADS, HG_DIM, HG_DIM), lambda b, t: (b, 0, 0, 0)),
            pl.BlockSpec((1, HG_WIDTH), lambda b, t: (0, 0)),
            pl.BlockSpec((1, HG_DIM), lambda b, t: (0, 0)),
            pl.BlockSpec((cs, cs), lambda b, t: (0, 0)),
            pl.BlockSpec(memory_space=pl.ANY),
        ],
        out_specs=[
            pl.BlockSpec((chunk, HG_WIDTH), lambda b, t: (c0 + b * nt + t, 0)),
            pl.BlockSpec((1, HG_HEADS, HG_DIM, HG_DIM), lambda b, t: (b, 0, 0, 0)),
        ],
        out_shape=[jax.ShapeDtypeStruct(o_acc.shape, BF16),
                   jax.ShapeDtypeStruct((batch, HG_HEADS, HG_DIM, HG_DIM), F32)],
        input_output_aliases={5: 0},
        scratch_shapes=[pltpu.VMEM((HG_HEADS, HG_DIM, HG_DIM), F32), scr(), scr(), scr(), scr(),
                        pltpu.VMEM((per_chunk, HG_DIM, HG_DIM), F32),
                        pltpu.VMEM((per_chunk, HG_DIM, HG_DIM), BF16),
                        pltpu.VMEM((chunk, HG_WIDTH), BF16)],
        compiler_params=_params("parallel", "arbitrary"),
        name="hgrn",
    )(zhg, s0, lb, gn, tri, o_acc)


def _attn_prompt_kernel(jstart_ref, q_ref, k_ref, v_ref, c_ref, cq_ref, o_acc, o_ref, s_scr, *, blk, nq, pairs):
    del o_acc
    b, g, i = pl.program_id(0), pl.program_id(1), pl.program_id(2)
    width = 2 * FOX_HD
    lane = lax.broadcasted_iota(jnp.int32, (blk, width), 1)
    row_id = lax.broadcasted_iota(jnp.int32, (blk, blk), 0)
    col_id = lax.broadcasted_iota(jnp.int32, (blk, blk), 1)
    key_rows = lambda j: pl.ds(pl.multiple_of(j * blk, blk), blk)
    first = (b * FOX_HEADS + g * pairs * 2) * nq + i
    j0 = jstart_ref[first]
    for h in range(1, 2 * pairs):
        j0 = jnp.minimum(j0, jstart_ref[first + h * nq])
    lanes_of = lambda pr: slice(pr * width, (pr + 1) * width)
    qh, c_q = [], []
    for pr in range(pairs):
        q2 = q_ref[:, lanes_of(pr)]
        for hh in range(2):
            qh.append(jnp.where((lane >= hh * FOX_HD) & (lane < (hh + 1) * FOX_HD), q2, jnp.zeros_like(q2)))
            c_q.append(cq_ref[0, pr, :, hh:hh + 1])

    def store(pr, outs):
        o_ref[:, lanes_of(pr)] = jnp.where(lane < FOX_HD, outs[0], outs[1]).astype(BF16)

    win = min(ATTN_WINDOW, nq)

    @pl.when(i - j0 >= win)
    def _():
        for pr in range(pairs):
            def fill(j, carry, causal=False):
                out = []
                for hh in range(2):
                    m, l = carry[2 * hh], carry[2 * hh + 1]
                    s = ((_dot_nt(qh[2 * pr + hh], k_ref[key_rows(j), lanes_of(pr)]) + c_q[2 * pr + hh])
                         - c_ref[0, pr, hh:hh + 1, key_rows(j)])
                    if causal:
                        s = jnp.where(col_id <= row_id, s, NEG_INF)
                    s_scr[hh, j] = s
                    m_new = jnp.maximum(m, jnp.max(s, axis=-1, keepdims=True))
                    l = l * jnp.exp(m - m_new) + jnp.sum(jnp.exp(s - m_new), axis=-1, keepdims=True)
                    out += [m_new, l]
                return tuple(out)

            neg = jnp.full((blk, 1), NEG_INF, F32)
            zero = jnp.zeros((blk, 1), F32)
            stats = lax.fori_loop(j0, i, fill, (neg, zero, neg, zero))
            m0, l0, m1, l1 = fill(i, stats, causal=True)
            ms = (m0, m1)
            inv_l = (1.0 / l0, 1.0 / l1)

            def weigh(j, accs):
                vblk = v_ref[key_rows(j), lanes_of(pr)]
                return tuple(accs[hh] + _dot((jnp.exp(s_scr[hh, j] - ms[hh]) * inv_l[hh]).astype(BF16), vblk)
                             for hh in range(2))

            acc = jnp.zeros((blk, width), F32)
            store(pr, lax.fori_loop(j0, i + 1, weigh, (acc, acc)))

    @pl.when(i - j0 < win)
    def _():
        start = jnp.maximum(i - (win - 1), 0)
        keys = pl.ds(pl.multiple_of(start * blk, blk), win * blk)
        key_pos = start * blk + lax.broadcasted_iota(jnp.int32, (blk, win * blk), 1)
        visible = key_pos <= i * blk + lax.broadcasted_iota(jnp.int32, (blk, win * blk), 0)
        for pr in range(pairs):
            kwin, vwin = k_ref[keys, lanes_of(pr)], v_ref[keys, lanes_of(pr)]
            outs = []
            for hh in range(2):
                s = (_dot_nt(qh[2 * pr + hh], kwin) + c_q[2 * pr + hh]) - c_ref[0, pr, hh:hh + 1, keys]
                s = jnp.where(visible, s, NEG_INF)
                e = jnp.exp(s - jnp.max(s, axis=-1, keepdims=True))
                inv = 1.0 / jnp.sum(e, axis=-1, keepdims=True)
                outs.append(_dot((e * inv).astype(BF16), vwin))
            store(pr, outs)


def _attn_prompt_call(jstart, qb, kb, vb, c, o_acc, *, batch, seq, blk):
    nq = seq // blk
    pairs = ATTN_PAIRS
    groups = FOX_PAIRS // pairs
    width = pairs * 2 * FOX_HD
    kv_spec = pl.BlockSpec((seq, width), lambda b, g, i, js: (b, g))
    c4 = c.reshape(batch, FOX_PAIRS, 2, seq)
    c_rows = c4.transpose(0, 1, 3, 2)
    return pl.pallas_call(
        functools.partial(_attn_prompt_kernel, blk=blk, nq=nq, pairs=pairs),
        grid_spec=pltpu.PrefetchScalarGridSpec(
            num_scalar_prefetch=1,
            grid=(batch, groups, nq),
            in_specs=[
                pl.BlockSpec((blk, width), lambda b, g, i, js: (b * nq + i, g)),
                kv_spec, kv_spec,
                pl.BlockSpec((1, pairs, 2, seq), lambda b, g, i, js: (b, g, 0, 0)),
                pl.BlockSpec((1, pairs, blk, 2), lambda b, g, i, js: (b, g, i, 0)),
                pl.BlockSpec(memory_space=pl.ANY),
            ],
            out_specs=pl.BlockSpec((blk, width), lambda b, g, i, js: (b * nq + i, g)),
            scratch_shapes=[pltpu.VMEM((2, nq, blk, blk), F32)],
        ),
        out_shape=jax.ShapeDtypeStruct(o_acc.shape, BF16),
        input_output_aliases={6: 0},
        compiler_params=_params("parallel", "parallel", "arbitrary"),
        name="attn_prompt",
    )(jstart, qb, kb, vb, c4, c_rows, o_acc)


def _attn_block_starts(sq_norms, c, *, batch, seq, blk):
    def norms(sq):
        return jnp.sqrt(sq).reshape(FOX_HEADS, batch, seq).transpose(1, 0, 2) * NORM_SLACK

    qn, kn = norms(sq_norms[:FOX_HEADS]), norms(sq_norms[FOX_HEADS:])
    nq = seq // blk
    blocks = lambda a: a.reshape(batch, FOX_HEADS, nq, blk)
    k_max = jnp.max(kn, axis=-1, keepdims=True)
    need = jnp.max(blocks(qn * (k_max + kn) + c), axis=-1) + EXP_UNDERFLOW
    c_min = jnp.min(blocks(c), axis=-1)
    skip = c_min[..., None, :] > need[..., :, None]
    lead = jnp.sum(jnp.cumprod(skip.astype(jnp.int32), axis=-1), axis=-1)
    start = jnp.minimum(lead, jnp.arange(nq, dtype=jnp.int32))
    return start.reshape(-1).astype(jnp.int32)


def _attn_decode_kernel(q_ref, kn_ref, vn_ref, kc_ref, vc_ref, cc_ref, cn_ref, cq_ref, o_acc, o_ref, *, tt, heads):
    del o_acc
    row_id = lax.broadcasted_iota(jnp.int32, (tt, tt), 0)
    col_id = lax.broadcasted_iota(jnp.int32, (tt, tt), 1)
    outs = []
    for hh in range(heads):
        cols = slice(hh * FOX_HD, (hh + 1) * FOX_HD)
        q = q_ref[:, cols]
        c_q = cq_ref[0, :, hh:hh + 1]
        s_c = (_dot(q, kc_ref[0, 0, hh].astype(BF16)) + c_q) - cc_ref[0, 0, hh:hh + 1, :]
        s_n = (_dot_nt(q, kn_ref[:, cols]) + c_q) - cn_ref[0, hh:hh + 1, :]
        s_n = jnp.where(col_id <= row_id, s_n, NEG_INF)
        m = jnp.maximum(jnp.max(s_c, axis=-1, keepdims=True), jnp.max(s_n, axis=-1, keepdims=True))
        e_c = jnp.exp(s_c - m)
        e_n = jnp.exp(s_n - m)
        inv_l = 1.0 / (jnp.sum(e_c, axis=-1, keepdims=True) + jnp.sum(e_n, axis=-1, keepdims=True))
        outs.append(_dot_nt((e_c * inv_l).astype(BF16), vc_ref[0, 0, hh].astype(BF16))
                    + _dot((e_n * inv_l).astype(BF16), vn_ref[:, cols]))
    o_ref[...] = jnp.concatenate(outs, axis=-1).astype(BF16)


def _attn_decode_call(qb, kb, vb, kc_t, vc_t, c_cache, c_new, o_acc, *, layer, row0, batch, tt):
    past = kc_t.shape[4]
    heads = DECODE_HEADS
    groups = FOX_HEADS // heads
    assert row0 % tt == 0
    r0 = row0 // tt
    tok_spec = pl.BlockSpec((tt, heads * FOX_HD), lambda b, p: (b, p))
    cache_spec = pl.BlockSpec((1, 1, heads, FOX_HD, past), lambda b, p: (layer, b, p, 0, 0))
    c_rows = c_new.transpose(0, 2, 1)
    return pl.pallas_call(
        functools.partial(_attn_decode_kernel, tt=tt, heads=heads),
        grid=(batch, groups),
        in_specs=[tok_spec, tok_spec, tok_spec, cache_spec, cache_spec,
                  pl.BlockSpec((1, 1, heads, past), lambda b, p: (layer, b, p, 0)),
                  pl.BlockSpec((1, heads, tt), lambda b, p: (b, p, 0)),
                  pl.BlockSpec((1, tt, heads), lambda b, p: (b, 0, p)),
                  pl.BlockSpec(memory_space=pl.ANY)],
        out_specs=pl.BlockSpec((tt, heads * FOX_HD), lambda b, p: (r0 + b, p)),
        out_shape=jax.ShapeDtypeStruct(o_acc.shape, BF16),
        input_output_aliases={8: 0},
        compiler_params=_params("parallel", "parallel"),
        name="attn_decode",
    )(qb, kb, vb, kc_t, vc_t, c_cache, c_new, c_rows, o_acc)


def _post_kernel(x_ref, ohg_ref, ofx_ref, mod_ref, nw_ref, w1_ref, w2_ref,
                 wr_ref, br_ref, sut_ref,
                 x1_ref, h2_ref, idx_ref, gate_ref, cnt_ref, cnt_s):
    i = pl.program_id(0)

    @pl.when(i == 0)
    def _():
        cnt_s[...] = jnp.zeros_like(cnt_s)

    y = _dot(ohg_ref[...], w1_ref[...]) + _dot(ofx_ref[...], w2_ref[...])
    x1 = _gated_residual(x_ref[...], y, mod_ref[0])
    x1_ref[...] = x1
    h2 = _modulate(_rms(x1) * nw_ref[...], mod_ref[2], mod_ref[1])
    h2_ref[...] = _pack_halves(h2)
    logits = _dot_nt(wr_ref[...], h2.astype(BF16)) + br_ref[...]
    e_f = lax.broadcasted_iota(jnp.int32, logits.shape, 0).astype(F32)
    work = logits
    vals, hots = [], []
    for _ in range(TOP_K):
        mx = jnp.max(work, axis=0, keepdims=True)
        ix = jnp.min(jnp.where(work == mx, e_f, float(N_EXPERTS)), axis=0, keepdims=True)
        hot = e_f == ix
        vals.append(mx)
        hots.append(hot)
        work = jnp.where(hot, -jnp.inf, work)
    exps = [jnp.exp(v - vals[0]) for v in vals]
    den = exps[0] + exps[1] + exps[2] + exps[3]
    hot_f = [h.astype(F32) for h in hots]
    hot_all = hot_f[0] + hot_f[1] + hot_f[2] + hot_f[3]
    before = cnt_s[:, 0:1] + _dot(hot_all.astype(BF16), sut_ref[...])
    for k in range(TOP_K):
        idx_ref[k:k + 1, :] = jnp.sum(hot_f[k] * e_f, axis=0, keepdims=True).astype(jnp.int32)
        idx_ref[TOP_K + k:TOP_K + k + 1, :] = jnp.sum(
            hot_f[k] * before, axis=0, keepdims=True).astype(jnp.int32)
        gate_ref[k:k + 1, :] = exps[k] / den
        gate_ref[TOP_K + k:TOP_K + k + 1, :] = jnp.zeros_like(den)
    total = cnt_s[...] + jnp.sum(hot_all, axis=1, keepdims=True)
    cnt_s[...] = total
    cnt_ref[...] = total


def _post_call(x, ohg, ofx, modg, nw, w1, w2, wr, br, sut):
    n, d = x.shape
    tm = ROW_TILE
    g = tm // MOD_GROUP
    const = lambda shape: pl.BlockSpec(shape, lambda i: (0,) * len(shape))
    row = lambda w: pl.BlockSpec((tm, w), lambda i: (i, 0))
    return pl.pallas_call(
        _post_kernel,
        grid=(n // tm,),
        in_specs=[
            row(d), row(HG_WIDTH), row(FOX_WIDTH),
            pl.BlockSpec((3, g, d), lambda i: (0, i, 0)),
            const((1, d)), const(w1.shape), const(w2.shape),
            const(wr.shape), const(br.shape), const(sut.shape),
        ],
        out_specs=[
            row(d), row(d // 2),
            pl.BlockSpec((2 * TOP_K, tm), lambda i: (0, i)),
            pl.BlockSpec((2 * TOP_K, tm), lambda i: (0, i)),
            const((N_EXPERTS, LANES)),
        ],
        out_shape=[
            jax.ShapeDtypeStruct((n, d), F32), jax.ShapeDtypeStruct((n, d // 2), U32),
            jax.ShapeDtypeStruct((2 * TOP_K, n), jnp.int32),
            jax.ShapeDtypeStruct((2 * TOP_K, n), F32),
            jax.ShapeDtypeStruct((N_EXPERTS, LANES), F32),
        ],
        scratch_shapes=[pltpu.VMEM((N_EXPERTS, LANES), F32)],
        compiler_params=_params("arbitrary"),
        name="post_mixer",
    )(x, ohg, ofx, modg, nw, w1, w2, wr, br, sut)


def _expert_kernel(blk_e_ref, n_used_ref, x_ref, wu_ref, bu_ref, wd_ref, bd_ref, perm_ref, o_ref, wu_s, wd_s):
    i = pl.program_id(0)
    used = i < n_used_ref[0]
    new_expert = jnp.logical_or(i == 0, blk_e_ref[i] != blk_e_ref[jnp.maximum(i - 1, 0)])

    @pl.when(jnp.logical_and(used, new_expert))
    def _():
        wd_s[...] = wd_ref[0, 0].astype(BF16)
        perm = perm_ref[...]
        f = wu_s.shape[1] // 2
        half = MXU_DIM // 2
        for t in range(wu_s.shape[1] // MXU_DIM):
            tile = wu_ref[0, 0, :, t * MXU_DIM:(t + 1) * MXU_DIM].astype(BF16)
            r = _dot(tile, perm).astype(BF16)
            wu_s[:, t * half:(t + 1) * half] = r[:, :half]
            wu_s[:, f + t * half:f + (t + 1) * half] = r[:, half:]

    @pl.when(used)
    def _():
        x = _unpack_halves(x_ref[...]).astype(BF16)
        u = _dot(x, wu_s[...]) + bu_ref[0]
        f = u.shape[1] // 2
        glu = jnp.minimum(u[:, :f], SWIGLU_LIMIT)
        lin = jnp.clip(u[:, f:], -SWIGLU_LIMIT, SWIGLU_LIMIT)
        a = glu * _sigmoid_pair(SWIGLU_ALPHA * glu)[0] * (lin + 1.0)
        o_ref[...] = _pack_halves(_dot(a.astype(BF16), wd_s[...]) + bd_ref[0, 0])

    @pl.when(jnp.logical_not(used))
    def _():
        o_ref[...] = jnp.zeros_like(o_ref)


def _expert_call(blk_e, n_used, xs, w_up, bu, w_down, b_down, *, layer):
    p, dh = xs.shape
    d = 2 * dh
    f2 = w_up.shape[3]
    tm = MOE_TILE
    j = jnp.arange(MXU_DIM, dtype=jnp.int32)
    half = MXU_DIM // 2
    src = jnp.where(j < half, 2 * j, 2 * (j - half) + 1)
    perm = (j[:, None] == src[None, :]).astype(BF16)
    return pl.pallas_call(
        _expert_kernel,
        grid_spec=pltpu.PrefetchScalarGridSpec(
            num_scalar_prefetch=2,
            grid=(p // tm,),
            in_specs=[
                pl.BlockSpec((tm, dh), lambda i, be, nu: (jnp.minimum(i, nu[0] - 1), 0)),
                pl.BlockSpec((1, 1, d, f2), lambda i, be, nu: (layer, be[i], 0, 0)),
                pl.BlockSpec((1, 1, f2), lambda i, be, nu: (be[i], 0, 0)),
                pl.BlockSpec((1, 1, f2 // 2, d), lambda i, be, nu: (layer, be[i], 0, 0)),
                pl.BlockSpec((1, 1, 1, d), lambda i, be, nu: (layer, be[i], 0, 0)),
                pl.BlockSpec((MXU_DIM, MXU_DIM), lambda i, be, nu: (0, 0)),
            ],
            out_specs=pl.BlockSpec((tm, dh), lambda i, be, nu: (i, 0)),
            scratch_shapes=[pltpu.VMEM((d, f2), BF16), pltpu.VMEM((f2 // 2, d), BF16)],
        ),
        out_shape=jax.ShapeDtypeStruct((p, dh), U32),
        compiler_params=_params("arbitrary"),
        name="experts",
    )(blk_e, n_used, xs, w_up, bu, w_down, b_down.reshape(b_down.shape[0], N_EXPERTS, 1, d), perm)


def _combine_kernel(x_ref, y_ref, gate_ref, mod_ref, nw_ref, o_ref, *, final):
    gates = gate_ref[...]
    y = gates[:, 0:1] * _unpack_halves(y_ref[0])
    for k in range(1, TOP_K):
        y = y + gates[:, k:k + 1] * _unpack_halves(y_ref[k])
    x2 = _gated_residual(x_ref[...], y, mod_ref[0])
    if final:
        x2 = _rms(x2) * nw_ref[...]
    o_ref[...] = x2


def _combine_call(x1, yg, gates, modg, nw, *, final, row0=0, rows=None):
    n, d = x1.shape
    rows = n if rows is None else rows
    tm = ROW_TILE
    g = tm // MOD_GROUP
    assert row0 % tm == 0 and rows % tm == 0
    i0 = row0 // tm
    return pl.pallas_call(
        functools.partial(_combine_kernel, final=final),
        grid=(rows // tm,),
        in_specs=[
            pl.BlockSpec((tm, d), lambda i: (i + i0, 0)),
            pl.BlockSpec((TOP_K, tm, d // 2), lambda i: (0, i + i0, 0)),
            pl.BlockSpec((tm, TOP_K), lambda i: (i + i0, 0)),
            pl.BlockSpec((1, g, d), lambda i: (5, i + i0, 0)),
            pl.BlockSpec((1, d), lambda i: (0, 0)),
        ],
        out_specs=pl.BlockSpec((tm, d), lambda i: (i, 0)),
        out_shape=jax.ShapeDtypeStruct((rows, d), F32),
        compiler_params=_params("parallel"),
        name="combine",
    )(x1, yg, gates, modg, nw)


def _hgrn2_lower_bounds(lb_logits):
    p = jax.nn.softmax(lb_logits.astype(F32), axis=0)
    return jnp.cumsum(p, axis=0) - p[0:1]


def _cumsum_last(x):
    t = x.shape[-1]
    nb = t // LANES
    xb = x.reshape(-1, nb, LANES)
    i = jnp.arange(LANES)
    upper = (i[:, None] <= i[None, :]).astype(F32)
    within = jnp.einsum("rnk,kj->rnj", xb, upper, precision=lax.Precision.HIGHEST)
    j = jnp.arange(nb)
    strict = (j[:, None] < j[None, :]).astype(F32)
    offset = jnp.dot(within[..., -1], strict, precision=lax.Precision.HIGHEST)
    return (within + offset[..., None]).reshape(x.shape)


def _route(idx8, cnt, n_tok):
    a = n_tok * TOP_K
    n_blocks = -(-a // MOE_TILE) + N_EXPERTS
    counts = cnt[:, 0].astype(jnp.int32)
    padded = (counts + MOE_TILE - 1) // MOE_TILE * MOE_TILE
    pends = jnp.cumsum(padded)
    pstarts = pends - padded
    e_ids = jnp.arange(N_EXPERTS, dtype=jnp.int32)
    base = jnp.sum(jnp.where(idx8[:TOP_K, :, None] == e_ids, pstarts, 0), axis=-1)
    dest = base + idx8[TOP_K:]
    token = jnp.broadcast_to(jnp.arange(n_tok, dtype=jnp.int32)[None, :], (TOP_K, n_tok))
    filler = jnp.arange(n_blocks * MOE_TILE, dtype=jnp.int32) % n_tok
    tok_buf = filler.at[dest.reshape(-1)].set(token.reshape(-1), unique_indices=True, mode="promise_in_bounds")
    blk_start = jnp.arange(n_blocks, dtype=jnp.int32) * MOE_TILE
    blk_e = jnp.minimum(jnp.sum(pends[None, :] <= blk_start[:, None], axis=1), N_EXPERTS - 1).astype(jnp.int32)
    n_used = (pends[-1:] // MOE_TILE).astype(jnp.int32)
    return dest, tok_buf, blk_e, n_used


def kernel(x_prompt, x_sample, c_prompt, c_sample, state_hgrn, cache_fox_k, cache_fox_v, cache_fox_lf,
           w_ada, b_ada, norm_mix, norm_ffn, w_in, lb_logits, hg_norm, fox_fb, w_out,
           w_router, b_router, w_up, b_up, w_down, b_down, norm_final):
    bp, tp, d = x_prompt.shape
    bs, ts, _ = x_sample.shape
    depth = w_in.shape[0]
    n_p, n_s = bp * tp, bs * ts
    n = n_p + n_s
    past = cache_fox_lf.shape[-1]
    assert tp % MOD_GROUP == 0 and ts == MOD_GROUP
    assert n_p % ROW_TILE == 0 and n_s % ROW_TILE == 0

    x = jnp.concatenate([x_prompt.reshape(n_p, d), x_sample.reshape(n_s, d)], axis=0)
    c = jnp.concatenate([c_prompt, c_sample], axis=0)
    bc = c.shape[0]
    c = jnp.pad(c, ((0, (-bc) % 8), (0, 0)))
    mod = _ada_call(c, w_ada, b_ada)
    mod_rows = jnp.concatenate([jnp.repeat(mod[:, :bp], tp // MOD_GROUP, axis=1), mod[:, bp:bp + bs]], axis=1)
    modg = mod_rows.reshape(depth, n // MOD_GROUP, 6, d).transpose(0, 2, 1, 3)

    lbs = _hgrn2_lower_bounds(lb_logits)
    t = jnp.arange(ROW_TILE, dtype=jnp.int32)
    sut = (t[:, None] < t[None, :]).astype(BF16)
    s0_prompt = jnp.zeros((bp, HG_HEADS, HG_DIM, HG_DIM), F32)
    f2 = w_up.shape[-1]
    blk = min(ATTN_BLOCK, tp)
    assert tp % blk == 0
    c_cache = _cumsum_last(cache_fox_lf.astype(F32))
    head_ind = (jnp.arange(FOX_WIDTH)[None, :] // FOX_HD == jnp.arange(FOX_HEADS)[:, None]).astype(BF16)

    kc_t = jnp.swapaxes(cache_fox_k, -1, -2)
    vc_t = jnp.swapaxes(cache_fox_v, -1, -2)
    k_p = jnp.zeros((depth, bp, FOX_HEADS, FOX_HD, tp), F32)
    v_p = jnp.zeros_like(k_p)
    k_s = jnp.zeros((depth, bs, FOX_HEADS, ts, FOX_HD), F32)
    v_s = jnp.zeros_like(k_s)
    new_s_p, new_lf_p, new_s_s, new_lf_s = [], [], [], []
    for l in range(depth):
        o4 = 4 * HG_WIDTH
        w_in_b = w_in[l].astype(BF16)
        whg = w_in_b[:, :o4]
        wfx = w_in_b[:, o4:o4 + 3 * FOX_WIDTH]
        wffT = w_in_b[:, o4 + 3 * FOX_WIDTH:].T
        fb = fox_fb[l].astype(F32).reshape(FOX_HEADS, 1)
        nw_mix = norm_mix[l].reshape(1, d)
        pre = functools.partial(_pre_call, x, modg[l], nw_mix, whg, wfx, wffT, fb, head_ind, layer=l)
        zhg_p, qb_p, kb_p, vb_p, k_p, v_p, lfT_p, sq_p = pre(k_p, v_p, row0=0, batch=bp, seq=tp)
        zhg_s, qb_s, kb_s, vb_s, k_s, v_s, lfT_s, _ = pre(k_s, v_s, row0=n_p, batch=bs, seq=ts)
        lf_p = lfT_p.reshape(FOX_HEADS, bp, tp).transpose(1, 0, 2)
        lf_s = lfT_s.reshape(FOX_HEADS, bs, ts).transpose(1, 0, 2)

        lb = lbs[l].reshape(1, HG_WIDTH)
        gn = hg_norm[l].astype(F32).reshape(1, HG_DIM)
        mix0 = jnp.zeros((n, HG_WIDTH), BF16)
        ohg, s_p = _hgrn_call(zhg_p, s0_prompt, lb, gn, mix0, row0=0, batch=bp, seq=tp)
        ohg, s_s = _hgrn_call(zhg_s, state_hgrn[l].astype(F32), lb, gn, ohg, row0=n_p, batch=bs, seq=ts)

        c_p = jnp.cumsum(lf_p, axis=-1)
        jstart = _attn_block_starts(sq_p, c_p, batch=bp, seq=tp, blk=blk)
        ofx = _attn_prompt_call(jstart, qb_p, kb_p, vb_p, c_p, jnp.zeros((n, FOX_WIDTH), BF16),
                                batch=bp, seq=tp, blk=blk)
        c_new = c_cache[l][..., -1:] + jnp.cumsum(lf_s, axis=-1)
        ofx = _attn_decode_call(qb_s, kb_s, vb_s, kc_t, vc_t, c_cache, c_new, ofx,
                                layer=l, row0=n_p, batch=bs, tt=ts)

        w_out_b = w_out[l].astype(BF16)
        x1, h2, idx8, gate8, cnt = _post_call(
            x, ohg, ofx, modg[l, 2:5], norm_ffn[l].reshape(1, d), w_out_b[:HG_WIDTH], w_out_b[HG_WIDTH:],
            w_router[l].T.astype(BF16), b_router[l].astype(F32).reshape(N_EXPERTS, 1), sut)

        dest, tok_buf, blk_e, n_used = _route(idx8, cnt, n)
        h2_big = jnp.concatenate([h2, jnp.zeros((tok_buf.shape[0] - n, d // 2), U32)], axis=0)
        xs = h2_big.at[tok_buf].get(mode="promise_in_bounds")
        bu = jnp.concatenate([b_up[l][:, 0::2], b_up[l][:, 1::2]], axis=-1).reshape(N_EXPERTS, 1, f2)
        yb = _expert_call(blk_e, n_used, xs, w_up, bu, w_down, b_down, layer=l)
        yg = yb.at[dest].get(mode="promise_in_bounds")
        combine = functools.partial(_combine_call, x1, yg, gate8[:TOP_K].T, modg[l], norm_final.reshape(1, d))
        if l < depth - 1:
            x = combine(final=False)
        else:
            y_prompt = combine(final=True, row0=0, rows=n_p).reshape(bp, tp, d)
            y_sample = combine(final=True, row0=n_p, rows=n_s).reshape(bs, ts, d)

        new_s_p.append(s_p); new_lf_p.append(lf_p)
        new_s_s.append(s_s); new_lf_s.append(lf_s)

    st = jnp.stack
    tr = lambda a: jnp.swapaxes(a, -1, -2)
    return (y_prompt, y_sample, st(new_s_p), tr(k_p), tr(v_p), st(new_lf_p),
            st(new_s_s), k_s, v_s, st(new_lf_s))
```

```python
import functools

import jax
import jax.numpy as jnp
from jax import lax
from jax.experimental import pallas as pl
from jax.experimental.pallas import tpu as pltpu

F32 = jnp.float32
BF16 = jnp.bfloat16
U32 = jnp.uint32

HG_HEADS = 4
HG_DIM = 128
HG_WIDTH = HG_HEADS * HG_DIM
FOX_HEADS = 8
FOX_HD = 64
FOX_WIDTH = FOX_HEADS * FOX_HD
FOX_PAIRS = FOX_HEADS // 2
N_EXPERTS = 32
TOP_K = 4
GLA_BLOCK = 16
MOD_GROUP = 64
SWIGLU_ALPHA = 1.702
SWIGLU_LIMIT = 7.0
EPS = 1e-6
NEG_INF = -1e30
F_FLOOR = 1e-30
EXP_UNDERFLOW = 88.0
NORM_SLACK = 1.01

LANES = 128
MXU_DIM = 256
ROW_TILE = 512
HGRN_CHUNK = 512
ATTN_BLOCK = 256
ATTN_WINDOW = 3
ATTN_PAIRS = 4
MOE_TILE = 512
DECODE_HEADS = 8
VMEM_LIMIT = 56 * 1024 * 1024


def _params(*sem):
    return pltpu.CompilerParams(dimension_semantics=sem, vmem_limit_bytes=VMEM_LIMIT)


def _dot(a, b):
    return jnp.dot(a, b, preferred_element_type=F32)


def _dot_nt(a, b):
    return lax.dot_general(a, b, (((1,), (1,)), ((), ())), preferred_element_type=F32)


def _dot_tn(a, b):
    return lax.dot_general(a, b, (((0,), (0,)), ((), ())), preferred_element_type=F32)


def _sigmoid_pair(z):
    e = jnp.exp(-jnp.abs(z))
    r = 1.0 / (1.0 + e)
    er = e * r
    pos = z >= 0.0
    return jnp.where(pos, r, er), jnp.where(pos, er, r)


def _silu(z):
    return z * _sigmoid_pair(z)[0]


def _log_sigmoid(z):
    return jnp.minimum(z, 0.0) - jnp.log(1.0 + jnp.exp(-jnp.abs(z)))


def _rms(x):
    return x * lax.rsqrt(jnp.mean(x * x, axis=-1, keepdims=True) + EPS)


def _modulate(y, scale, shift):
    rows, d = y.shape
    g = rows // MOD_GROUP
    y3 = y.reshape(g, MOD_GROUP, d)
    return (y3 * (1.0 + scale[:, None, :]) + shift[:, None, :]).reshape(rows, d)


def _gated_residual(x, y, gate):
    rows, d = y.shape
    g = rows // MOD_GROUP
    return x + (y.reshape(g, MOD_GROUP, d) * gate[:, None, :]).reshape(rows, d)


def _pack_halves(y):
    w = y.shape[1] // 2
    hi = lax.bitcast_convert_type(y[:, :w].astype(BF16).astype(F32), U32)
    lo = lax.bitcast_convert_type(y[:, w:].astype(BF16).astype(F32), U32)
    return hi | (lo >> 16)


def _unpack_halves(u):
    hi = lax.bitcast_convert_type(u & jnp.uint32(0xFFFF0000), F32)
    lo = lax.bitcast_convert_type(u << 16, F32)
    return jnp.concatenate([hi, lo], axis=-1)


def _ada_kernel(c_ref, w_ref, b_ref, o_ref):
    s = _silu(c_ref[...])
    o_ref[0] = _dot(s.astype(BF16), w_ref[0].astype(BF16)) + b_ref[0]


def _ada_call(c, w_ada, b_ada):
    depth, d, d6 = w_ada.shape
    bc = c.shape[0]
    tn = d6 // 4
    return pl.pallas_call(
        _ada_kernel,
        grid=(depth, d6 // tn),
        in_specs=[
            pl.BlockSpec((bc, d), lambda l, j: (0, 0)),
            pl.BlockSpec((1, d, tn), lambda l, j: (l, 0, j)),
            pl.BlockSpec((1, 1, tn), lambda l, j: (l, 0, j)),
        ],
        out_specs=pl.BlockSpec((1, bc, tn), lambda l, j: (l, 0, j)),
        out_shape=jax.ShapeDtypeStruct((depth, bc, d6), F32),
        compiler_params=_params("parallel", "parallel"),
        name="ada_mod",
    )(c, w_ada, b_ada.reshape(depth, 1, d6))


def _pre_kernel(x_ref, mod_ref, nw_ref, whg_ref, wfx_ref, wff_ref, fb_ref, ind_ref, k_acc, v_acc,
                zhg_ref, q_ref, kb_ref, vb_ref, k_ref, v_ref, lf_ref, nrm_ref, *, nb, tt, time_minor):
    del k_acc, v_acc
    x = x_ref[...]
    h = _modulate(_rms(x) * nw_ref[...], mod_ref[1], mod_ref[0])
    hb = h.astype(BF16)
    zhg_ref[...] = _dot(hb, whg_ref[...])
    zfx = _dot(hb, wfx_ref[...])
    q = zfx[:, :FOX_WIDTH] * (FOX_HD ** -0.5)
    k = zfx[:, FOX_WIDTH:2 * FOX_WIDTH]
    v = zfx[:, 2 * FOX_WIDTH:]
    qb, kb = q.astype(BF16), k.astype(BF16)
    q_ref[...] = qb
    kb_ref[...] = kb
    vb_ref[...] = v.astype(BF16)
    ind = ind_ref[...]
    nrm_ref[:FOX_HEADS, :] = _dot_nt(ind, (qb * qb))
    nrm_ref[FOX_HEADS:, :] = _dot_nt(ind, (kb * kb))
    if time_minor:
        k, v = k.T, v.T
    for b in range(nb):
        for hd in range(FOX_HEADS):
            rows = slice(b * tt, (b + 1) * tt)
            cols = slice(hd * FOX_HD, (hd + 1) * FOX_HD)
            k_ref[0, b, hd] = k[cols, rows] if time_minor else k[rows, cols]
            v_ref[0, b, hd] = v[cols, rows] if time_minor else v[rows, cols]
    ff = _dot_nt(wff_ref[...], hb)
    lf_ref[...] = _log_sigmoid(ff + fb_ref[...])


def _pre_call(x, modg, nw, whg, wfx, wffT, fb, ind, k_acc, v_acc, *, layer, row0, batch, seq):
    d = x.shape[1]
    time_minor = k_acc.shape[-2:] == (FOX_HD, seq) and seq != FOX_HD
    n = batch * seq
    tm = ROW_TILE
    tt = min(seq, tm)
    nb = tm // tt
    assert n % tm == 0 and row0 % tm == 0 and tm % tt == 0 and seq % tt == 0
    i0 = row0 // tm
    g = tm // MOD_GROUP
    tpb = seq // tt

    hm_block = (1, nb, FOX_HEADS, FOX_HD, tt) if time_minor else (1, nb, FOX_HEADS, tt, FOX_HD)

    def head_major_idx(i):
        if nb > 1:
            return (layer, i, 0, 0, 0)
        return (layer, i // tpb, 0, 0, i % tpb) if time_minor else (layer, i // tpb, 0, i % tpb, 0)

    tok = lambda w, dt: jax.ShapeDtypeStruct((n, w), dt)
    hm = jax.ShapeDtypeStruct(k_acc.shape, F32)
    const = lambda shape: pl.BlockSpec(shape, lambda i: (0,) * len(shape))
    any_spec = pl.BlockSpec(memory_space=pl.ANY)
    return pl.pallas_call(
        functools.partial(_pre_kernel, nb=nb, tt=tt, time_minor=time_minor),
        grid=(n // tm,),
        in_specs=[
            pl.BlockSpec((tm, d), lambda i: (i + i0, 0)),
            pl.BlockSpec((2, g, d), lambda i: (0, i + i0, 0)),
            const((1, d)),
            const(whg.shape), const(wfx.shape), const(wffT.shape), const(fb.shape), const(ind.shape),
            any_spec, any_spec,
        ],
        out_specs=[
            pl.BlockSpec((tm, 4 * HG_WIDTH), lambda i: (i, 0)),
            pl.BlockSpec((tm, FOX_WIDTH), lambda i: (i, 0)),
            pl.BlockSpec((tm, FOX_WIDTH), lambda i: (i, 0)),
            pl.BlockSpec((tm, FOX_WIDTH), lambda i: (i, 0)),
            pl.BlockSpec(hm_block, head_major_idx),
            pl.BlockSpec(hm_block, head_major_idx),
            pl.BlockSpec((FOX_HEADS, tm), lambda i: (0, i)),
            pl.BlockSpec((2 * FOX_HEADS, tm), lambda i: (0, i)),
        ],
        out_shape=[tok(4 * HG_WIDTH, F32), tok(FOX_WIDTH, BF16), tok(FOX_WIDTH, BF16),
                   tok(FOX_WIDTH, BF16), hm, hm, jax.ShapeDtypeStruct((FOX_HEADS, n), F32),
                   jax.ShapeDtypeStruct((2 * FOX_HEADS, n), F32)],
        input_output_aliases={8: 4, 9: 5},
        compiler_params=_params("parallel"),
        name="pre_mixer",
    )(x, modg, nw, whg, wfx, wffT, fb, ind, k_acc, v_acc)


def _hgrn_kernel(z_ref, s0_ref, lb_ref, gn_ref, tri_ref, o_acc, o_ref, s_ref,
                 st_ref, q_s, k_s, b_s, oi_s, u_s, sb_s, qe_s, *, chunk):
    del o_acc
    t = pl.program_id(1)
    L = GLA_BLOCK
    W = HG_WIDTH

    @pl.when(t == 0)
    def _():
        for h in range(HG_HEADS):
            st_ref[h] = s0_ref[0, h].T

    lb = lb_ref[...]
    zf = z_ref[:, W:2 * W]
    sig, sig_neg = _sigmoid_pair(zf)
    f = lb + (1.0 - lb) * sig
    g = jnp.log(jnp.maximum(f, F_FLOOR))
    k_s[...] = (1.0 - lb) * sig_neg
    q_s[...] = _silu(z_ref[:, :W]) * (HG_DIM ** -0.5)
    g0 = g.astype(BF16)
    r1 = g - g0.astype(F32)
    g1 = r1.astype(BF16)
    g2 = (r1 - g1.astype(F32)).astype(BF16)
    tri = tri_ref[...]
    cs = tri.shape[0]
    for r in range(chunk // cs):
        rows = slice(r * cs, (r + 1) * cs)
        b_s[rows, :] = _dot(tri, g0[rows]) + _dot(tri, g1[rows]) + _dot(tri, g2[rows])

    H = L // 2
    row_id = lax.broadcasted_iota(jnp.int32, (L, L), 0)
    col_id = lax.broadcasted_iota(jnp.int32, (L, L), 1)
    col_hi = lax.broadcasted_iota(jnp.int32, (H, L), 1)
    gn = gn_ref[...]
    nblk = chunk // L
    block_rows = lambda i: pl.ds(pl.multiple_of(i * L, L), L)

    def intra(i, carry):
        rows = block_rows(i)
        for h in range(HG_HEADS):
            cols = slice(h * HG_DIM, (h + 1) * HG_DIM)
            q = q_s[rows, cols]
            kk = k_s[rows, cols]
            b = b_s[rows, cols]
            vb = z_ref[rows, 2 * W + h * HG_DIM:2 * W + (h + 1) * HG_DIM].astype(BF16)
            att = jnp.zeros((L, L), F32)
            for s in range(H):
                dec = jnp.exp(jnp.minimum(b - b[s:s + 1, :], 0.0))
                col = jnp.sum(q * kk[s:s + 1, :] * dec, axis=-1, keepdims=True)
                att = jnp.where(col_id == s, col, att)
            att_hi = att[H:]
            q_hi, b_hi = q[H:], b[H:]
            for s in range(H, L):
                dec = jnp.exp(jnp.minimum(b_hi - b[s:s + 1, :], 0.0))
                col = jnp.sum(q_hi * kk[s:s + 1, :] * dec, axis=-1, keepdims=True)
                att_hi = jnp.where(col_hi == s, col, att_hi)
            att = jnp.concatenate([att[:H], att_hi], axis=0)
            att = jnp.where(row_id >= col_id, att, 0.0)
            oi_s[rows, cols] = _dot(att.astype(BF16), vb)
            kd = kk * jnp.exp(b[L - 1:L, :] - b)
            u_s[i * HG_HEADS + h] = _dot_tn(vb, kd.astype(BF16))
        return carry

    def scan(i, carry):
        tail = pl.ds(pl.multiple_of(i * L + H, H), H)
        for h in range(HG_HEADS):
            cols = slice(h * HG_DIM, (h + 1) * HG_DIM)
            st = st_ref[h]
            sb_s[i * HG_HEADS + h] = st.astype(BF16)
            st_ref[h] = st * jnp.exp(b_s[tail, cols][H - 1:H, :]) + u_s[i * HG_HEADS + h]
        return carry

    def inter(i, carry):
        rows = block_rows(i)
        for h in range(HG_HEADS):
            cols = slice(h * HG_DIM, (h + 1) * HG_DIM)
            oi_s[rows, cols] += _dot_nt(qe_s[rows, cols], sb_s[i * HG_HEADS + h])
        return carry

    unroll = 2 if nblk % 2 == 0 else 1
    lax.fori_loop(0, nblk, intra, 0, unroll=8 if nblk % 8 == 0 else unroll)
    lax.fori_loop(0, nblk, scan, 0, unroll=unroll)
    qe_s[...] = (q_s[...] * jnp.exp(b_s[...])).astype(BF16)
    lax.fori_loop(0, nblk, inter, 0, unroll=8 if nblk % 8 == 0 else unroll)
    for h in range(HG_HEADS):
        cols = slice(h * HG_DIM, (h + 1) * HG_DIM)
        hg = z_ref[:, 3 * W + h * HG_DIM:3 * W + (h + 1) * HG_DIM]
        o_ref[:, cols] = (_rms(oi_s[:, cols]) * gn * _silu(hg)).astype(BF16)

    @pl.when(t == pl.num_programs(1) - 1)
    def _():
        for h in range(HG_HEADS):
            s_ref[0, h] = st_ref[h].T


def _hgrn_call(zhg, s0, lb, gn, o_acc, *, row0, batch, seq):
    chunk = min(HGRN_CHUNK, seq)
    cs = min(LANES, chunk)
    assert seq % chunk == 0 and chunk % cs == 0 and cs % GLA_BLOCK == 0 and row0 % chunk == 0
    c0 = row0 // chunk
    r = jnp.arange(cs, dtype=jnp.int32)
    tri = ((r[:, None] // GLA_BLOCK == r[None, :] // GLA_BLOCK) & (r[None, :] <= r[:, None])).astype(BF16)
    nt = seq // chunk
    scr = lambda: pltpu.VMEM((chunk, HG_WIDTH), F32)
    per_chunk = chunk // GLA_BLOCK * HG_HEADS
    return pl.pallas_call(
        functools.partial(_hgrn_kernel, chunk=chunk),
        grid=(batch, nt),
        in_specs=[
            pl.BlockSpec((chunk, 4 * HG_WIDTH), lambda b, t: (b * nt + t, 0)),
            pl.BlockSpec((1, HG_HEADS, HG_DIM, HG_DIM), lambda b, t: (b, 0, 0, 0)),
            pl.BlockSpec((1, HG_WIDTH), lambda b, t: (0, 0)),
            pl.BlockSpec((1, HG_DIM), lambda b, t: (0, 0)),
            pl.BlockSpec((cs, cs), lambda b, t: (0, 0)),
            pl.BlockSpec(memory_space=pl.ANY),
        ],
        out_specs=[
            pl.BlockSpec((chunk, HG_WIDTH), lambda b, t: (c0 + b * nt + t, 0)),
            pl.BlockSpec((1, HG_HEADS, HG_DIM, HG_DIM), lambda b, t: (b, 0, 0, 0)),
        ],
        out_shape=[jax.ShapeDtypeStruct(o_acc.shape, BF16),
                   jax.ShapeDtypeStruct((batch, HG_HEADS, HG_DIM, HG_DIM), F32)],
        input_output_aliases={5: 0},
        scratch_shapes=[pltpu.VMEM((HG_HEADS, HG_DIM, HG_DIM), F32), scr(), scr(), scr(), scr(),
                        pltpu.VMEM((per_chunk, HG_DIM, HG_DIM), F32),
                        pltpu.VMEM((per_chunk, HG_DIM, HG_DIM), BF16),
                        pltpu.VMEM((chunk, HG_WIDTH), BF16)],
        compiler_params=_params("parallel", "arbitrary"),
        name="hgrn",
    )(zhg, s0, lb, gn, tri, o_acc)


def _attn_prompt_kernel(jstart_ref, q_ref, k_ref, v_ref, c_ref, cq_ref, o_acc, o_ref, s_scr, *, blk, nq, pairs):
    del o_acc
    b, g, i = pl.program_id(0), pl.program_id(1), pl.program_id(2)
    width = 2 * FOX_HD
    lane = lax.broadcasted_iota(jnp.int32, (blk, width), 1)
    row_id = lax.broadcasted_iota(jnp.int32, (blk, blk), 0)
    col_id = lax.broadcasted_iota(jnp.int32, (blk, blk), 1)
    key_rows = lambda j: pl.ds(pl.multiple_of(j * blk, blk), blk)
    first = (b * FOX_HEADS + g * pairs * 2) * nq + i
    j0 = jstart_ref[first]
    for h in range(1, 2 * pairs):
        j0 = jnp.minimum(j0, jstart_ref[first + h * nq])
    lanes_of = lambda pr: slice(pr * width, (pr + 1) * width)
    qh, c_q = [], []
    for pr in range(pairs):
        q2 = q_ref[:, lanes_of(pr)]
        for hh in range(2):
            qh.append(jnp.where((lane >= hh * FOX_HD) & (lane < (hh + 1) * FOX_HD), q2, jnp.zeros_like(q2)))
            c_q.append(cq_ref[0, pr, :, hh:hh + 1])

    def store(pr, outs):
        o_ref[:, lanes_of(pr)] = jnp.where(lane < FOX_HD, outs[0], outs[1]).astype(BF16)

    win = min(ATTN_WINDOW, nq)

    windowed = jnp.logical_and(i - j0 < win, i >= win - 1)

    @pl.when(jnp.logical_not(windowed))
    def _():
        for pr in range(pairs):
            def fill(j, carry, causal=False):
                out = []
                for hh in range(2):
                    m, l = carry[2 * hh], carry[2 * hh + 1]
                    s = ((_dot_nt(qh[2 * pr + hh], k_ref[key_rows(j), lanes_of(pr)]) + c_q[2 * pr + hh])
                         - c_ref[0, pr, hh:hh + 1, key_rows(j)])
                    if causal:
                        s = jnp.where(col_id <= row_id, s, NEG_INF)
                    s_scr[hh, j] = s
                    m_new = jnp.maximum(m, jnp.max(s, axis=-1, keepdims=True))
                    l = l * jnp.exp(m - m_new) + jnp.sum(jnp.exp(s - m_new), axis=-1, keepdims=True)
                    out += [m_new, l]
                return tuple(out)

            neg = jnp.full((blk, 1), NEG_INF, F32)
            zero = jnp.zeros((blk, 1), F32)
            stats = lax.fori_loop(j0, i, fill, (neg, zero, neg, zero))
            m0, l0, m1, l1 = fill(i, stats, causal=True)
            ms = (m0, m1)
            inv_l = (1.0 / l0, 1.0 / l1)

            def weigh(j, accs):
                vblk = v_ref[key_rows(j), lanes_of(pr)]
                return tuple(accs[hh] + _dot((jnp.exp(s_scr[hh, j] - ms[hh]) * inv_l[hh]).astype(BF16), vblk)
                             for hh in range(2))

            acc = jnp.zeros((blk, width), F32)
            store(pr, lax.fori_loop(j0, i + 1, weigh, (acc, acc)))

    @pl.when(windowed)
    def _():
        keys = pl.ds(pl.multiple_of((i - (win - 1)) * blk, blk), win * blk)
        older = (win - 1) * blk
        for pr in range(pairs):
            kwin, vwin = k_ref[keys, lanes_of(pr)], v_ref[keys, lanes_of(pr)]
            outs = []
            for hh in range(2):
                s = (_dot_nt(qh[2 * pr + hh], kwin) + c_q[2 * pr + hh]) - c_ref[0, pr, hh:hh + 1, keys]
                s = jnp.concatenate([s[:, :older], jnp.where(col_id <= row_id, s[:, older:], NEG_INF)], axis=1)
                e = jnp.exp(s - jnp.max(s, axis=-1, keepdims=True))
                inv = 1.0 / jnp.sum(e, axis=-1, keepdims=True)
                outs.append(_dot((e * inv).astype(BF16), vwin))
            store(pr, outs)


def _attn_prompt_call(jstart, qb, kb, vb, c, o_acc, *, batch, seq, blk):
    nq = seq // blk
    pairs = ATTN_PAIRS
    groups = FOX_PAIRS // pairs
    width = pairs * 2 * FOX_HD
    kv_spec = pl.BlockSpec((seq, width), lambda b, g, i, js: (b, g))
    c4 = c.reshape(batch, FOX_PAIRS, 2, seq)
    c_rows = c4.transpose(0, 1, 3, 2)
    return pl.pallas_call(
        functools.partial(_attn_prompt_kernel, blk=blk, nq=nq, pairs=pairs),
        grid_spec=pltpu.PrefetchScalarGridSpec(
            num_scalar_prefetch=1,
            grid=(batch, groups, nq),
            in_specs=[
                pl.BlockSpec((blk, width), lambda b, g, i, js: (b * nq + i, g)),
                kv_spec, kv_spec,
                pl.BlockSpec((1, pairs, 2, seq), lambda b, g, i, js: (b, g, 0, 0)),
                pl.BlockSpec((1, pairs, blk, 2), lambda b, g, i, js: (b, g, i, 0)),
                pl.BlockSpec(memory_space=pl.ANY),
            ],
            out_specs=pl.BlockSpec((blk, width), lambda b, g, i, js: (b * nq + i, g)),
            scratch_shapes=[pltpu.VMEM((2, nq, blk, blk), F32)],
        ),
        out_shape=jax.ShapeDtypeStruct(o_acc.shape, BF16),
        input_output_aliases={6: 0},
        compiler_params=_params("parallel", "parallel", "arbitrary"),
        name="attn_prompt",
    )(jstart, qb, kb, vb, c4, c_rows, o_acc)


def _attn_block_starts(sq_norms, c, *, batch, seq, blk):
    def norms(sq):
        return jnp.sqrt(sq).reshape(FOX_HEADS, batch, seq).transpose(1, 0, 2) * NORM_SLACK

    qn, kn = norms(sq_norms[:FOX_HEADS]), norms(sq_norms[FOX_HEADS:])
    nq = seq // blk
    blocks = lambda a: a.reshape(batch, FOX_HEADS, nq, blk)
    k_max = jnp.max(kn, axis=-1, keepdims=True)
    need = jnp.max(blocks(qn * (k_max + kn) + c), axis=-1) + EXP_UNDERFLOW
    c_min = jnp.min(blocks(c), axis=-1)
    skip = c_min[..., None, :] > need[..., :, None]
    lead = jnp.sum(jnp.cumprod(skip.astype(jnp.int32), axis=-1), axis=-1)
    start = jnp.minimum(lead, jnp.arange(nq, dtype=jnp.int32))
    return start.reshape(-1).astype(jnp.int32)


def _attn_decode_kernel(q_ref, kn_ref, vn_ref, kc_ref, vc_ref, cc_ref, cn_ref, cq_ref, o_acc, o_ref, *, tt, heads):
    del o_acc
    row_id = lax.broadcasted_iota(jnp.int32, (tt, tt), 0)
    col_id = lax.broadcasted_iota(jnp.int32, (tt, tt), 1)
    outs = []
    for hh in range(heads):
        cols = slice(hh * FOX_HD, (hh + 1) * FOX_HD)
        q = q_ref[:, cols]
        c_q = cq_ref[0, :, hh:hh + 1]
        s_c = (_dot(q, kc_ref[0, 0, hh].astype(BF16)) + c_q) - cc_ref[0, 0, hh:hh + 1, :]
        s_n = (_dot_nt(q, kn_ref[:, cols]) + c_q) - cn_ref[0, hh:hh + 1, :]
        s_n = jnp.where(col_id <= row_id, s_n, NEG_INF)
        m = jnp.maximum(jnp.max(s_c, axis=-1, keepdims=True), jnp.max(s_n, axis=-1, keepdims=True))
        e_c = jnp.exp(s_c - m)
        e_n = jnp.exp(s_n - m)
        inv_l = 1.0 / (jnp.sum(e_c, axis=-1, keepdims=True) + jnp.sum(e_n, axis=-1, keepdims=True))
        outs.append(_dot_nt((e_c * inv_l).astype(BF16), vc_ref[0, 0, hh].astype(BF16))
                    + _dot((e_n * inv_l).astype(BF16), vn_ref[:, cols]))
    o_ref[...] = jnp.concatenate(outs, axis=-1).astype(BF16)


def _attn_decode_call(qb, kb, vb, kc_t, vc_t, c_cache, c_new, o_acc, *, layer, row0, batch, tt):
    past = kc_t.shape[4]
    heads = DECODE_HEADS
    groups = FOX_HEADS // heads
    assert row0 % tt == 0
    r0 = row0 // tt
    tok_spec = pl.BlockSpec((tt, heads * FOX_HD), lambda b, p: (b, p))
    cache_spec = pl.BlockSpec((1, 1, heads, FOX_HD, past), lambda b, p: (layer, b, p, 0, 0))
    c_rows = c_new.transpose(0, 2, 1)
    return pl.pallas_call(
        functools.partial(_attn_decode_kernel, tt=tt, heads=heads),
        grid=(batch, groups),
        in_specs=[tok_spec, tok_spec, tok_spec, cache_spec, cache_spec,
                  pl.BlockSpec((1, 1, heads, past), lambda b, p: (layer, b, p, 0)),
                  pl.BlockSpec((1, heads, tt), lambda b, p: (b, p, 0)),
                  pl.BlockSpec((1, tt, heads), lambda b, p: (b, 0, p)),
                  pl.BlockSpec(memory_space=pl.ANY)],
        out_specs=pl.BlockSpec((tt, heads * FOX_HD), lambda b, p: (r0 + b, p)),
        out_shape=jax.ShapeDtypeStruct(o_acc.shape, BF16),
        input_output_aliases={8: 0},
        compiler_params=_params("parallel", "parallel"),
        name="attn_decode",
    )(qb, kb, vb, kc_t, vc_t, c_cache, c_new, c_rows, o_acc)


def _post_kernel(x_ref, ohg_ref, ofx_ref, mod_ref, nw_ref, w1_ref, w2_ref,
                 wr_ref, br_ref, sut_ref,
                 x1_ref, h2_ref, idx_ref, gate_ref, cnt_ref, cnt_s):
    i = pl.program_id(0)

    @pl.when(i == 0)
    def _():
        cnt_s[...] = jnp.zeros_like(cnt_s)

    y = _dot(ohg_ref[...], w1_ref[...]) + _dot(ofx_ref[...], w2_ref[...])
    x1 = _gated_residual(x_ref[...], y, mod_ref[0])
    x1_ref[...] = x1
    h2 = _modulate(_rms(x1) * nw_ref[...], mod_ref[2], mod_ref[1])
    h2_ref[...] = _pack_halves(h2)
    logits = _dot_nt(wr_ref[...], h2.astype(BF16)) + br_ref[...]
    e_f = lax.broadcasted_iota(jnp.int32, logits.shape, 0).astype(F32)
    work = logits
    vals, hots = [], []
    for _ in range(TOP_K):
        mx = jnp.max(work, axis=0, keepdims=True)
        ix = jnp.min(jnp.where(work == mx, e_f, float(N_EXPERTS)), axis=0, keepdims=True)
        hot = e_f == ix
        vals.append(mx)
        hots.append(hot)
        work = jnp.where(hot, -jnp.inf, work)
    exps = [jnp.exp(v - vals[0]) for v in vals]
    den = exps[0] + exps[1] + exps[2] + exps[3]
    hot_f = [h.astype(F32) for h in hots]
    hot_all = hot_f[0] + hot_f[1] + hot_f[2] + hot_f[3]
    before = cnt_s[:, 0:1] + _dot(hot_all.astype(BF16), sut_ref[...])
    for k in range(TOP_K):
        idx_ref[k:k + 1, :] = jnp.sum(hot_f[k] * e_f, axis=0, keepdims=True).astype(jnp.int32)
        idx_ref[TOP_K + k:TOP_K + k + 1, :] = jnp.sum(
            hot_f[k] * before, axis=0, keepdims=True).astype(jnp.int32)
        gate_ref[k:k + 1, :] = exps[k] / den
        gate_ref[TOP_K + k:TOP_K + k + 1, :] = jnp.zeros_like(den)
    total = cnt_s[...] + jnp.sum(hot_all, axis=1, keepdims=True)
    cnt_s[...] = total
    cnt_ref[...] = total


def _post_call(x, ohg, ofx, modg, nw, w1, w2, wr, br, sut):
    n, d = x.shape
    tm = ROW_TILE
    g = tm // MOD_GROUP
    const = lambda shape: pl.BlockSpec(shape, lambda i: (0,) * len(shape))
    row = lambda w: pl.BlockSpec((tm, w), lambda i: (i, 0))
    return pl.pallas_call(
        _post_kernel,
        grid=(n // tm,),
        in_specs=[
            row(d), row(HG_WIDTH), row(FOX_WIDTH),
            pl.BlockSpec((3, g, d), lambda i: (0, i, 0)),
            const((1, d)), const(w1.shape), const(w2.shape),
            const(wr.shape), const(br.shape), const(sut.shape),
        ],
        out_specs=[
            row(d), row(d // 2),
            pl.BlockSpec((2 * TOP_K, tm), lambda i: (0, i)),
            pl.BlockSpec((2 * TOP_K, tm), lambda i: (0, i)),
            const((N_EXPERTS, LANES)),
        ],
        out_shape=[
            jax.ShapeDtypeStruct((n, d), F32), jax.ShapeDtypeStruct((n, d // 2), U32),
            jax.ShapeDtypeStruct((2 * TOP_K, n), jnp.int32),
            jax.ShapeDtypeStruct((2 * TOP_K, n), F32),
            jax.ShapeDtypeStruct((N_EXPERTS, LANES), F32),
        ],
        scratch_shapes=[pltpu.VMEM((N_EXPERTS, LANES), F32)],
        compiler_params=_params("arbitrary"),
        name="post_mixer",
    )(x, ohg, ofx, modg, nw, w1, w2, wr, br, sut)


def _expert_kernel(blk_e_ref, n_used_ref, x_ref, wu_ref, bu_ref, wd_ref, bd_ref, perm_ref, o_ref, wu_s, wd_s):
    i = pl.program_id(0)
    used = i < n_used_ref[0]
    new_expert = jnp.logical_or(i == 0, blk_e_ref[i] != blk_e_ref[jnp.maximum(i - 1, 0)])

    @pl.when(jnp.logical_and(used, new_expert))
    def _():
        wd_s[...] = wd_ref[0, 0].astype(BF16)
        perm = perm_ref[...]
        f = wu_s.shape[1] // 2
        half = MXU_DIM // 2
        for t in range(wu_s.shape[1] // MXU_DIM):
            tile = wu_ref[0, 0, :, t * MXU_DIM:(t + 1) * MXU_DIM].astype(BF16)
            r = _dot(tile, perm).astype(BF16)
            wu_s[:, t * half:(t + 1) * half] = r[:, :half]
            wu_s[:, f + t * half:f + (t + 1) * half] = r[:, half:]

    @pl.when(used)
    def _():
        x = _unpack_halves(x_ref[...]).astype(BF16)
        u = _dot(x, wu_s[...]) + bu_ref[0]
        f = u.shape[1] // 2
        glu = jnp.minimum(u[:, :f], SWIGLU_LIMIT)
        lin = jnp.clip(u[:, f:], -SWIGLU_LIMIT, SWIGLU_LIMIT)
        a = glu * _sigmoid_pair(SWIGLU_ALPHA * glu)[0] * (lin + 1.0)
        o_ref[...] = _pack_halves(_dot(a.astype(BF16), wd_s[...]) + bd_ref[0, 0])

    @pl.when(jnp.logical_not(used))
    def _():
        o_ref[...] = jnp.zeros_like(o_ref)


def _expert_call(blk_e, n_used, xs, w_up, bu, w_down, b_down, *, layer):
    p, dh = xs.shape
    d = 2 * dh
    f2 = w_up.shape[3]
    tm = MOE_TILE
    j = jnp.arange(MXU_DIM, dtype=jnp.int32)
    half = MXU_DIM // 2
    src = jnp.where(j < half, 2 * j, 2 * (j - half) + 1)
    perm = (j[:, None] == src[None, :]).astype(BF16)
    return pl.pallas_call(
        _expert_kernel,
        grid_spec=pltpu.PrefetchScalarGridSpec(
            num_scalar_prefetch=2,
            grid=(p // tm,),
            in_specs=[
                pl.BlockSpec((tm, dh), lambda i, be, nu: (jnp.minimum(i, nu[0] - 1), 0)),
                pl.BlockSpec((1, 1, d, f2), lambda i, be, nu: (layer, be[i], 0, 0)),
                pl.BlockSpec((1, 1, f2), lambda i, be, nu: (be[i], 0, 0)),
                pl.BlockSpec((1, 1, f2 // 2, d), lambda i, be, nu: (layer, be[i], 0, 0)),
                pl.BlockSpec((1, 1, 1, d), lambda i, be, nu: (layer, be[i], 0, 0)),
                pl.BlockSpec((MXU_DIM, MXU_DIM), lambda i, be, nu: (0, 0)),
            ],
            out_specs=pl.BlockSpec((tm, dh), lambda i, be, nu: (i, 0)),
            scratch_shapes=[pltpu.VMEM((d, f2), BF16), pltpu.VMEM((f2 // 2, d), BF16)],
        ),
        out_shape=jax.ShapeDtypeStruct((p, dh), U32),
        compiler_params=_params("arbitrary"),
        name="experts",
    )(blk_e, n_used, xs, w_up, bu, w_down, b_down.reshape(b_down.shape[0], N_EXPERTS, 1, d), perm)


def _combine_kernel(x_ref, y_ref, gate_ref, mod_ref, nw_ref, o_ref, *, final):
    gates = gate_ref[...]
    y = gates[:, 0:1] * _unpack_halves(y_ref[0])
    for k in range(1, TOP_K):
        y = y + gates[:, k:k + 1] * _unpack_halves(y_ref[k])
    x2 = _gated_residual(x_ref[...], y, mod_ref[0])
    if final:
        x2 = _rms(x2) * nw_ref[...]
    o_ref[...] = x2


def _combine_call(x1, yg, gates, modg, nw, *, final, row0=0, rows=None):
    n, d = x1.shape
    rows = n if rows is None else rows
    tm = ROW_TILE
    g = tm // MOD_GROUP
    assert row0 % tm == 0 and rows % tm == 0
    i0 = row0 // tm
    return pl.pallas_call(
        functools.partial(_combine_kernel, final=final),
        grid=(rows // tm,),
        in_specs=[
            pl.BlockSpec((tm, d), lambda i: (i + i0, 0)),
            pl.BlockSpec((TOP_K, tm, d // 2), lambda i: (0, i + i0, 0)),
            pl.BlockSpec((tm, TOP_K), lambda i: (i + i0, 0)),
            pl.BlockSpec((1, g, d), lambda i: (5, i + i0, 0)),
            pl.BlockSpec((1, d), lambda i: (0, 0)),
        ],
        out_specs=pl.BlockSpec((tm, d), lambda i: (i, 0)),
        out_shape=jax.ShapeDtypeStruct((rows, d), F32),
        compiler_params=_params("parallel"),
        name="combine",
    )(x1, yg, gates, modg, nw)


def _hgrn2_lower_bounds(lb_logits):
    p = jax.nn.softmax(lb_logits.astype(F32), axis=0)
    return jnp.cumsum(p, axis=0) - p[0:1]


def _cumsum_last(x):
    t = x.shape[-1]
    nb = t // LANES
    xb = x.reshape(-1, nb, LANES)
    i = jnp.arange(LANES)
    upper = (i[:, None] <= i[None, :]).astype(F32)
    within = jnp.einsum("rnk,kj->rnj", xb, upper, precision=lax.Precision.HIGHEST)
    j = jnp.arange(nb)
    strict = (j[:, None] < j[None, :]).astype(F32)
    offset = jnp.dot(within[..., -1], strict, precision=lax.Precision.HIGHEST)
    return (within + offset[..., None]).reshape(x.shape)


def _route(idx8, cnt, n_tok):
    a = n_tok * TOP_K
    n_blocks = -(-a // MOE_TILE) + N_EXPERTS
    counts = cnt[:, 0].astype(jnp.int32)
    padded = (counts + MOE_TILE - 1) // MOE_TILE * MOE_TILE
    pends = jnp.cumsum(padded)
    pstarts = pends - padded
    e_ids = jnp.arange(N_EXPERTS, dtype=jnp.int32)
    base = jnp.sum(jnp.where(idx8[:TOP_K, :, None] == e_ids, pstarts, 0), axis=-1)
    dest = base + idx8[TOP_K:]
    token = jnp.broadcast_to(jnp.arange(n_tok, dtype=jnp.int32)[None, :], (TOP_K, n_tok))
    filler = jnp.arange(n_blocks * MOE_TILE, dtype=jnp.int32) % n_tok
    tok_buf = filler.at[dest.reshape(-1)].set(token.reshape(-1), unique_indices=True, mode="promise_in_bounds")
    blk_start = jnp.arange(n_blocks, dtype=jnp.int32) * MOE_TILE
    blk_e = jnp.minimum(jnp.sum(pends[None, :] <= blk_start[:, None], axis=1), N_EXPERTS - 1).astype(jnp.int32)
    n_used = (pends[-1:] // MOE_TILE).astype(jnp.int32)
    return dest, tok_buf, blk_e, n_used


def kernel(x_prompt, x_sample, c_prompt, c_sample, state_hgrn, cache_fox_k, cache_fox_v, cache_fox_lf,
           w_ada, b_ada, norm_mix, norm_ffn, w_in, lb_logits, hg_norm, fox_fb, w_out,
           w_router, b_router, w_up, b_up, w_down, b_down, norm_final):
    bp, tp, d = x_prompt.shape
    bs, ts, _ = x_sample.shape
    depth = w_in.shape[0]
    n_p, n_s = bp * tp, bs * ts
    n = n_p + n_s
    past = cache_fox_lf.shape[-1]
    assert tp % MOD_GROUP == 0 and ts == MOD_GROUP
    assert n_p % ROW_TILE == 0 and n_s % ROW_TILE == 0

    x = jnp.concatenate([x_prompt.reshape(n_p, d), x_sample.reshape(n_s, d)], axis=0)
    c = jnp.concatenate([c_prompt, c_sample], axis=0)
    bc = c.shape[0]
    c = jnp.pad(c, ((0, (-bc) % 8), (0, 0)))
    mod = _ada_call(c, w_ada, b_ada)
    mod_rows = jnp.concatenate([jnp.repeat(mod[:, :bp], tp // MOD_GROUP, axis=1), mod[:, bp:bp + bs]], axis=1)
    modg = mod_rows.reshape(depth, n // MOD_GROUP, 6, d).transpose(0, 2, 1, 3)

    lbs = _hgrn2_lower_bounds(lb_logits)
    t = jnp.arange(ROW_TILE, dtype=jnp.int32)
    sut = (t[:, None] < t[None, :]).astype(BF16)
    s0_prompt = jnp.zeros((bp, HG_HEADS, HG_DIM, HG_DIM), F32)
    f2 = w_up.shape[-1]
    blk = min(ATTN_BLOCK, tp)
    assert tp % blk == 0
    c_cache = _cumsum_last(cache_fox_lf.astype(F32))
    head_ind = (jnp.arange(FOX_WIDTH)[None, :] // FOX_HD == jnp.arange(FOX_HEADS)[:, None]).astype(BF16)

    kc_t = jnp.swapaxes(cache_fox_k, -1, -2)
    vc_t = jnp.swapaxes(cache_fox_v, -1, -2)
    k_p = jnp.zeros((depth, bp, FOX_HEADS, FOX_HD, tp), F32)
    v_p = jnp.zeros_like(k_p)
    k_s = jnp.zeros((depth, bs, FOX_HEADS, ts, FOX_HD), F32)
    v_s = jnp.zeros_like(k_s)
    new_s_p, new_lf_p, new_s_s, new_lf_s = [], [], [], []
    for l in range(depth):
        o4 = 4 * HG_WIDTH
        w_in_b = w_in[l].astype(BF16)
        whg = w_in_b[:, :o4]
        wfx = w_in_b[:, o4:o4 + 3 * FOX_WIDTH]
        wffT = w_in_b[:, o4 + 3 * FOX_WIDTH:].T
        fb = fox_fb[l].astype(F32).reshape(FOX_HEADS, 1)
        nw_mix = norm_mix[l].reshape(1, d)
        pre = functools.partial(_pre_call, x, modg[l], nw_mix, whg, wfx, wffT, fb, head_ind, layer=l)
        zhg_p, qb_p, kb_p, vb_p, k_p, v_p, lfT_p, sq_p = pre(k_p, v_p, row0=0, batch=bp, seq=tp)
        zhg_s, qb_s, kb_s, vb_s, k_s, v_s, lfT_s, _ = pre(k_s, v_s, row0=n_p, batch=bs, seq=ts)
        lf_p = lfT_p.reshape(FOX_HEADS, bp, tp).transpose(1, 0, 2)
        lf_s = lfT_s.reshape(FOX_HEADS, bs, ts).transpose(1, 0, 2)

        lb = lbs[l].reshape(1, HG_WIDTH)
        gn = hg_norm[l].astype(F32).reshape(1, HG_DIM)
        mix0 = jnp.zeros((n, HG_WIDTH), BF16)
        ohg, s_p = _hgrn_call(zhg_p, s0_prompt, lb, gn, mix0, row0=0, batch=bp, seq=tp)
        ohg, s_s = _hgrn_call(zhg_s, state_hgrn[l].astype(F32), lb, gn, ohg, row0=n_p, batch=bs, seq=ts)

        c_p = jnp.cumsum(lf_p, axis=-1)
        jstart = _attn_block_starts(sq_p, c_p, batch=bp, seq=tp, blk=blk)
        ofx = _attn_prompt_call(jstart, qb_p, kb_p, vb_p, c_p, jnp.zeros((n, FOX_WIDTH), BF16),
                                batch=bp, seq=tp, blk=blk)
        c_new = c_cache[l][..., -1:] + jnp.cumsum(lf_s, axis=-1)
        ofx = _attn_decode_call(qb_s, kb_s, vb_s, kc_t, vc_t, c_cache, c_new, ofx,
                                layer=l, row0=n_p, batch=bs, tt=ts)

        w_out_b = w_out[l].astype(BF16)
        x1, h2, idx8, gate8, cnt = _post_call(
            x, ohg, ofx, modg[l, 2:5], norm_ffn[l].reshape(1, d), w_out_b[:HG_WIDTH], w_out_b[HG_WIDTH:],
            w_router[l].T.astype(BF16), b_router[l].astype(F32).reshape(N_EXPERTS, 1), sut)

        dest, tok_buf, blk_e, n_used = _route(idx8, cnt, n)
        h2_big = jnp.concatenate([h2, jnp.zeros((tok_buf.shape[0] - n, d // 2), U32)], axis=0)
        xs = h2_big.at[tok_buf].get(mode="promise_in_bounds")
        bu = jnp.concatenate([b_up[l][:, 0::2], b_up[l][:, 1::2]], axis=-1).reshape(N_EXPERTS, 1, f2)
        yb = _expert_call(blk_e, n_used, xs, w_up, bu, w_down, b_down, layer=l)
        yg = yb.at[dest].get(mode="promise_in_bounds")
        combine = functools.partial(_combine_call, x1, yg, gate8[:TOP_K].T, modg[l], norm_final.reshape(1, d))
        if l < depth - 1:
            x = combine(final=False)
        else:
            y_prompt = combine(final=True, row0=0, rows=n_p).reshape(bp, tp, d)
            y_sample = combine(final=True, row0=n_p, rows=n_s).reshape(bs, ts, d)

        new_s_p.append(s_p); new_lf_p.append(lf_p)
        new_s_s.append(s_s); new_lf_s.append(lf_s)

    st = jnp.stack
    tr = lambda a: jnp.swapaxes(a, -1, -2)
    return (y_prompt, y_sample, st(new_s_p), tr(k_p), tr(v_p), st(new_lf_p),
            st(new_s_s), k_s, v_s, st(new_lf_s))
```
